```python
import jax
import jax.numpy as jnp
from jax import lax
import numpy as np

D_MODEL = 2048
BATCH = 1
SEQ = 8192
DEPTH = 2

GRID_W = 64
CTX_LEN = 256
HEAD_DIM = 128
MIX_W = D_MODEL // 2
POOL_GROUPS = 4
POOL_WINDOWS = (2, 4, 8, 16)
POOL_GROUP_W = MIX_W // POOL_GROUPS
G_Q_HEADS = MIX_W // HEAD_DIM
G_KV_HEADS = 2
W_Q_HEADS = MIX_W // HEAD_DIM
W_KV_HEADS = 2
WINDOW = 128
Q_BLOCK = 128
N_BRANCH = 3
ROPE_THETA = 10000.0
N_EXPERTS = 32
N_GROUPS = 4
EXPERTS_PER_GROUP = N_EXPERTS // N_GROUPS
TOP_K = 2
EXPERT_FF = D_MODEL // 2
MOE_BLOCK = 128
RMS_EPS = 1e-6
NEG_INF = -1e30
ATTN_SCALE = HEAD_DIM ** -0.5

OFF_GQ = MIX_W
OFF_GK = OFF_GQ + G_Q_HEADS * HEAD_DIM
OFF_GV = OFF_GK + G_KV_HEADS * HEAD_DIM
OFF_WQ = OFF_GV + G_KV_HEADS * HEAD_DIM
OFF_WK = OFF_WQ + W_Q_HEADS * HEAD_DIM
OFF_WV = OFF_WK + W_KV_HEADS * HEAD_DIM
OFF_GATE = OFF_WV + W_KV_HEADS * HEAD_DIM
IN_WIDTH = OFF_GATE + N_BRANCH * D_MODEL

kernel_name = 'hybrid_pool_gqa_swa_moe_dit'


def rmsnorm(x, g):
    xf = x.astype(jnp.float32)
    y = xf * lax.rsqrt(jnp.mean(xf * xf, axis=-1, keepdims=True) + RMS_EPS)
    return (y * g.astype(jnp.float32)).astype(x.dtype)


def modulate(h, shift, scale):
    return h * (1 + scale[:, None, :]) + shift[:, None, :]


def as_q(t, n_kv):
    B, L, _ = t.shape
    return t.reshape(B, L, n_kv, -1, HEAD_DIM)


def as_kv(t):
    B, L, _ = t.shape
    return t.reshape(B, L, -1, HEAD_DIM)


def proj_cols(h, w, a, b):
    return jnp.einsum('bld,de->ble', h, w[:, a:b])


def axial_rope_tables(rows):
    row_ids = jnp.repeat(jnp.arange(rows, dtype=jnp.float32), GRID_W)
    col_ids = jnp.tile(jnp.arange(GRID_W, dtype=jnp.float32), rows)
    n_freq = HEAD_DIM // 4
    inv = ROPE_THETA ** (-jnp.arange(n_freq, dtype=jnp.float32) / n_freq)
    ar = row_ids[:, None] * inv
    ac = col_ids[:, None] * inv
    ang = jnp.concatenate([ar, ar, ac, ac], axis=-1)
    return jnp.cos(ang), jnp.sin(ang)


def apply_axial_rope(x, cos, sin):
    L, dh = cos.shape
    shape = (L,) + (1,) * (x.ndim - 3) + (dh,)
    cs = cos.reshape(shape).astype(x.dtype)
    sn = sin.reshape(shape).astype(x.dtype)
    half, q4 = dh // 2, dh // 4
    xr, xc = x[..., :half], x[..., half:]
    rot = jnp.concatenate([-xr[..., q4:], xr[..., :q4], -xc[..., q4:], xc[..., :q4]], axis=-1)
    return x * cs + rot * sn


def attend(q, k, v, keep, sink):
    s = jnp.einsum('bqhgd,bshd->bhgqs', q, k, preferred_element_type=jnp.float32) * ATTN_SCALE
    if keep is not None:
        s = jnp.where(keep, s, NEG_INF)
    if sink is not None:
        sk = jnp.broadcast_to(sink.astype(jnp.float32).reshape(1, q.shape[2], q.shape[3], 1, 1),
                              s.shape[:-1] + (1,))
        p = jax.nn.softmax(jnp.concatenate([s, sk], axis=-1), axis=-1)[..., :-1]
    else:
        p = jax.nn.softmax(s, axis=-1)
    return jnp.einsum('bhgqs,bshd->bqhgd', p.astype(v.dtype), v)


def global_attention_latent(q, k_all, v_all):
    B, L, Hk, G, Dh = q.shape
    nb = L // Q_BLOCK
    qb = q.reshape(B, nb, Q_BLOCK, Hk, G, Dh).swapaxes(0, 1)
    out = lax.map(lambda qi: attend(qi, k_all, v_all, None, None), qb)
    return out.swapaxes(0, 1).reshape(B, L, Hk, G, Dh)


def window_attention_latent(q, k, v, k_ctx, v_ctx, sink):
    B, L, Hk, G, Dh = q.shape
    nb = L // Q_BLOCK
    span = 3 * Q_BLOCK
    pad = ((0, 0), (Q_BLOCK, Q_BLOCK), (0, 0), (0, 0))
    kp = jnp.pad(k, pad)
    vp = jnp.pad(v, pad)
    qb = q.reshape(B, nb, Q_BLOCK, Hk, G, Dh).swapaxes(0, 1)
    ctx_keep = jnp.ones((Q_BLOCK, k_ctx.shape[1]), dtype=bool)

    def block(args):
        qi, i = args
        start = i * Q_BLOCK
        kw = lax.dynamic_slice_in_dim(kp, start, span, axis=1)
        vw = lax.dynamic_slice_in_dim(vp, start, span, axis=1)
        qpos = start + jnp.arange(Q_BLOCK)
        kpos = start - Q_BLOCK + jnp.arange(span)
        band = ((kpos >= 0) & (kpos < L))[None, :] & (jnp.abs(qpos[:, None] - kpos[None, :]) <= WINDOW)
        keep = jnp.concatenate([ctx_keep, band], axis=1)
        return attend(qi, jnp.concatenate([k_ctx, kw], axis=1), jnp.concatenate([v_ctx, vw], axis=1), keep, sink)

    out = lax.map(block, (qb, jnp.arange(nb)))
    return out.swapaxes(0, 1).reshape(B, L, Hk, G, Dh)


def multiscale_pool(u, pool_w, pool_scale):
    B, L, _ = u.shape
    uf = u.astype(jnp.float32)
    cs = jnp.concatenate([jnp.zeros_like(uf[:, :1]), jnp.cumsum(uf, axis=1)], axis=1)
    t = jnp.arange(L)
    means = []
    for cs_g, w in zip(jnp.split(cs, POOL_GROUPS, axis=-1), POOL_WINDOWS):
        lo = jnp.maximum(t - (w - 1) // 2, 0)
        hi = jnp.minimum(t + w // 2 + 1, L)
        means.append((cs_g[:, hi] - cs_g[:, lo]) / (hi - lo).astype(jnp.float32)[None, :, None])
    d = (jnp.concatenate(means, axis=-1).astype(u.dtype) - u).reshape(B, L, POOL_GROUPS, POOL_GROUP_W)
    y = jnp.einsum('blgc,gcd->blgd', d, pool_w).reshape(B, L, MIX_W)
    return y * pool_scale


def merge_branches(pool_y, glob_o, win_o, gate_logits, w_branch, w_out):
    B, L, _ = pool_y.shape
    ys = jnp.stack([pool_y, glob_o.reshape(B, L, MIX_W), win_o.reshape(B, L, MIX_W)], axis=2)
    yb = jnp.einsum('blnc,ncd->blnd', ys, w_branch)
    g = jax.nn.sigmoid(gate_logits.reshape(B, L, N_BRANCH, D_MODEL))
    return jnp.einsum('bld,de->ble', jnp.sum(g * yb, axis=2), w_out)


def route(h, router_w, router_b):
    n = h.shape[0]
    scores = jax.nn.sigmoid(jnp.einsum('nd,de->ne', h, router_w, preferred_element_type=jnp.float32))
    sel = (scores + router_b.astype(jnp.float32)).reshape(n, N_GROUPS, EXPERTS_PER_GROUP)
    group_score = jnp.sum(lax.top_k(sel, TOP_K)[0], axis=-1)
    grp = jnp.argmax(group_score, axis=-1)
    in_group = sel[jnp.arange(n), grp]
    _, local = lax.top_k(in_group, TOP_K)
    idx = grp[:, None] * EXPERTS_PER_GROUP + local
    w = jnp.take_along_axis(scores, idx, axis=1)
    return idx, w / jnp.sum(w, axis=-1, keepdims=True)


def moe_ffn(h, router_w, router_b, w_gate, w_up, w_down):
    n, d = h.shape
    idx, w = route(h, router_w, router_b)
    m = n * TOP_K
    flat_e = idx.reshape(m)
    order = jnp.argsort(flat_e)
    sorted_e = flat_e[order]
    counts = jnp.bincount(flat_e, length=N_EXPERTS)
    padded = (counts + MOE_BLOCK - 1) // MOE_BLOCK * MOE_BLOCK
    start = jnp.cumsum(counts) - counts
    pad_end = jnp.cumsum(padded)
    pad_start = pad_end - padded
    dest = pad_start[sorted_e] + jnp.arange(m) - start[sorted_e]
    nblk = -(-m // MOE_BLOCK) + N_EXPERTS
    token = order // TOP_K
    rows = jnp.full((nblk * MOE_BLOCK,), n, dtype=token.dtype).at[dest].set(token)
    blk_e = jnp.minimum(jnp.searchsorted(pad_end, jnp.arange(nblk) * MOE_BLOCK, side='right'), N_EXPERTS - 1)
    h_pad = jnp.concatenate([h, jnp.zeros((1, d), h.dtype)], axis=0)
    xb = h_pad[rows].reshape(nblk, MOE_BLOCK, d)

    def expert_block(args):
        xi, e = args
        return (jax.nn.silu(xi @ w_gate[e]) * (xi @ w_up[e])) @ w_down[e]

    yb = lax.map(expert_block, (xb, blk_e)).reshape(nblk * MOE_BLOCK, d)
    y = yb[dest] * w.reshape(m)[order][:, None].astype(h.dtype)
    return jax.ops.segment_sum(y, token, num_segments=n)


def setup_inputs(seed: int = 0) -> dict:
    key = jax.random.key(seed)
    ks = jax.random.split(key, 24)
    D = D_MODEL

    def nrm(k, shape, scale):
        return jax.random.normal(k, shape, jnp.float32) * scale

    return {
        'x': nrm(ks[0], (BATCH, SEQ, D), 1.0),
        'c': nrm(ks[1], (BATCH, D), 1.0),
        'ctx': nrm(ks[2], (BATCH, CTX_LEN, D), 1.0),
        'c_ctx': nrm(ks[3], (D,), 1.0),
        'w_ada': nrm(ks[4], (DEPTH, D, 6 * D), 0.5 * D ** -0.5),
        'b_ada': nrm(ks[5], (DEPTH, 6 * D), 0.01),
        'norm1_g': 1.0 + nrm(ks[6], (DEPTH, D), 0.02),
        'norm2_g': 1.0 + nrm(ks[7], (DEPTH, D), 0.02),
        'w_in': nrm(ks[8], (DEPTH, D, IN_WIDTH), D ** -0.5),
        'pool_w': nrm(ks[9], (DEPTH, POOL_GROUPS, POOL_GROUP_W, POOL_GROUP_W), POOL_GROUP_W ** -0.5),
        'pool_scale': 1.0 + nrm(ks[10], (DEPTH, MIX_W), 0.1),
        'q_norm_g': 1.0 + nrm(ks[11], (DEPTH, HEAD_DIM), 0.02),
        'k_norm_g': 1.0 + nrm(ks[12], (DEPTH, HEAD_DIM), 0.02),
        'sink': nrm(ks[13], (DEPTH, W_Q_HEADS), 0.5),
        'w_branch': nrm(ks[14], (DEPTH, N_BRANCH, MIX_W, D), MIX_W ** -0.5),
        'w_out': nrm(ks[15], (DEPTH, D, D), D ** -0.5),
        'router_w': nrm(ks[16], (D, N_EXPERTS), D ** -0.5),
        'router_b': nrm(ks[17], (N_EXPERTS,), 0.01),
        'w_gate': nrm(ks[18], (DEPTH, N_EXPERTS, D, EXPERT_FF), D ** -0.5),
        'w_up': nrm(ks[19], (DEPTH, N_EXPERTS, D, EXPERT_FF), D ** -0.5),
        'w_down': nrm(ks[20], (DEPTH, N_EXPERTS, EXPERT_FF, D), EXPERT_FF ** -0.5),
        'final_g': 1.0 + nrm(ks[21], (D,), 0.02),
    }


def reference(x, c, ctx, c_ctx, w_ada, b_ada, norm1_g, norm2_g, w_in, pool_w, pool_scale,
              q_norm_g, k_norm_g, sink, w_branch, w_out, router_w, router_b, w_gate, w_up,
              w_down, final_g):
    B, L, D = x.shape
    C = ctx.shape[1]
    rows = L // GRID_W
    cos, sin = axial_rope_tables(rows)
    silu_c = jax.nn.silu(c)
    silu_cc = jax.nn.silu(c_ctx)
    for l in range(DEPTH):
        last = l == DEPTH - 1
        wl = w_in[l]
        mx = jnp.split(silu_c @ w_ada[l] + b_ada[l], 6, axis=-1)
        mc = jnp.split(jnp.broadcast_to(silu_cc @ w_ada[l] + b_ada[l], (B, 6 * D)), 6, axis=-1)
        sink_l = sink[l].reshape(W_KV_HEADS, W_Q_HEADS // W_KV_HEADS)

        hx = modulate(rmsnorm(x, norm1_g[l]), mx[0], mx[1])
        hc = modulate(rmsnorm(ctx, norm1_g[l]), mc[0], mc[1])
        p = jnp.einsum('bld,de->ble', hx, wl)
        ux = p[..., :OFF_GQ]
        gqx = apply_axial_rope(rmsnorm(as_q(p[..., OFF_GQ:OFF_GK], G_KV_HEADS), q_norm_g[l]), cos, sin)
        gkx = apply_axial_rope(rmsnorm(as_kv(p[..., OFF_GK:OFF_GV]), k_norm_g[l]), cos, sin)
        gvx = as_kv(p[..., OFF_GV:OFF_WQ])
        wqx = apply_axial_rope(as_q(p[..., OFF_WQ:OFF_WK], W_KV_HEADS), cos, sin)
        wkx = apply_axial_rope(as_kv(p[..., OFF_WK:OFF_WV]), cos, sin)
        wvx = as_kv(p[..., OFF_WV:OFF_GATE])
        glx = p[..., OFF_GATE:]
        gkc = rmsnorm(as_kv(proj_cols(hc, wl, OFF_GK, OFF_GV)), k_norm_g[l])
        gvc = as_kv(proj_cols(hc, wl, OFF_GV, OFF_WQ))
        wkc = as_kv(proj_cols(hc, wl, OFF_WK, OFF_WV))
        wvc = as_kv(proj_cols(hc, wl, OFF_WV, OFF_GATE))

        glob_x = global_attention_latent(gqx, jnp.concatenate([gkc, gkx], axis=1), jnp.concatenate([gvc, gvx], axis=1))
        win_x = window_attention_latent(wqx, wkx, wvx, wkc, wvc, sink_l)
        pool_x = multiscale_pool(ux, pool_w[l], pool_scale[l])
        x = x + mx[2][:, None, :] * merge_branches(pool_x, glob_x, win_x, glx, w_branch[l], w_out[l])

        if not last:
            uc = proj_cols(hc, wl, 0, OFF_GQ)
            gqc = rmsnorm(as_q(proj_cols(hc, wl, OFF_GQ, OFF_GK), G_KV_HEADS), q_norm_g[l])
            wqc = as_q(proj_cols(hc, wl, OFF_WQ, OFF_WK), W_KV_HEADS)
            glc = proj_cols(hc, wl, OFF_GATE, IN_WIDTH)
            glob_c = attend(gqc, gkc, gvc, None, None)
            win_c = attend(wqc, wkc, wvc, None, sink_l)
            pool_c = multiscale_pool(uc, pool_w[l], pool_scale[l])
            ctx = ctx + mc[2][:, None, :] * merge_branches(pool_c, glob_c, win_c, glc, w_branch[l], w_out[l])

        hx2 = modulate(rmsnorm(x, norm2_g[l]), mx[3], mx[4]).reshape(B * L, D)
        if last:
            yx = moe_ffn(hx2, router_w, router_b, w_gate[l], w_up[l], w_down[l])
        else:
            hc2 = modulate(rmsnorm(ctx, norm2_g[l]), mc[3], mc[4]).reshape(B * C, D)
            y = moe_ffn(jnp.concatenate([hc2, hx2], axis=0), router_w, router_b, w_gate[l], w_up[l], w_down[l])
            ctx = ctx + mc[5][:, None, :] * y[:B * C].reshape(B, C, D)
            yx = y[B * C:]
        x = x + mx[5][:, None, :] * yx.reshape(B, L, D)

    return rmsnorm(x, final_g)
```

```python
import functools

import jax
import jax.numpy as jnp
from jax import lax
from jax.experimental import pallas as pl
from jax.experimental.pallas import tpu as pltpu

HEAD_DIM = 128
LANES = 128
BF16_ROWS = 16
KV_HEADS = 2
GRID_W = 64
WINDOW = 128
POOL_WINDOWS = (2, 4, 8, 16)
POOL_HALO = 16
N_GROUPS = 4
ROPE_THETA = 10000.0
RMS_EPS = 1e-6
NEG_INF = -1e30
ATTN_SCALE = HEAD_DIM ** -0.5
MOE_BLOCK = 256
VMEM_CAP = 60 * 1024 * 1024

BF16 = jnp.bfloat16
F32 = jnp.float32


def _pick(n, prefs):
    for p in prefs:
        if n % p == 0:
            return p
    raise ValueError(f"no tile in {prefs} divides {n}")


def _params(n_axes, vmem_bytes):
    limit = int(min(max(vmem_bytes * 5 // 4 + (4 << 20), 16 << 20), VMEM_CAP))
    return pltpu.CompilerParams(dimension_semantics=("arbitrary",) * n_axes, vmem_limit_bytes=limit)


def _sigmoid(x):
    return 1.0 / (1.0 + jnp.exp(-x))


def _dot(a, b):
    return jnp.dot(a, b, preferred_element_type=F32)


def _dot_nt(a, b):
    return lax.dot_general(a, b, (((1,), (1,)), ((), ())), preferred_element_type=F32)


def _row_select(row0, n_rows, n_ctx, mod_ref):
    rows = row0 + lax.broadcasted_iota(jnp.int32, (n_rows, 1), 0)
    return jnp.where(rows < n_ctx, mod_ref[1:2, :], mod_ref[0:1, :])


def _norm_modulate(x, g_ref, shift, scale):
    ms = jnp.mean(x * x, axis=-1, keepdims=True)
    y = x * lax.rsqrt(ms + RMS_EPS) * g_ref[...]
    return y * (1.0 + scale) + shift


def _ada_kernel(cb_ref, w_ref, b_ref, o_ref, *, tn):
    s = cb_ref[...]
    s = s * _sigmoid(s)
    outs = []
    for r in range(2):
        cols = [jnp.sum(w_ref[:, c * LANES:(c + 1) * LANES] * s[r], axis=0, keepdims=True)
                for c in range(tn // LANES)]
        outs.append(jnp.concatenate(cols, axis=1) + b_ref[...])
    o_ref[...] = jnp.concatenate(outs + [jnp.zeros((6, tn), F32)], axis=0)


def _ada(c, c_ctx, w_ada, b_ada):
    depth, d, w6 = w_ada.shape
    tn = _pick(w6, (1024, 512, 256, 128))
    cb = jnp.broadcast_to(jnp.stack([c[0], c_ctx])[:, :, None], (2, d, LANES))
    vmem = 2 * d * tn * 4 + 2 * d * LANES * 4 * 2 + d * LANES * 4 * 4
    return pl.pallas_call(
        functools.partial(_ada_kernel, tn=tn),
        grid=(depth, w6 // tn),
        in_specs=[pl.BlockSpec((2, d, LANES), lambda l, j: (0, 0, 0)),
                  pl.BlockSpec((None, d, tn), lambda l, j: (l, 0, j)),
                  pl.BlockSpec((None, 1, tn), lambda l, j: (l, 0, j))],
        out_specs=pl.BlockSpec((None, 8, tn), lambda l, j: (l, 0, j)),
        out_shape=jax.ShapeDtypeStruct((depth, 8, w6), F32),
        compiler_params=_params(2, vmem),
        name="ada",
    )(cb, w_ada, b_ada.reshape(depth, 1, w6))


def _rope(x, cos, sin_signed, first_half):
    rot = jnp.where(first_half, pltpu.roll(x, 3 * HEAD_DIM // 4, axis=1), pltpu.roll(x, HEAD_DIM // 4, axis=1))
    return x * cos + rot * sin_signed


def _head_rms(x, g_ref):
    ms = jnp.mean(x * x, axis=-1, keepdims=True)
    return x * lax.rsqrt(ms + RMS_EPS) * g_ref[...]


def _inproj_kernel(x_ref, shift_ref, scale_ref, g_ref, w_ref, cos_ref, sin_ref, qg_ref, kg_ref,
                   o_ref, h_sc, *, tm, tn, n_ctx, signatures):
    i = pl.program_id(0)
    j = pl.program_id(1)

    @pl.when(j == 0)
    def _():
        shift = _row_select(i * tm, tm, n_ctx, shift_ref)
        scale = _row_select(i * tm, tm, n_ctx, scale_ref)
        h_sc[...] = _norm_modulate(x_ref[...], g_ref, shift, scale).astype(BF16)

    acc = _dot(h_sc[...], w_ref[...].astype(BF16))

    lane = lax.broadcasted_iota(jnp.int32, (1, HEAD_DIM), 1)
    first_half = (lane % (HEAD_DIM // 2)) < (HEAD_DIM // 4)

    for sig, js in signatures:
        cond = functools.reduce(jnp.logical_or, [j == jj for jj in js])

        @pl.when(cond)
        def _(sig=sig):
            if all(t == "plain" for t in sig):
                o_ref[...] = acc.astype(BF16)
                return
            if all(t == "sig" for t in sig):
                o_ref[...] = _sigmoid(acc).astype(BF16)
                return
            for bi, typ in enumerate(sig):
                a = acc[:, bi * LANES:(bi + 1) * LANES]
                if typ == "sig":
                    a = _sigmoid(a)
                elif typ != "plain":
                    if typ in ("qnr", "knr"):
                        a = _head_rms(a, qg_ref if typ == "qnr" else kg_ref)
                    a = _rope(a, cos_ref[...], sin_ref[...], first_half)
                    if typ in ("qnr", "qr"):
                        a = a * ATTN_SCALE
                o_ref[:, bi * LANES:(bi + 1) * LANES] = a.astype(BF16)


def _col_types(mix, d):
    hq = mix // HEAD_DIM
    types = (["plain"] * hq + ["qnr"] * hq + ["knr"] * KV_HEADS + ["plain"] * KV_HEADS
             + ["qr"] * hq + ["kr"] * KV_HEADS + ["plain"] * KV_HEADS + ["sig"] * (3 * d // LANES))
    return types


def _inproj(xs, mod_l, g1, w_in, layer, cos, sin_signed, qg, kg, n_ctx):
    t, d = xs.shape
    in_w = w_in.shape[-1]
    mix = d // 2
    tm = _pick(t, (768, 512, 256, 128))
    tn = _pick(in_w, (1024, 512, 256))
    types = _col_types(mix, d)
    assert len(types) * LANES == in_w
    per = tn // LANES
    sigs = {}
    for jj in range(in_w // tn):
        sigs.setdefault(tuple(types[jj * per:(jj + 1) * per]), []).append(jj)
    signatures = tuple((s, tuple(js)) for s, js in sigs.items())
    vmem = (2 * tm * d * 4 + 2 * d * tn * 4 + tm * d * 2 + 2 * tm * tn * 2 + tm * tn * 4 * 2 + d * tn * 2
            + 4 * tm * LANES * 4)
    return pl.pallas_call(
        functools.partial(_inproj_kernel, tm=tm, tn=tn, n_ctx=n_ctx, signatures=signatures),
        grid=(t // tm, in_w // tn),
        in_specs=[pl.BlockSpec((tm, d), lambda i, j: (i, 0)),
                  pl.BlockSpec((8, d), lambda i, j: (0, 0)),
                  pl.BlockSpec((8, d), lambda i, j: (0, 1)),
                  pl.BlockSpec((1, d), lambda i, j: (0, 0)),
                  pl.BlockSpec((None, d, tn), lambda i, j: (layer, 0, j)),
                  pl.BlockSpec((tm, HEAD_DIM), lambda i, j: (i, 0)),
                  pl.BlockSpec((tm, HEAD_DIM), lambda i, j: (i, 0)),
                  pl.BlockSpec((1, HEAD_DIM), lambda i, j: (0, 0)),
                  pl.BlockSpec((1, HEAD_DIM), lambda i, j: (0, 0))],
        out_specs=pl.BlockSpec((tm, tn), lambda i, j: (i, j)),
        out_shape=jax.ShapeDtypeStruct((t, in_w), BF16),
        scratch_shapes=[pltpu.VMEM((tm, d), BF16)],
        compiler_params=_params(2, vmem),
        name="inproj",
    )(xs, mod_l, mod_l, g1, w_in, cos, sin_signed, qg, kg)


def _stack_heads(q_ref, g):
    return jnp.concatenate([q_ref[:, h * HEAD_DIM:(h + 1) * HEAD_DIM] for h in range(g)], axis=0)


def _unstack_store(o_ref, out, g, tq):
    for h in range(g):
        o_ref[:, h * HEAD_DIM:(h + 1) * HEAD_DIM] = out[h * tq:(h + 1) * tq, :].astype(o_ref.dtype)


def _gattn_kernel(q_ref, k_ref, v_ref, o_ref, m_sc, l_sc, acc_sc, *, tq, tk, g, n_ctx, n_lat):
    qi = pl.program_id(1)
    q = _stack_heads(q_ref, g)

    def step(k, v):
        s = _dot_nt(q, k)
        m_prev = m_sc[...]
        m_new = jnp.maximum(m_prev, jnp.max(s, axis=-1, keepdims=True))
        alpha = jnp.exp(m_prev - m_new)
        p = jnp.exp(s - m_new)
        l_sc[...] = alpha * l_sc[...] + jnp.sum(p, axis=-1, keepdims=True)
        acc_sc[...] = alpha * acc_sc[...] + _dot(p.astype(BF16), v)
        m_sc[...] = m_new

    m_sc[...] = jnp.full(m_sc.shape, -jnp.inf, F32)
    l_sc[...] = jnp.zeros(l_sc.shape, F32)
    acc_sc[...] = jnp.zeros(acc_sc.shape, F32)
    step(k_ref[0:n_ctx, :], v_ref[0:n_ctx, :])

    n_chunks = jnp.where(qi * tq < n_ctx, 0, n_lat // tk)

    def body(c, carry):
        off = pl.multiple_of(n_ctx + c * tk, LANES)
        step(k_ref[pl.ds(off, tk), :], v_ref[pl.ds(off, tk), :])
        return carry

    lax.fori_loop(0, n_chunks, body, 0)
    _unstack_store(o_ref, acc_sc[...] / l_sc[...], g, tq)


def _gattn(p, n_ctx, mix):
    t = p.shape[0]
    hq = mix // HEAD_DIM
    g = hq // KV_HEADS
    n_lat = t - n_ctx
    tq = _pick(n_ctx, (256, 128))
    assert t % tq == 0
    tk = _pick(n_lat, (512, 256, 128))
    gw = g * HEAD_DIM
    off_q = mix // gw
    off_k = (mix + hq * HEAD_DIM) // HEAD_DIM
    off_v = off_k + KV_HEADS
    rows = g * tq
    vmem = (2 * tq * gw * 2 * 2 + 2 * 2 * t * HEAD_DIM * 2 + rows * LANES * 4 * 3
            + rows * max(tk, n_ctx) * 4 * 3)
    return pl.pallas_call(
        functools.partial(_gattn_kernel, tq=tq, tk=tk, g=g, n_ctx=n_ctx, n_lat=n_lat),
        grid=(KV_HEADS, t // tq),
        in_specs=[pl.BlockSpec((tq, gw), lambda h, i: (i, off_q + h)),
                  pl.BlockSpec((t, HEAD_DIM), lambda h, i: (0, off_k + h)),
                  pl.BlockSpec((t, HEAD_DIM), lambda h, i: (0, off_v + h))],
        out_specs=pl.BlockSpec((tq, gw), lambda h, i: (i, h)),
        out_shape=jax.ShapeDtypeStruct((t, mix), BF16),
        scratch_shapes=[pltpu.VMEM((rows, 1), F32), pltpu.VMEM((rows, 1), F32), pltpu.VMEM((rows, HEAD_DIM), F32)],
        compiler_params=_params(2, vmem),
        name="gattn",
    )(p, p, p)


def _wattn_kernel(q_ref, k_ref, v_ref, sink_ref, o_ref, *, tq, g, n_ctx, span, t):
    qi = pl.program_id(1)
    q = _stack_heads(q_ref, g)
    start = pl.multiple_of(jnp.clip(qi * tq - WINDOW, 0, t - span), LANES)
    kw = k_ref[pl.ds(start, span), :]
    vw = v_ref[pl.ds(start, span), :]
    kc = k_ref[0:n_ctx, :]
    vc = v_ref[0:n_ctx, :]

    qrow = qi * tq + lax.broadcasted_iota(jnp.int32, (tq, 1), 0)
    qrow = jnp.concatenate([qrow] * g, axis=0)
    krow = start + lax.broadcasted_iota(jnp.int32, (1, span), 1)
    keep = (qrow >= n_ctx) & (krow >= n_ctx) & (jnp.abs(qrow - krow) <= WINDOW)

    s_c = _dot_nt(q, kc)
    s_w = jnp.where(keep, _dot_nt(q, kw), NEG_INF)
    sink = sink_ref[...]
    m = jnp.maximum(jnp.maximum(jnp.max(s_c, axis=-1, keepdims=True), jnp.max(s_w, axis=-1, keepdims=True)), sink)
    p_c = jnp.exp(s_c - m)
    p_w = jnp.exp(s_w - m)
    denom = jnp.sum(p_c, axis=-1, keepdims=True) + jnp.sum(p_w, axis=-1, keepdims=True) + jnp.exp(sink - m)
    out = (_dot(p_c.astype(BF16), vc) + _dot(p_w.astype(BF16), vw)) / denom
    _unstack_store(o_ref, out, g, tq)


def _wattn(p, sink_l, n_ctx, mix):
    t = p.shape[0]
    hq = mix // HEAD_DIM
    g = hq // KV_HEADS
    tq = _pick(n_ctx, (256, 128))
    span = tq + 2 * WINDOW
    assert t % tq == 0 and t >= span
    gw = g * HEAD_DIM
    base = mix + hq * HEAD_DIM + 2 * KV_HEADS * HEAD_DIM
    off_q = base // gw
    off_k = (base + hq * HEAD_DIM) // HEAD_DIM
    off_v = off_k + KV_HEADS
    rows = g * tq
    sink_rows = jnp.repeat(sink_l.reshape(KV_HEADS, g), tq, axis=1).reshape(KV_HEADS, rows, 1)
    vmem = (2 * tq * gw * 2 * 2 + 2 * 2 * t * HEAD_DIM * 2 + 2 * rows * LANES * 4
            + rows * (span + n_ctx) * 4 * 3)
    return pl.pallas_call(
        functools.partial(_wattn_kernel, tq=tq, g=g, n_ctx=n_ctx, span=span, t=t),
        grid=(KV_HEADS, t // tq),
        in_specs=[pl.BlockSpec((tq, gw), lambda h, i: (i, off_q + h)),
                  pl.BlockSpec((t, HEAD_DIM), lambda h, i: (0, off_k + h)),
                  pl.BlockSpec((t, HEAD_DIM), lambda h, i: (0, off_v + h)),
                  pl.BlockSpec((None, rows, 1), lambda h, i: (h, 0, 0))],
        out_specs=pl.BlockSpec((tq, gw), lambda h, i: (i, h)),
        out_shape=jax.ShapeDtypeStruct((t, mix), BF16),
        compiler_params=_params(2, vmem),
        name="wattn",
    )(p, p, p, sink_rows)


def _merge_kernel(u_ref, up_ref, un_ref, og_ref, ow_ref, g0_ref, g1_ref, g2_ref, pw_ref, ps_ref, wb_ref,
                  z_ref, ext_sc, pool_sc, *, tm, n_ctx, t, mix):
    i = pl.program_id(0)
    j = pl.program_id(1)
    gw = mix // len(POOL_WINDOWS)

    @pl.when(j == 0)
    def _():
        ext_sc[0:POOL_HALO, :] = up_ref[...].astype(F32)
        ext_sc[POOL_HALO:POOL_HALO + tm, :] = u_ref[...].astype(F32)
        ext_sc[POOL_HALO + tm:, :] = un_ref[...].astype(F32)
        r = i * tm + lax.broadcasted_iota(jnp.int32, (tm, 1), 0)
        r_ctx = r < n_ctx
        for gi, w in enumerate(POOL_WINDOWS):
            c0, c1 = gi * gw, (gi + 1) * gw
            tot = jnp.zeros((tm, gw), F32)
            cnt = jnp.zeros((tm, 1), F32)
            for off in range(-((w - 1) // 2), w // 2 + 1):
                rr = r + off
                ok = (rr >= 0) & (rr < t) & ((rr < n_ctx) == r_ctx)
                tot = tot + jnp.where(ok, ext_sc[POOL_HALO + off:POOL_HALO + off + tm, c0:c1], 0.0)
                cnt = cnt + ok.astype(F32)
            dlt = tot / cnt - ext_sc[POOL_HALO:POOL_HALO + tm, c0:c1]
            y = _dot(dlt.astype(BF16), pw_ref[gi].astype(BF16)) * ps_ref[:, c0:c1]
            pool_sc[:, c0:c1] = y.astype(BF16)

    z = g0_ref[...].astype(F32) * _dot(pool_sc[...], wb_ref[0].astype(BF16))
    z = z + g1_ref[...].astype(F32) * _dot(og_ref[...], wb_ref[1].astype(BF16))
    z = z + g2_ref[...].astype(F32) * _dot(ow_ref[...], wb_ref[2].astype(BF16))
    z_ref[...] = z.astype(BF16)


def _merge(p, og, ow, pool_w, pool_scale, w_branch, layer, n_ctx, d):
    t = p.shape[0]
    mix = d // 2
    tm = _pick(t, (768, 512, 256, 128))
    tn = _pick(d, (512, 256, 128))
    hb = tm // POOL_HALO
    n_hb = t // POOL_HALO
    gate0 = (p.shape[1] - 3 * d) // tn
    gs = pool_w.shape[-1]
    vmem = (2 * 3 * tm * mix * 2 + 2 * 3 * tm * tn * 2 + 2 * 3 * mix * tn * 4 + 3 * mix * tn * 2
            + (tm + 2 * POOL_HALO) * mix * 4 + tm * mix * 2 + 2 * tm * tn * 2 + 4 * tm * tn * 4
            + 2 * len(POOL_WINDOWS) * gs * gs * 4 + 6 * tm * gs * 4)
    return pl.pallas_call(
        functools.partial(_merge_kernel, tm=tm, n_ctx=n_ctx, t=t, mix=mix),
        grid=(t // tm, d // tn),
        in_specs=[pl.BlockSpec((tm, mix), lambda i, j: (i, 0)),
                  pl.BlockSpec((POOL_HALO, mix), lambda i, j: (jnp.maximum(i * hb - 1, 0), 0)),
                  pl.BlockSpec((POOL_HALO, mix), lambda i, j: (jnp.minimum((i + 1) * hb, n_hb - 1), 0)),
                  pl.BlockSpec((tm, mix), lambda i, j: (i, 0)),
                  pl.BlockSpec((tm, mix), lambda i, j: (i, 0)),
                  pl.BlockSpec((tm, tn), lambda i, j: (i, gate0 + j)),
                  pl.BlockSpec((tm, tn), lambda i, j: (i, gate0 + d // tn + j)),
                  pl.BlockSpec((tm, tn), lambda i, j: (i, gate0 + 2 * (d // tn) + j)),
                  pl.BlockSpec((None, len(POOL_WINDOWS), gs, gs), lambda i, j: (layer, 0, 0, 0)),
                  pl.BlockSpec((1, mix), lambda i, j: (0, 0)),
                  pl.BlockSpec((None, 3, mix, tn), lambda i, j: (layer, 0, 0, j))],
        out_specs=pl.BlockSpec((tm, tn), lambda i, j: (i, j)),
        out_shape=jax.ShapeDtypeStruct((t, d), BF16),
        scratch_shapes=[pltpu.VMEM((tm + 2 * POOL_HALO, mix), F32), pltpu.VMEM((tm, mix), BF16)],
        compiler_params=_params(2, vmem),
        name="merge",
    )(p, p, p, og, ow, p, p, p, pool_w, pool_scale, w_branch)


def _outproj_kernel(z_ref, w_ref, xs_ref, gate_ref, o_ref, *, tm, n_ctx):
    i = pl.program_id(0)
    gate = _row_select(i * tm, tm, n_ctx, gate_ref)
    o_ref[...] = xs_ref[...] + gate * _dot(z_ref[...], w_ref[...].astype(BF16))


def _outproj(z, w_out, layer, xs, mod_l, n_ctx):
    t, d = xs.shape
    tm = _pick(t, (768, 512, 256, 128))
    tn = _pick(d, (512, 256, 128))
    vmem = 2 * tm * d * 2 + 2 * d * tn * 4 + d * tn * 2 + 4 * tm * tn * 4 + 2 * tm * tn * 4
    return pl.pallas_call(
        functools.partial(_outproj_kernel, tm=tm, n_ctx=n_ctx),
        grid=(t // tm, d // tn),
        in_specs=[pl.BlockSpec((tm, d), lambda i, j: (i, 0)),
                  pl.BlockSpec((None, d, tn), lambda i, j: (layer, 0, j)),
                  pl.BlockSpec((tm, tn), lambda i, j: (i, j)),
                  pl.BlockSpec((8, tn), lambda i, j: (0, 2 * (d // tn) + j))],
        out_specs=pl.BlockSpec((tm, tn), lambda i, j: (i, j)),
        out_shape=jax.ShapeDtypeStruct((t, d), F32),
        compiler_params=_params(2, vmem),
        name="outproj",
    )(z, w_out, xs, mod_l)


def _router_kernel(xs_ref, shift_ref, scale_ref, g_ref, rwt_ref, rb_ref,
                   h_ref, ints_ref, ws_ref, cnt_ref, carry_sc, *, tm, n_ctx, n_exp):
    i = pl.program_id(0)

    @pl.when(i == 0)
    def _():
        carry_sc[...] = jnp.zeros(carry_sc.shape, F32)

    shift = _row_select(i * tm, tm, n_ctx, shift_ref)
    scale = _row_select(i * tm, tm, n_ctx, scale_ref)
    h = _norm_modulate(xs_ref[...], g_ref, shift, scale)
    h_ref[...] = h

    scores = _sigmoid(_dot_nt(rwt_ref[...].astype(BF16), h.astype(BF16)))
    sel = scores + rb_ref[...]
    per = n_exp // N_GROUPS
    sub = lax.broadcasted_iota(jnp.int32, (per, tm), 0)

    def top2(v):
        m1 = jnp.max(v, axis=0, keepdims=True)
        i1 = jnp.min(jnp.where(v == m1, sub, per), axis=0, keepdims=True)
        rest = jnp.where(sub == i1, -jnp.inf, v)
        m2 = jnp.max(rest, axis=0, keepdims=True)
        i2 = jnp.min(jnp.where(rest == m2, sub, per), axis=0, keepdims=True)
        return m1 + m2, i1, i2

    tops = [top2(sel[gi * per:(gi + 1) * per, :]) for gi in range(N_GROUPS)]
    best, l1, l2 = tops[0]
    grp = jnp.zeros((1, tm), jnp.int32)
    for gi in range(1, N_GROUPS):
        gs, a1, a2 = tops[gi]
        better = gs > best
        best = jnp.where(better, gs, best)
        grp = jnp.where(better, gi, grp)
        l1 = jnp.where(better, a1, l1)
        l2 = jnp.where(better, a2, l2)

    hot1 = [(grp == gi) & (sub == l1) for gi in range(N_GROUPS)]
    hot2 = [(grp == gi) & (sub == l2) for gi in range(N_GROUPS)]
    assign = jnp.concatenate([(a | b).astype(F32) for a, b in zip(hot1, hot2)], axis=0)

    before = (lax.broadcasted_iota(jnp.int32, (tm, tm), 0) < lax.broadcasted_iota(jnp.int32, (tm, tm), 1))
    pos = _dot(assign.astype(BF16), before.astype(F32).astype(BF16)) + carry_sc[...]

    def pick(hots, val):
        return sum(jnp.sum(jnp.where(hots[gi], val[gi * per:(gi + 1) * per, :], 0.0), axis=0, keepdims=True)
                   for gi in range(N_GROUPS))

    s1 = pick(hot1, scores)
    s2 = pick(hot2, scores)
    r1 = pick(hot1, pos)
    r2 = pick(hot2, pos)
    tot = s1 + s2
    ints_ref[0:1, :] = grp * per + l1
    ints_ref[1:2, :] = grp * per + l2
    ints_ref[2:3, :] = r1.astype(jnp.int32)
    ints_ref[3:4, :] = r2.astype(jnp.int32)
    ints_ref[4:8, :] = jnp.zeros((4, tm), jnp.int32)
    ws_ref[0:1, :] = s1 / tot
    ws_ref[1:2, :] = s2 / tot
    ws_ref[2:8, :] = jnp.zeros((6, tm), F32)
    carry_sc[...] = carry_sc[...] + jnp.sum(assign, axis=1, keepdims=True)
    cnt_ref[...] = jnp.broadcast_to(carry_sc[...], cnt_ref.shape)


def _router(xs, mod_l, g2, router_w, router_b, n_ctx):
    t, d = xs.shape
    n_exp = router_w.shape[1]
    tm = _pick(t, (768, 512, 256, 128))
    vmem = 2 * tm * d * 4 * 2 + 4 * tm * d * 4 + 3 * tm * tm * 4 + 2 * n_exp * d * 4
    return pl.pallas_call(
        functools.partial(_router_kernel, tm=tm, n_ctx=n_ctx, n_exp=n_exp),
        grid=(t // tm,),
        in_specs=[pl.BlockSpec((tm, d), lambda i: (i, 0)),
                  pl.BlockSpec((8, d), lambda i: (0, 3)),
                  pl.BlockSpec((8, d), lambda i: (0, 4)),
                  pl.BlockSpec((1, d), lambda i: (0, 0)),
                  pl.BlockSpec((n_exp, d), lambda i: (0, 0)),
                  pl.BlockSpec((n_exp, 1), lambda i: (0, 0))],
        out_specs=[pl.BlockSpec((tm, d), lambda i: (i, 0)),
                   pl.BlockSpec((8, tm), lambda i: (0, i)),
                   pl.BlockSpec((8, tm), lambda i: (0, i)),
                   pl.BlockSpec((n_exp, LANES), lambda i: (0, 0))],
        out_shape=[jax.ShapeDtypeStruct((t, d), F32),
                   jax.ShapeDtypeStruct((8, t), jnp.int32),
                   jax.ShapeDtypeStruct((8, t), F32),
                   jax.ShapeDtypeStruct((n_exp, LANES), F32)],
        scratch_shapes=[pltpu.VMEM((n_exp, 1), F32)],
        compiler_params=_params(1, vmem),
        name="router",
    )(xs, mod_l, mod_l, g2, router_w.T, router_b.reshape(n_exp, 1))


def _row_copy(src_ref, src_row, dst_ref, dst_row, sem):
    return pltpu.make_async_copy(src_ref.at[pl.ds(src_row, 1)], dst_ref.at[pl.ds(dst_row, 1)], sem)


def _dispatch_kernel(pad_start_ref, idx_ref, h_ref, xs_hbm, sem, *, tm):
    def issue(r, carry):
        for k in range(2):
            dst = pad_start_ref[idx_ref[0, 0, k * tm + r]] + idx_ref[0, 0, (2 + k) * tm + r]
            _row_copy(h_ref, r, xs_hbm, dst, sem).start()
        return carry

    lax.fori_loop(0, tm, issue, 0)
    for k in range(2):
        pltpu.make_async_copy(h_ref, xs_hbm.at[pl.ds(0, tm)], sem).wait()


def _block_indices(ints, tm):
    t = ints.shape[1]
    return ints[:4].reshape(4, t // tm, tm).transpose(1, 0, 2).reshape(t // tm, 1, 4 * tm)


def _dispatch(h2, idx_blocks, pad_start, n_rows, tm):
    t, d = h2.shape
    grid_spec = pltpu.PrefetchScalarGridSpec(
        num_scalar_prefetch=1,
        grid=(t // tm,),
        in_specs=[pl.BlockSpec((1, 1, 4 * tm), lambda i, ps: (i, 0, 0), memory_space=pltpu.SMEM),
                  pl.BlockSpec((tm, d), lambda i, ps: (i, 0))],
        out_specs=pl.BlockSpec(memory_space=pl.ANY),
        scratch_shapes=[pltpu.SemaphoreType.DMA(())],
    )
    return pl.pallas_call(
        functools.partial(_dispatch_kernel, tm=tm),
        grid_spec=grid_spec,
        out_shape=jax.ShapeDtypeStruct((n_rows, d), F32),
        compiler_params=_params(1, 2 * tm * d * 4),
        name="dispatch",
    )(pad_start, idx_blocks, h2)


def _expert_changed(b, blk_e_ref):
    return (b == 0) | (blk_e_ref[b] != blk_e_ref[jnp.maximum(b - 1, 0)])


def _ffn_up_kernel(blk_e_ref, blk_valid_ref, n_used_ref, x_ref, wg_ref, wu_ref, h_ref, wg_sc, wu_sc):
    b = pl.program_id(0)

    @pl.when(b < n_used_ref[0])
    def _():
        @pl.when(_expert_changed(b, blk_e_ref))
        def _():
            wg_sc[...] = wg_ref[...].astype(BF16)
            wu_sc[...] = wu_ref[...].astype(BF16)

        rows = lax.broadcasted_iota(jnp.int32, (x_ref.shape[0], 1), 0)
        x = jnp.where(rows < blk_valid_ref[b], x_ref[...], 0.0).astype(BF16)
        gte = _dot(x, wg_sc[...])
        up = _dot(x, wu_sc[...])
        h_ref[...] = (gte * _sigmoid(gte) * up).astype(BF16)


def _ffn_down_kernel(blk_e_ref, blk_valid_ref, n_used_ref, h_ref, wd_ref, y_ref, wd_sc):
    b = pl.program_id(0)

    @pl.when(b < n_used_ref[0])
    def _():
        @pl.when(_expert_changed(b, blk_e_ref))
        def _():
            wd_sc[...] = wd_ref[...].astype(BF16)

        y_ref[...] = _dot(h_ref[...], wd_sc[...])


def _used(b, n_used_ref):
    return jnp.minimum(b, n_used_ref[0] - 1)


def _ffn(xs_sorted, blk_e, blk_valid, n_used, w_gate, w_up, w_down, layer):
    n_rows, d = xs_sorted.shape
    ff = w_gate.shape[-1]
    blk = MOE_BLOCK
    n_blk = n_rows // blk
    row_map = lambda b, e, v, n: (_used(b, n), 0)
    w_map = lambda b, e, v, n: (layer, e[b], 0, 0)
    up_spec = pltpu.PrefetchScalarGridSpec(
        num_scalar_prefetch=3,
        grid=(n_blk,),
        in_specs=[pl.BlockSpec((blk, d), row_map),
                  pl.BlockSpec((None, None, d, ff), w_map),
                  pl.BlockSpec((None, None, d, ff), w_map)],
        out_specs=pl.BlockSpec((blk, ff), row_map),
        scratch_shapes=[pltpu.VMEM((d, ff), BF16), pltpu.VMEM((d, ff), BF16)],
    )
    hidden = pl.pallas_call(
        _ffn_up_kernel,
        grid_spec=up_spec,
        out_shape=jax.ShapeDtypeStruct((n_rows, ff), BF16),
        compiler_params=_params(1, 2 * blk * d * 4 + 4 * d * ff * 4 + 2 * d * ff * 2 + 2 * blk * ff * 2
                                + 4 * blk * ff * 4 + blk * d * 2),
        name="ffn_up",
    )(blk_e, blk_valid, n_used, xs_sorted, w_gate, w_up)
    down_spec = pltpu.PrefetchScalarGridSpec(
        num_scalar_prefetch=3,
        grid=(n_blk,),
        in_specs=[pl.BlockSpec((blk, ff), row_map),
                  pl.BlockSpec((None, None, ff, d), w_map)],
        out_specs=pl.BlockSpec((blk, d), row_map),
        scratch_shapes=[pltpu.VMEM((ff, d), BF16)],
    )
    return pl.pallas_call(
        _ffn_down_kernel,
        grid_spec=down_spec,
        out_shape=jax.ShapeDtypeStruct((n_rows, d), F32),
        compiler_params=_params(1, 2 * blk * ff * 2 + 2 * d * ff * 4 + d * ff * 2 + 3 * blk * d * 4),
        name="ffn_down",
    )(blk_e, blk_valid, n_used, hidden, w_down)


def _combine_kernel(pad_start_ref, idx_ref, y_hbm, xs_ref, ws_ref, gate_ref, fg_ref, o_ref, ybuf, sem,
                    *, tm, n_ctx, final):
    i = pl.program_id(0)

    def issue(r, carry):
        for k in range(2):
            src = pad_start_ref[idx_ref[0, 0, k * tm + r]] + idx_ref[0, 0, (2 + k) * tm + r]
            _row_copy(y_hbm, src, ybuf.at[k], r, sem).start()
        return carry

    lax.fori_loop(0, tm, issue, 0)
    for k in range(2):
        pltpu.make_async_copy(y_hbm.at[pl.ds(0, tm)], ybuf.at[k], sem).wait()

    wcol = jnp.transpose(ws_ref[...])
    y = wcol[:, 0:1] * ybuf[0] + wcol[:, 1:2] * ybuf[1]
    x = xs_ref[...] + _row_select(i * tm, tm, n_ctx, gate_ref) * y
    if final:
        ms = jnp.mean(x * x, axis=-1, keepdims=True)
        x = x * lax.rsqrt(ms + RMS_EPS) * fg_ref[...]
    o_ref[...] = x


def _combine(y_sorted, idx_blocks, pad_start, xs, ws, mod_l, final_g, n_ctx, tm, final):
    t, d = xs.shape
    if final:
        skip = n_ctx // tm
        out_rows = t - n_ctx
        out_map = lambda i, ps: (jnp.maximum(i - skip, 0), 0)
    else:
        out_rows = t
        out_map = lambda i, ps: (i, 0)
    grid_spec = pltpu.PrefetchScalarGridSpec(
        num_scalar_prefetch=1,
        grid=(t // tm,),
        in_specs=[pl.BlockSpec((1, 1, 4 * tm), lambda i, ps: (i, 0, 0), memory_space=pltpu.SMEM),
                  pl.BlockSpec(memory_space=pl.ANY),
                  pl.BlockSpec((tm, d), lambda i, ps: (i, 0)),
                  pl.BlockSpec((8, tm), lambda i, ps: (0, i)),
                  pl.BlockSpec((8, d), lambda i, ps: (0, 5)),
                  pl.BlockSpec((1, d), lambda i, ps: (0, 0))],
        out_specs=pl.BlockSpec((tm, d), out_map),
        scratch_shapes=[pltpu.VMEM((2, tm, d), F32), pltpu.SemaphoreType.DMA(())],
    )
    return pl.pallas_call(
        functools.partial(_combine_kernel, tm=tm, n_ctx=n_ctx, final=final),
        grid_spec=grid_spec,
        out_shape=jax.ShapeDtypeStruct((out_rows, d), F32),
        compiler_params=_params(1, 2 * tm * d * 4 + 4 * tm * d * 4 + 4 * tm * d * 4),
        name="combine",
    )(pad_start, idx_blocks, y_sorted, xs, ws, mod_l, final_g)


def _rope_tables(n_lat, n_ctx):
    rows = n_lat // GRID_W
    row_ids = jnp.repeat(jnp.arange(rows, dtype=F32), GRID_W)
    col_ids = jnp.tile(jnp.arange(GRID_W, dtype=F32), rows)
    n_freq = HEAD_DIM // 4
    inv = ROPE_THETA ** (-jnp.arange(n_freq, dtype=F32) / n_freq)
    ar = row_ids[:, None] * inv
    ac = col_ids[:, None] * inv
    ang = jnp.concatenate([ar, ar, ac, ac], axis=-1)
    sign = jnp.where((jnp.arange(HEAD_DIM) % (HEAD_DIM // 2)) < HEAD_DIM // 4, -1.0, 1.0).astype(F32)
    cos = jnp.concatenate([jnp.ones((n_ctx, HEAD_DIM), F32), jnp.cos(ang)], axis=0)
    sin = jnp.concatenate([jnp.zeros((n_ctx, HEAD_DIM), F32), jnp.sin(ang) * sign], axis=0)
    return cos, sin


def _moe_plan(counts, n_blk):
    blk = MOE_BLOCK
    n_exp = counts.shape[0]
    padded = (counts + blk - 1) // blk * blk
    pad_end = jnp.cumsum(padded)
    pad_start = pad_end - padded
    n_used = jnp.maximum(pad_end[-1] // blk, 1)
    b = jnp.minimum(jnp.arange(n_blk, dtype=jnp.int32), n_used - 1)
    blk_e = jnp.sum((pad_end[None, :] <= (b * blk)[:, None]).astype(jnp.int32), axis=1)
    blk_e = jnp.minimum(blk_e, n_exp - 1)
    blk_valid = jnp.clip(counts[blk_e] - (b * blk - pad_start[blk_e]), 0, blk).astype(jnp.int32)
    return pad_start.astype(jnp.int32), blk_e, blk_valid, n_used.astype(jnp.int32).reshape(1)


def kernel(x, c, ctx, c_ctx, w_ada, b_ada, norm1_g, norm2_g, w_in, pool_w, pool_scale, q_norm_g, k_norm_g,
           sink, w_branch, w_out, router_w, router_b, w_gate, w_up, w_down, final_g):
    assert x.shape[0] == 1, "single-sequence kernel"
    n_lat, d = x.shape[1], x.shape[2]
    n_ctx = ctx.shape[1]
    t = n_ctx + n_lat
    mix = d // 2
    depth = w_in.shape[0]
    n_exp = router_w.shape[1]
    tok = _pick(n_ctx, (256, 128))
    assert (2 * t) % MOE_BLOCK == 0
    n_blk = 2 * t // MOE_BLOCK + n_exp
    cos, sin_signed = _rope_tables(n_lat, n_ctx)
    mod = _ada(c, c_ctx, w_ada, b_ada)
    xs = jnp.concatenate([ctx[0], x[0]], axis=0)
    for l in range(depth):
        last = l == depth - 1
        p = _inproj(xs, mod[l], norm1_g[l][None], w_in, l, cos, sin_signed,
                    q_norm_g[l][None], k_norm_g[l][None], n_ctx)
        og = _gattn(p, n_ctx, mix)
        ow = _wattn(p, sink[l], n_ctx, mix)
        z = _merge(p, og, ow, pool_w, pool_scale[l][None], w_branch, l, n_ctx, d)
        xs = _outproj(z, w_out, l, xs, mod[l], n_ctx)
        h2, ints, ws, cnt = _router(xs, mod[l], norm2_g[l][None], router_w, router_b, n_ctx)
        pad_start, blk_e, blk_valid, n_used = _moe_plan(cnt[:, 0].astype(jnp.int32), n_blk)
        idx_blocks = _block_indices(ints, tok)
        xs_sorted = _dispatch(h2, idx_blocks, pad_start, n_blk * MOE_BLOCK, tok)
        y_sorted = _ffn(xs_sorted, blk_e, blk_valid, n_used, w_gate, w_up, w_down, l)
        xs = _combine(y_sorted, idx_blocks, pad_start, xs, ws, mod[l], final_g[None], n_ctx, tok, last)
    return xs[None]
```

```python
import functools

import jax
import jax.numpy as jnp
from jax import lax
from jax.experimental import pallas as pl
from jax.experimental.pallas import tpu as pltpu

HEAD_DIM = 128
LANES = 128
BF16_ROWS = 16
KV_HEADS = 2
GRID_W = 64
WINDOW = 128
POOL_WINDOWS = (2, 4, 8, 16)
POOL_HALO = 16
N_GROUPS = 4
ROPE_THETA = 10000.0
RMS_EPS = 1e-6
NEG_INF = -1e30
ATTN_SCALE = HEAD_DIM ** -0.5
LOG2E = 1.4426950408889634
MOE_BLOCK = 256
DMA_UNROLL = 8
VMEM_CAP = 60 * 1024 * 1024

BF16 = jnp.bfloat16
F32 = jnp.float32


def _pick(n, prefs):
    for p in prefs:
        if n % p == 0:
            return p
    raise ValueError(f"no tile in {prefs} divides {n}")


def _largest_tile(n, cap, mult):
    return max(k for k in range(mult, min(n, cap) + 1, mult) if n % k == 0)


def _params(n_axes, vmem_bytes):
    limit = int(min(max(vmem_bytes * 5 // 4 + (4 << 20), 16 << 20), VMEM_CAP))
    return pltpu.CompilerParams(dimension_semantics=("arbitrary",) * n_axes, vmem_limit_bytes=limit)


def _sigmoid(x):
    return 1.0 / (1.0 + jnp.exp(-x))


def _sigmoid_tanh(x):
    return 0.5 * jnp.tanh(0.5 * x) + 0.5


def _dot(a, b):
    return jnp.dot(a, b, preferred_element_type=F32)


def _dot_nt(a, b):
    return lax.dot_general(a, b, (((1,), (1,)), ((), ())), preferred_element_type=F32)


def _row_select(row0, n_rows, n_ctx, mod_ref):
    rows = row0 + lax.broadcasted_iota(jnp.int32, (n_rows, 1), 0)
    return jnp.where(rows < n_ctx, mod_ref[1:2, :], mod_ref[0:1, :])


def _norm_modulate(x, g_ref, shift, scale):
    ms = jnp.mean(x * x, axis=-1, keepdims=True)
    y = x * lax.rsqrt(ms + RMS_EPS) * g_ref[...]
    return y * (1.0 + scale) + shift


def _ada_kernel(cb_ref, w_ref, b_ref, o_ref, *, tn):
    s = cb_ref[...]
    s = s * _sigmoid(s)
    outs = []
    for r in range(2):
        cols = [jnp.sum(w_ref[:, c * LANES:(c + 1) * LANES] * s[r], axis=0, keepdims=True)
                for c in range(tn // LANES)]
        outs.append(jnp.concatenate(cols, axis=1) + b_ref[...])
    o_ref[...] = jnp.concatenate(outs + [jnp.zeros((6, tn), F32)], axis=0)


def _ada(c, c_ctx, w_ada, b_ada):
    depth, d, w6 = w_ada.shape
    tn = _pick(w6, (1024, 512, 256, 128))
    cb = jnp.broadcast_to(jnp.stack([c[0], c_ctx])[:, :, None], (2, d, LANES))
    vmem = 2 * d * tn * 4 + 2 * d * LANES * 4 * 2 + d * LANES * 4 * 4
    return pl.pallas_call(
        functools.partial(_ada_kernel, tn=tn),
        grid=(depth, w6 // tn),
        in_specs=[pl.BlockSpec((2, d, LANES), lambda l, j: (0, 0, 0)),
                  pl.BlockSpec((None, d, tn), lambda l, j: (l, 0, j)),
                  pl.BlockSpec((None, 1, tn), lambda l, j: (l, 0, j))],
        out_specs=pl.BlockSpec((None, 8, tn), lambda l, j: (l, 0, j)),
        out_shape=jax.ShapeDtypeStruct((depth, 8, w6), F32),
        compiler_params=_params(2, vmem),
        name="ada",
    )(cb, w_ada, b_ada.reshape(depth, 1, w6))


def _rope(x, cos, sin_signed, first_half):
    rot = jnp.where(first_half, pltpu.roll(x, 3 * HEAD_DIM // 4, axis=1), pltpu.roll(x, HEAD_DIM // 4, axis=1))
    return x * cos + rot * sin_signed


def _head_rms(x, g_ref):
    ms = jnp.mean(x * x, axis=-1, keepdims=True)
    return x * lax.rsqrt(ms + RMS_EPS) * g_ref[...]


def _inproj_kernel(x_ref, shift_ref, scale_ref, g_ref, w_ref, cos_ref, sin_ref, qg_ref, kg_ref,
                   o_ref, h_sc, *, tm, tn, n_ctx, signatures):
    i = pl.program_id(0)
    j = pl.program_id(1)

    def normed():
        x = x_ref[...]
        return x * lax.rsqrt(jnp.mean(x * x, axis=-1, keepdims=True) + RMS_EPS)

    has_ctx_rows = i * tm < n_ctx

    @pl.when((j == 0) & has_ctx_rows)
    def _():
        gain = g_ref[...] * (1.0 + scale_ref[0:2, :])
        rows = i * tm + lax.broadcasted_iota(jnp.int32, (tm, 1), 0)
        is_ctx = rows < n_ctx
        h = normed() * jnp.where(is_ctx, gain[1:2], gain[0:1]) + jnp.where(is_ctx, shift_ref[1:2, :], shift_ref[0:1, :])
        h_sc[...] = h.astype(BF16)

    @pl.when((j == 0) & jnp.logical_not(has_ctx_rows))
    def _():
        h = normed() * (g_ref[...] * (1.0 + scale_ref[0:1, :])) + shift_ref[0:1, :]
        h_sc[...] = h.astype(BF16)

    acc = _dot(h_sc[...], w_ref[...].astype(BF16))

    lane = lax.broadcasted_iota(jnp.int32, (1, HEAD_DIM), 1)
    first_half = (lane % (HEAD_DIM // 2)) < (HEAD_DIM // 4)

    for sig, js in signatures:
        cond = functools.reduce(jnp.logical_or, [j == jj for jj in js])

        @pl.when(cond)
        def _(sig=sig):
            if all(t == "plain" for t in sig):
                o_ref[...] = acc.astype(BF16)
                return
            if all(t == "sig" for t in sig):
                o_ref[...] = _sigmoid_tanh(acc).astype(BF16)
                return
            for bi, typ in enumerate(sig):
                a = acc[:, bi * LANES:(bi + 1) * LANES]
                if typ == "sig":
                    a = _sigmoid_tanh(a)
                elif typ != "plain":
                    if typ in ("qnr", "knr"):
                        a = _head_rms(a, qg_ref if typ == "qnr" else kg_ref)
                    a = _rope(a, cos_ref[...], sin_ref[...], first_half)
                    if typ == "qnr":
                        a = a * (ATTN_SCALE * LOG2E)
                    elif typ == "qr":
                        a = a * ATTN_SCALE
                o_ref[:, bi * LANES:(bi + 1) * LANES] = a.astype(BF16)


def _col_types(mix, d):
    hq = mix // HEAD_DIM
    types = (["plain"] * hq + ["qnr"] * hq + ["knr"] * KV_HEADS + ["plain"] * KV_HEADS
             + ["qr"] * hq + ["kr"] * KV_HEADS + ["plain"] * KV_HEADS + ["sig"] * (3 * d // LANES))
    return types


def _inproj(xs, mod_l, g1, w_in, layer, cos, sin_signed, qg, kg, n_ctx):
    t, d = xs.shape
    in_w = w_in.shape[-1]
    mix = d // 2
    tm = _pick(t, (768, 512, 256, 128))
    tn = _pick(in_w, (1024, 512, 256))
    types = _col_types(mix, d)
    assert len(types) * LANES == in_w
    per = tn // LANES
    sigs = {}
    for jj in range(in_w // tn):
        sigs.setdefault(tuple(types[jj * per:(jj + 1) * per]), []).append(jj)
    signatures = tuple((s, tuple(js)) for s, js in sigs.items())
    vmem = (2 * tm * d * 4 + 2 * d * tn * 4 + tm * d * 2 + 2 * tm * tn * 2 + tm * tn * 4 * 2 + d * tn * 2
            + 4 * tm * LANES * 4)
    return pl.pallas_call(
        functools.partial(_inproj_kernel, tm=tm, tn=tn, n_ctx=n_ctx, signatures=signatures),
        grid=(t // tm, in_w // tn),
        in_specs=[pl.BlockSpec((tm, d), lambda i, j: (i, 0)),
                  pl.BlockSpec((8, d), lambda i, j: (0, 0)),
                  pl.BlockSpec((8, d), lambda i, j: (0, 1)),
                  pl.BlockSpec((1, d), lambda i, j: (0, 0)),
                  pl.BlockSpec((None, d, tn), lambda i, j: (layer, 0, j)),
                  pl.BlockSpec((tm, HEAD_DIM), lambda i, j: (i, 0)),
                  pl.BlockSpec((tm, HEAD_DIM), lambda i, j: (i, 0)),
                  pl.BlockSpec((1, HEAD_DIM), lambda i, j: (0, 0)),
                  pl.BlockSpec((1, HEAD_DIM), lambda i, j: (0, 0))],
        out_specs=pl.BlockSpec((tm, tn), lambda i, j: (i, j)),
        out_shape=jax.ShapeDtypeStruct((t, in_w), BF16),
        scratch_shapes=[pltpu.VMEM((tm, d), BF16)],
        compiler_params=_params(2, vmem),
        name="inproj",
    )(xs, mod_l, mod_l, g1, w_in, cos, sin_signed, qg, kg)


def _lane_repeat(x, n):
    return jnp.concatenate([x] * n, axis=1)


def _stack_heads(q_ref, g):
    return jnp.concatenate([q_ref[:, h * HEAD_DIM:(h + 1) * HEAD_DIM] for h in range(g)], axis=0)


def _unstack_store(o_ref, out, g, tq):
    for h in range(g):
        o_ref[:, h * HEAD_DIM:(h + 1) * HEAD_DIM] = out[h * tq:(h + 1) * tq, :].astype(o_ref.dtype)


def _gattn_kernel(q_ref, k_ref, v_ref, o_ref, vx_sc, m_sc, acc_sc, *, tq, tk, g, n_ctx, n_lat):
    qi = pl.program_id(1)

    @pl.when(qi == 0)
    def _():
        vx_sc[:, 0:HEAD_DIM] = v_ref[...]
        vx_sc[:, HEAD_DIM:] = jnp.ones((vx_sc.shape[0], HEAD_DIM), BF16)

    q = _stack_heads(q_ref, g)
    rows = q.shape[0]

    def step(lo, size, first):
        s = _dot_nt(q, k_ref[lo:lo + size, :])
        mx = jnp.max(s, axis=-1, keepdims=True)
        if first:
            m_new = jnp.broadcast_to(mx, (rows, LANES))
        else:
            m_prev = m_sc[...]
            m_new = jnp.maximum(m_prev, mx)
        p = jnp.exp2(s - _lane_repeat(m_new, size // LANES))
        pv = _dot(p.astype(BF16), vx_sc[lo:lo + size, :])
        if first:
            acc_sc[...] = pv
        else:
            alpha = jnp.exp2(m_prev - m_new)
            acc_sc[...] = _lane_repeat(alpha, 2) * acc_sc[...] + pv
        m_sc[...] = m_new

    step(0, n_ctx, True)

    @pl.when(qi * tq >= n_ctx)
    def _():
        for c in range(n_lat // tk):
            step(n_ctx + c * tk, tk, False)

    acc = acc_sc[...]
    _unstack_store(o_ref, acc[:, :HEAD_DIM] / acc[:, HEAD_DIM:], g, tq)


def _gattn(p, n_ctx, mix):
    t = p.shape[0]
    hq = mix // HEAD_DIM
    g = hq // KV_HEADS
    n_lat = t - n_ctx
    tq = _pick(n_ctx, (256, 128))
    assert t % tq == 0
    tk = _pick(n_lat, (2048, 1024, 512, 256, 128))
    gw = g * HEAD_DIM
    off_q = mix // gw
    off_k = (mix + hq * HEAD_DIM) // HEAD_DIM
    off_v = off_k + KV_HEADS
    rows = g * tq
    vmem = (2 * tq * gw * 2 * 2 + 2 * 2 * t * HEAD_DIM * 2 + t * 2 * HEAD_DIM * 2 + rows * LANES * 4 * 4
            + rows * max(tk, n_ctx) * 4 * 3)
    return pl.pallas_call(
        functools.partial(_gattn_kernel, tq=tq, tk=tk, g=g, n_ctx=n_ctx, n_lat=n_lat),
        grid=(KV_HEADS, t // tq),
        in_specs=[pl.BlockSpec((tq, gw), lambda h, i: (i, off_q + h)),
                  pl.BlockSpec((t, HEAD_DIM), lambda h, i: (0, off_k + h)),
                  pl.BlockSpec((t, HEAD_DIM), lambda h, i: (0, off_v + h))],
        out_specs=pl.BlockSpec((tq, gw), lambda h, i: (i, h)),
        out_shape=jax.ShapeDtypeStruct((t, mix), BF16),
        scratch_shapes=[pltpu.VMEM((t, 2 * HEAD_DIM), BF16), pltpu.VMEM((rows, LANES), F32),
                        pltpu.VMEM((rows, 2 * HEAD_DIM), F32)],
        compiler_params=_params(2, vmem),
        name="gattn",
    )(p, p, p)


def _wattn_kernel(q_ref, k_ref, v_ref, sink_ref, o_ref, *, tq, g, n_ctx, span, t):
    qi = pl.program_id(1)
    q = _stack_heads(q_ref, g)
    start = pl.multiple_of(jnp.clip(qi * tq - WINDOW, 0, t - span), LANES)
    kw = k_ref[pl.ds(start, span), :]
    vw = v_ref[pl.ds(start, span), :]
    kc = k_ref[0:n_ctx, :]
    vc = v_ref[0:n_ctx, :]

    qrow = qi * tq + lax.broadcasted_iota(jnp.int32, (tq, 1), 0)
    qrow = jnp.concatenate([qrow] * g, axis=0)
    krow = start + lax.broadcasted_iota(jnp.int32, (1, span), 1)
    keep = (qrow >= n_ctx) & (krow >= n_ctx) & (jnp.abs(qrow - krow) <= WINDOW)

    s_c = _dot_nt(q, kc)
    s_w = jnp.where(keep, _dot_nt(q, kw), NEG_INF)
    sink = sink_ref[...]
    m = jnp.maximum(jnp.maximum(jnp.max(s_c, axis=-1, keepdims=True), jnp.max(s_w, axis=-1, keepdims=True)), sink)
    p_c = jnp.exp(s_c - m)
    p_w = jnp.exp(s_w - m)
    denom = jnp.sum(p_c, axis=-1, keepdims=True) + jnp.sum(p_w, axis=-1, keepdims=True) + jnp.exp(sink - m)
    out = (_dot(p_c.astype(BF16), vc) + _dot(p_w.astype(BF16), vw)) / denom
    _unstack_store(o_ref, out, g, tq)


def _wattn(p, sink_l, n_ctx, mix):
    t = p.shape[0]
    hq = mix // HEAD_DIM
    g = hq // KV_HEADS
    tq = _pick(n_ctx, (256, 128))
    span = tq + 2 * WINDOW
    assert t % tq == 0 and t >= span
    gw = g * HEAD_DIM
    base = mix + hq * HEAD_DIM + 2 * KV_HEADS * HEAD_DIM
    off_q = base // gw
    off_k = (base + hq * HEAD_DIM) // HEAD_DIM
    off_v = off_k + KV_HEADS
    rows = g * tq
    sink_rows = jnp.repeat(sink_l.reshape(KV_HEADS, g), tq, axis=1).reshape(KV_HEADS, rows, 1)
    vmem = (2 * tq * gw * 2 * 2 + 2 * 2 * t * HEAD_DIM * 2 + 2 * rows * LANES * 4
            + rows * (span + n_ctx) * 4 * 3)
    return pl.pallas_call(
        functools.partial(_wattn_kernel, tq=tq, g=g, n_ctx=n_ctx, span=span, t=t),
        grid=(KV_HEADS, t // tq),
        in_specs=[pl.BlockSpec((tq, gw), lambda h, i: (i, off_q + h)),
                  pl.BlockSpec((t, HEAD_DIM), lambda h, i: (0, off_k + h)),
                  pl.BlockSpec((t, HEAD_DIM), lambda h, i: (0, off_v + h)),
                  pl.BlockSpec((None, rows, 1), lambda h, i: (h, 0, 0))],
        out_specs=pl.BlockSpec((tq, gw), lambda h, i: (i, h)),
        out_shape=jax.ShapeDtypeStruct((t, mix), BF16),
        compiler_params=_params(2, vmem),
        name="wattn",
    )(p, p, p, sink_rows)


def _merge_kernel(u_ref, up_ref, un_ref, og_ref, ow_ref, g0_ref, g1_ref, g2_ref, pw_ref, ps_ref, wb_ref,
                  z_ref, ext_sc, pool_sc, *, tm, n_ctx, t, mix):
    i = pl.program_id(0)
    j = pl.program_id(1)
    gw = mix // len(POOL_WINDOWS)

    @pl.when(j == 0)
    def _():
        ext_sc[0:POOL_HALO, :] = up_ref[...].astype(F32)
        ext_sc[POOL_HALO:POOL_HALO + tm, :] = u_ref[...].astype(F32)
        ext_sc[POOL_HALO + tm:, :] = un_ref[...].astype(F32)
        r = i * tm + lax.broadcasted_iota(jnp.int32, (tm, 1), 0)
        r_ctx = r < n_ctx
        for gi, w in enumerate(POOL_WINDOWS):
            c0, c1 = gi * gw, (gi + 1) * gw
            tot = jnp.zeros((tm, gw), F32)
            cnt = jnp.zeros((tm, 1), F32)
            for off in range(-((w - 1) // 2), w // 2 + 1):
                rr = r + off
                ok = (rr >= 0) & (rr < t) & ((rr < n_ctx) == r_ctx)
                tot = tot + jnp.where(ok, ext_sc[POOL_HALO + off:POOL_HALO + off + tm, c0:c1], 0.0)
                cnt = cnt + ok.astype(F32)
            dlt = tot / cnt - ext_sc[POOL_HALO:POOL_HALO + tm, c0:c1]
            y = _dot(dlt.astype(BF16), pw_ref[gi].astype(BF16)) * ps_ref[:, c0:c1]
            pool_sc[:, c0:c1] = y.astype(BF16)

    z = g0_ref[...].astype(F32) * _dot(pool_sc[...], wb_ref[0].astype(BF16))
    z = z + g1_ref[...].astype(F32) * _dot(og_ref[...], wb_ref[1].astype(BF16))
    z = z + g2_ref[...].astype(F32) * _dot(ow_ref[...], wb_ref[2].astype(BF16))
    z_ref[...] = z.astype(BF16)


def _merge(p, og, ow, pool_w, pool_scale, w_branch, layer, n_ctx, d):
    t = p.shape[0]
    mix = d // 2
    tm = _pick(t, (768, 512, 256, 128))
    tn = _pick(d, (512, 256, 128))
    hb = tm // POOL_HALO
    n_hb = t // POOL_HALO
    gate0 = (p.shape[1] - 3 * d) // tn
    gs = pool_w.shape[-1]
    vmem = (2 * 3 * tm * mix * 2 + 2 * 3 * tm * tn * 2 + 2 * 3 * mix * tn * 4 + 3 * mix * tn * 2
            + (tm + 2 * POOL_HALO) * mix * 4 + tm * mix * 2 + 2 * tm * tn * 2 + 4 * tm * tn * 4
            + 2 * len(POOL_WINDOWS) * gs * gs * 4 + 6 * tm * gs * 4)
    return pl.pallas_call(
        functools.partial(_merge_kernel, tm=tm, n_ctx=n_ctx, t=t, mix=mix),
        grid=(t // tm, d // tn),
        in_specs=[pl.BlockSpec((tm, mix), lambda i, j: (i, 0)),
                  pl.BlockSpec((POOL_HALO, mix), lambda i, j: (jnp.maximum(i * hb - 1, 0), 0)),
                  pl.BlockSpec((POOL_HALO, mix), lambda i, j: (jnp.minimum((i + 1) * hb, n_hb - 1), 0)),
                  pl.BlockSpec((tm, mix), lambda i, j: (i, 0)),
                  pl.BlockSpec((tm, mix), lambda i, j: (i, 0)),
                  pl.BlockSpec((tm, tn), lambda i, j: (i, gate0 + j)),
                  pl.BlockSpec((tm, tn), lambda i, j: (i, gate0 + d // tn + j)),
                  pl.BlockSpec((tm, tn), lambda i, j: (i, gate0 + 2 * (d // tn) + j)),
                  pl.BlockSpec((None, len(POOL_WINDOWS), gs, gs), lambda i, j: (layer, 0, 0, 0)),
                  pl.BlockSpec((1, mix), lambda i, j: (0, 0)),
                  pl.BlockSpec((None, 3, mix, tn), lambda i, j: (layer, 0, 0, j))],
        out_specs=pl.BlockSpec((tm, tn), lambda i, j: (i, j)),
        out_shape=jax.ShapeDtypeStruct((t, d), BF16),
        scratch_shapes=[pltpu.VMEM((tm + 2 * POOL_HALO, mix), F32), pltpu.VMEM((tm, mix), BF16)],
        compiler_params=_params(2, vmem),
        name="merge",
    )(p, p, p, og, ow, p, p, p, pool_w, pool_scale, w_branch)


def _outproj_kernel(z_ref, w_ref, xs_ref, gate_ref, o_ref, *, tm, n_ctx):
    i = pl.program_id(0)
    gate = _row_select(i * tm, tm, n_ctx, gate_ref)
    o_ref[...] = xs_ref[...] + gate * _dot(z_ref[...], w_ref[...].astype(BF16))


def _outproj(z, w_out, layer, xs, mod_l, n_ctx):
    t, d = xs.shape
    tm = _largest_tile(t, 1536, BF16_ROWS)
    tn = _pick(d, (512, 256, 128))
    vmem = 2 * tm * d * 2 + 2 * d * tn * 4 + d * tn * 2 + 4 * tm * tn * 4 + 2 * tm * tn * 4
    return pl.pallas_call(
        functools.partial(_outproj_kernel, tm=tm, n_ctx=n_ctx),
        grid=(t // tm, d // tn),
        in_specs=[pl.BlockSpec((tm, d), lambda i, j: (i, 0)),
                  pl.BlockSpec((None, d, tn), lambda i, j: (layer, 0, j)),
                  pl.BlockSpec((tm, tn), lambda i, j: (i, j)),
                  pl.BlockSpec((8, tn), lambda i, j: (0, 2 * (d // tn) + j))],
        out_specs=pl.BlockSpec((tm, tn), lambda i, j: (i, j)),
        out_shape=jax.ShapeDtypeStruct((t, d), F32),
        compiler_params=_params(2, vmem),
        name="outproj",
    )(z, w_out, xs, mod_l)


def _router_kernel(xs_ref, shift_ref, scale_ref, g_ref, rwt_ref, rb_ref,
                   h_ref, ints_ref, ws_ref, cnt_ref, carry_sc, *, tm, n_ctx, n_exp):
    i = pl.program_id(0)

    @pl.when(i == 0)
    def _():
        carry_sc[...] = jnp.zeros(carry_sc.shape, F32)

    shift = _row_select(i * tm, tm, n_ctx, shift_ref)
    scale = _row_select(i * tm, tm, n_ctx, scale_ref)
    h = _norm_modulate(xs_ref[...], g_ref, shift, scale)
    h_ref[...] = h

    scores = _sigmoid(_dot_nt(rwt_ref[...].astype(BF16), h.astype(BF16)))
    sel = scores + rb_ref[...]
    per = n_exp // N_GROUPS
    sub = lax.broadcasted_iota(jnp.int32, (per, tm), 0)

    def top2(v):
        m1 = jnp.max(v, axis=0, keepdims=True)
        i1 = jnp.min(jnp.where(v == m1, sub, per), axis=0, keepdims=True)
        rest = jnp.where(sub == i1, -jnp.inf, v)
        m2 = jnp.max(rest, axis=0, keepdims=True)
        i2 = jnp.min(jnp.where(rest == m2, sub, per), axis=0, keepdims=True)
        return m1 + m2, i1, i2

    tops = [top2(sel[gi * per:(gi + 1) * per, :]) for gi in range(N_GROUPS)]
    best, l1, l2 = tops[0]
    grp = jnp.zeros((1, tm), jnp.int32)
    for gi in range(1, N_GROUPS):
        gs, a1, a2 = tops[gi]
        better = gs > best
        best = jnp.where(better, gs, best)
        grp = jnp.where(better, gi, grp)
        l1 = jnp.where(better, a1, l1)
        l2 = jnp.where(better, a2, l2)

    hot1 = [(grp == gi) & (sub == l1) for gi in range(N_GROUPS)]
    hot2 = [(grp == gi) & (sub == l2) for gi in range(N_GROUPS)]
    assign = jnp.concatenate([(a | b).astype(F32) for a, b in zip(hot1, hot2)], axis=0)

    before = (lax.broadcasted_iota(jnp.int32, (tm, tm), 0) < lax.broadcasted_iota(jnp.int32, (tm, tm), 1))
    pos = _dot(assign.astype(BF16), before.astype(F32).astype(BF16)) + carry_sc[...]

    def pick(hots, val):
        return sum(jnp.sum(jnp.where(hots[gi], val[gi * per:(gi + 1) * per, :], 0.0), axis=0, keepdims=True)
                   for gi in range(N_GROUPS))

    s1 = pick(hot1, scores)
    s2 = pick(hot2, scores)
    r1 = pick(hot1, pos)
    r2 = pick(hot2, pos)
    tot = s1 + s2
    ints_ref[0:1, :] = grp * per + l1
    ints_ref[1:2, :] = grp * per + l2
    ints_ref[2:3, :] = r1.astype(jnp.int32)
    ints_ref[3:4, :] = r2.astype(jnp.int32)
    ints_ref[4:8, :] = jnp.zeros((4, tm), jnp.int32)
    ws_ref[0:1, :] = s1 / tot
    ws_ref[1:2, :] = s2 / tot
    ws_ref[2:8, :] = jnp.zeros((6, tm), F32)
    carry_sc[...] = carry_sc[...] + jnp.sum(assign, axis=1, keepdims=True)
    cnt_ref[...] = jnp.broadcast_to(carry_sc[...], cnt_ref.shape)


def _router(xs, mod_l, g2, router_w, router_b, n_ctx):
    t, d = xs.shape
    n_exp = router_w.shape[1]
    tm = _pick(t, (768, 512, 256, 128))
    vmem = 2 * tm * d * 4 * 2 + 4 * tm * d * 4 + 3 * tm * tm * 4 + 2 * n_exp * d * 4
    return pl.pallas_call(
        functools.partial(_router_kernel, tm=tm, n_ctx=n_ctx, n_exp=n_exp),
        grid=(t // tm,),
        in_specs=[pl.BlockSpec((tm, d), lambda i: (i, 0)),
                  pl.BlockSpec((8, d), lambda i: (0, 3)),
                  pl.BlockSpec((8, d), lambda i: (0, 4)),
                  pl.BlockSpec((1, d), lambda i: (0, 0)),
                  pl.BlockSpec((n_exp, d), lambda i: (0, 0)),
                  pl.BlockSpec((n_exp, 1), lambda i: (0, 0))],
        out_specs=[pl.BlockSpec((tm, d), lambda i: (i, 0)),
                   pl.BlockSpec((8, tm), lambda i: (0, i)),
                   pl.BlockSpec((8, tm), lambda i: (0, i)),
                   pl.BlockSpec((n_exp, LANES), lambda i: (0, 0))],
        out_shape=[jax.ShapeDtypeStruct((t, d), F32),
                   jax.ShapeDtypeStruct((8, t), jnp.int32),
                   jax.ShapeDtypeStruct((8, t), F32),
                   jax.ShapeDtypeStruct((n_exp, LANES), F32)],
        scratch_shapes=[pltpu.VMEM((n_exp, 1), F32)],
        compiler_params=_params(1, vmem),
        name="router",
    )(xs, mod_l, mod_l, g2, router_w.T, router_b.reshape(n_exp, 1))


def _row_copy(src_ref, src_row, dst_ref, dst_row, sem):
    return pltpu.make_async_copy(src_ref.at[pl.ds(src_row, 1)], dst_ref.at[pl.ds(dst_row, 1)], sem)


def _dispatch_kernel(dst_ref, h_ref, xs_hbm, sem, *, tm):
    def issue(r, carry):
        for k in range(2):
            _row_copy(h_ref, r, xs_hbm, dst_ref[0, 0, k * tm + r], sem).start()
        return carry

    lax.fori_loop(0, tm, issue, 0, unroll=DMA_UNROLL)
    for k in range(2):
        pltpu.make_async_copy(h_ref, xs_hbm.at[pl.ds(0, tm)], sem).wait()


def _dest_blocks(ints, pad_start, tm):
    t = ints.shape[1]
    dest = pad_start[ints[0:2]] + ints[2:4]
    return dest.reshape(2, t // tm, tm).transpose(1, 0, 2).reshape(t // tm, 1, 2 * tm)


def _dispatch(h2, dest_blocks, n_rows, tm):
    t, d = h2.shape
    return pl.pallas_call(
        functools.partial(_dispatch_kernel, tm=tm),
        grid=(t // tm,),
        in_specs=[pl.BlockSpec((1, 1, 2 * tm), lambda i: (i, 0, 0), memory_space=pltpu.SMEM),
                  pl.BlockSpec((tm, d), lambda i: (i, 0))],
        out_specs=pl.BlockSpec(memory_space=pl.ANY),
        out_shape=jax.ShapeDtypeStruct((n_rows, d), F32),
        scratch_shapes=[pltpu.SemaphoreType.DMA(())],
        compiler_params=_params(1, 2 * tm * d * 4),
        name="dispatch",
    )(dest_blocks, h2)


def _expert_weights(b, plan_refs, w_hbm, stage, cast, sems, layer):
    blk_seg_ref, blk_first_ref, seg_e_ref, cnt_ref = plan_refs
    n_seg = cnt_ref[1]

    def copies(k, slot):
        e = seg_e_ref[k]
        return [pltpu.make_async_copy(w.at[layer, e], st.at[slot], sm.at[slot])
                for w, st, sm in zip(w_hbm, stage, sems)]

    @pl.when(b == 0)
    def _():
        for cp in copies(0, 0):
            cp.start()

        @pl.when(n_seg > 1)
        def _():
            for cp in copies(1, 1):
                cp.start()

    @pl.when(blk_first_ref[b] == 1)
    def _():
        k = blk_seg_ref[b]
        slot = k % 2
        for cp in copies(k, slot):
            cp.wait()
        for st, dst in zip(stage, cast):
            dst[...] = st[slot].astype(BF16)

        @pl.when(k + 2 < n_seg)
        def _():
            for cp in copies(k + 2, slot):
                cp.start()


def _ffn_up_kernel(blk_valid_ref, blk_seg_ref, blk_first_ref, seg_e_ref, cnt_ref, x_ref, wg_hbm, wu_hbm, h_ref,
                   wg_st, wu_st, wg_sc, wu_sc, sem_g, sem_u, *, layer):
    b = pl.program_id(0)

    @pl.when(b < cnt_ref[0])
    def _():
        _expert_weights(b, (blk_seg_ref, blk_first_ref, seg_e_ref, cnt_ref), (wg_hbm, wu_hbm), (wg_st, wu_st),
                        (wg_sc, wu_sc), (sem_g, sem_u), layer)
        rows = lax.broadcasted_iota(jnp.int32, (x_ref.shape[0], 1), 0)
        x = jnp.where(rows < blk_valid_ref[b], x_ref[...], 0.0).astype(BF16)
        gte = _dot(x, wg_sc[...])
        up = _dot(x, wu_sc[...])
        h_ref[...] = (gte * _sigmoid(gte) * up).astype(BF16)


def _ffn_down_kernel(blk_valid_ref, blk_seg_ref, blk_first_ref, seg_e_ref, cnt_ref, h_ref, wd_hbm, y_ref,
                     wd_st, wd_sc, sem_d, *, layer):
    b = pl.program_id(0)

    @pl.when(b < cnt_ref[0])
    def _():
        _expert_weights(b, (blk_seg_ref, blk_first_ref, seg_e_ref, cnt_ref), (wd_hbm,), (wd_st,), (wd_sc,),
                        (sem_d,), layer)
        y_ref[...] = _dot(h_ref[...], wd_sc[...])


def _ffn(xs_sorted, plan, w_gate, w_up, w_down, layer):
    n_rows, d = xs_sorted.shape
    ff = w_gate.shape[-1]
    blk = MOE_BLOCK
    n_blk = n_rows // blk
    prefetch = (plan["blk_valid"], plan["blk_seg"], plan["blk_first"], plan["seg_e"], plan["cnt"])
    row_map = lambda b, v, s, f, e, c: (jnp.minimum(b, c[0] - 1), 0)
    hbm = pl.BlockSpec(memory_space=pl.ANY)
    dma2 = pltpu.SemaphoreType.DMA((2,))
    up_spec = pltpu.PrefetchScalarGridSpec(
        num_scalar_prefetch=len(prefetch),
        grid=(n_blk,),
        in_specs=[pl.BlockSpec((blk, d), row_map), hbm, hbm],
        out_specs=pl.BlockSpec((blk, ff), row_map),
        scratch_shapes=[pltpu.VMEM((2, d, ff), F32), pltpu.VMEM((2, d, ff), F32),
                        pltpu.VMEM((d, ff), BF16), pltpu.VMEM((d, ff), BF16), dma2, dma2],
    )
    hidden = pl.pallas_call(
        functools.partial(_ffn_up_kernel, layer=layer),
        grid_spec=up_spec,
        out_shape=jax.ShapeDtypeStruct((n_rows, ff), BF16),
        compiler_params=_params(1, 2 * blk * d * 4 + 4 * d * ff * 4 + 2 * d * ff * 2 + 2 * blk * ff * 2
                                + 4 * blk * ff * 4 + blk * d * 2),
        name="ffn_up",
    )(*prefetch, xs_sorted, w_gate, w_up)
    down_spec = pltpu.PrefetchScalarGridSpec(
        num_scalar_prefetch=len(prefetch),
        grid=(n_blk,),
        in_specs=[pl.BlockSpec((blk, ff), row_map), hbm],
        out_specs=pl.BlockSpec((blk, d), row_map),
        scratch_shapes=[pltpu.VMEM((2, ff, d), F32), pltpu.VMEM((ff, d), BF16), dma2],
    )
    return pl.pallas_call(
        functools.partial(_ffn_down_kernel, layer=layer),
        grid_spec=down_spec,
        out_shape=jax.ShapeDtypeStruct((n_rows, d), F32),
        compiler_params=_params(1, 2 * blk * ff * 2 + 2 * d * ff * 4 + d * ff * 2 + 3 * blk * d * 4),
        name="ffn_down",
    )(*prefetch, hidden, w_down)


def _combine_kernel(src_ref, y_hbm, xs_ref, ws_ref, gate_ref, fg_ref, o_ref, ybuf, sem, *, tm, n_ctx, final):
    i = pl.program_id(0)

    def issue(r, carry):
        for k in range(2):
            _row_copy(y_hbm, src_ref[0, 0, k * tm + r], ybuf.at[k], r, sem).start()
        return carry

    lax.fori_loop(0, tm, issue, 0, unroll=DMA_UNROLL)
    for k in range(2):
        pltpu.make_async_copy(y_hbm.at[pl.ds(0, tm)], ybuf.at[k], sem).wait()

    wcol = jnp.transpose(ws_ref[...])
    y = wcol[:, 0:1] * ybuf[0] + wcol[:, 1:2] * ybuf[1]
    x = xs_ref[...] + _row_select(i * tm, tm, n_ctx, gate_ref) * y
    if final:
        ms = jnp.mean(x * x, axis=-1, keepdims=True)
        x = x * lax.rsqrt(ms + RMS_EPS) * fg_ref[...]
    o_ref[...] = x


def _combine(y_sorted, dest_blocks, xs, ws, mod_l, final_g, n_ctx, tm, final):
    t, d = xs.shape
    if final:
        skip = n_ctx // tm
        out_rows = t - n_ctx
        out_map = lambda i: (jnp.maximum(i - skip, 0), 0)
    else:
        out_rows = t
        out_map = lambda i: (i, 0)
    return pl.pallas_call(
        functools.partial(_combine_kernel, tm=tm, n_ctx=n_ctx, final=final),
        grid=(t // tm,),
        in_specs=[pl.BlockSpec((1, 1, 2 * tm), lambda i: (i, 0, 0), memory_space=pltpu.SMEM),
                  pl.BlockSpec(memory_space=pl.ANY),
                  pl.BlockSpec((tm, d), lambda i: (i, 0)),
                  pl.BlockSpec((8, tm), lambda i: (0, i)),
                  pl.BlockSpec((8, d), lambda i: (0, 5)),
                  pl.BlockSpec((1, d), lambda i: (0, 0))],
        out_specs=pl.BlockSpec((tm, d), out_map),
        out_shape=jax.ShapeDtypeStruct((out_rows, d), F32),
        scratch_shapes=[pltpu.VMEM((2, tm, d), F32), pltpu.SemaphoreType.DMA(())],
        compiler_params=_params(1, 2 * tm * d * 4 + 4 * tm * d * 4 + 4 * tm * d * 4),
        name="combine",
    )(dest_blocks, y_sorted, xs, ws, mod_l, final_g)


def _rope_tables(n_lat, n_ctx):
    rows = n_lat // GRID_W
    row_ids = jnp.repeat(jnp.arange(rows, dtype=F32), GRID_W)
    col_ids = jnp.tile(jnp.arange(GRID_W, dtype=F32), rows)
    n_freq = HEAD_DIM // 4
    inv = ROPE_THETA ** (-jnp.arange(n_freq, dtype=F32) / n_freq)
    ar = row_ids[:, None] * inv
    ac = col_ids[:, None] * inv
    ang = jnp.concatenate([ar, ar, ac, ac], axis=-1)
    sign = jnp.where((jnp.arange(HEAD_DIM) % (HEAD_DIM // 2)) < HEAD_DIM // 4, -1.0, 1.0).astype(F32)
    cos = jnp.concatenate([jnp.ones((n_ctx, HEAD_DIM), F32), jnp.cos(ang)], axis=0)
    sin = jnp.concatenate([jnp.zeros((n_ctx, HEAD_DIM), F32), jnp.sin(ang) * sign], axis=0)
    return cos, sin


def _moe_plan(counts, n_blk):
    blk = MOE_BLOCK
    n_exp = counts.shape[0]
    padded = (counts + blk - 1) // blk * blk
    pad_end = jnp.cumsum(padded)
    pad_start = pad_end - padded
    n_used = jnp.maximum(pad_end[-1] // blk, 1)
    b = jnp.minimum(jnp.arange(n_blk, dtype=jnp.int32), n_used - 1)
    blk_e = jnp.sum((pad_end[None, :] <= (b * blk)[:, None]).astype(jnp.int32), axis=1)
    blk_e = jnp.minimum(blk_e, n_exp - 1)
    blk_valid = jnp.clip(counts[blk_e] - (b * blk - pad_start[blk_e]), 0, blk).astype(jnp.int32)
    live = counts > 0
    seg_of_e = jnp.cumsum(live.astype(jnp.int32)) - 1
    n_seg = jnp.maximum(seg_of_e[-1] + 1, 1)
    ks = jnp.arange(n_exp, dtype=jnp.int32)
    seg_e = jnp.sum(jnp.where(live[None, :] & (seg_of_e[None, :] == ks[:, None]), ks[None, :], 0), axis=1)
    blk_first = jnp.concatenate([jnp.ones((1,), jnp.int32), (blk_e[1:] != blk_e[:-1]).astype(jnp.int32)])
    return {
        "pad_start": pad_start.astype(jnp.int32),
        "blk_valid": blk_valid,
        "blk_seg": seg_of_e[blk_e].astype(jnp.int32),
        "blk_first": blk_first,
        "seg_e": seg_e.astype(jnp.int32),
        "cnt": jnp.stack([n_used, n_seg]).astype(jnp.int32),
    }


def kernel(x, c, ctx, c_ctx, w_ada, b_ada, norm1_g, norm2_g, w_in, pool_w, pool_scale, q_norm_g, k_norm_g,
           sink, w_branch, w_out, router_w, router_b, w_gate, w_up, w_down, final_g):
    assert x.shape[0] == 1, "single-sequence kernel"
    n_lat, d = x.shape[1], x.shape[2]
    n_ctx = ctx.shape[1]
    t = n_ctx + n_lat
    mix = d // 2
    depth = w_in.shape[0]
    n_exp = router_w.shape[1]
    tok = _pick(n_ctx, (256, 128))
    assert (2 * t) % MOE_BLOCK == 0
    n_blk = 2 * t // MOE_BLOCK + n_exp
    cos, sin_signed = _rope_tables(n_lat, n_ctx)
    mod = _ada(c, c_ctx, w_ada, b_ada)
    xs = jnp.concatenate([ctx[0], x[0]], axis=0)
    for l in range(depth):
        last = l == depth - 1
        p = _inproj(xs, mod[l], norm1_g[l][None], w_in, l, cos, sin_signed,
                    q_norm_g[l][None], k_norm_g[l][None], n_ctx)
        og = _gattn(p, n_ctx, mix)
        ow = _wattn(p, sink[l], n_ctx, mix)
        z = _merge(p, og, ow, pool_w, pool_scale[l][None], w_branch, l, n_ctx, d)
        xs = _outproj(z, w_out, l, xs, mod[l], n_ctx)
        h2, ints, ws, cnt = _router(xs, mod[l], norm2_g[l][None], router_w, router_b, n_ctx)
        plan = _moe_plan(cnt[:, 0].astype(jnp.int32), n_blk)
        dest_blocks = _dest_blocks(ints, plan["pad_start"], tok)
        xs_sorted = _dispatch(h2, dest_blocks, n_blk * MOE_BLOCK, tok)
        y_sorted = _ffn(xs_sorted, plan, w_gate, w_up, w_down, l)
        xs = _combine(y_sorted, dest_blocks, xs, ws, mod[l], final_g[None], n_ctx, tok, last)
    return xs[None]
```

```python
import functools

import jax
import jax.numpy as jnp
from jax import lax
from jax.experimental import pallas as pl
from jax.experimental.pallas import tpu as pltpu

HEAD_DIM = 128
LANES = 128
BF16_ROWS = 16
KV_HEADS = 2
GRID_W = 64
WINDOW = 128
POOL_WINDOWS = (2, 4, 8, 16)
POOL_HALO = 16
N_GROUPS = 4
ROPE_THETA = 10000.0
RMS_EPS = 1e-6
NEG_INF = -1e30
ATTN_SCALE = HEAD_DIM ** -0.5
LOG2E = 1.4426950408889634
SOFTMAX_SAFE_LOG2 = 50.0
MOE_BLOCK = 256
DMA_UNROLL = 8
VMEM_CAP = 60 * 1024 * 1024

BF16 = jnp.bfloat16
F32 = jnp.float32


def _pick(n, prefs):
    for p in prefs:
        if n % p == 0:
            return p
    raise ValueError(f"no tile in {prefs} divides {n}")


def _largest_tile(n, cap, mult):
    return max(k for k in range(mult, min(n, cap) + 1, mult) if n % k == 0)


def _params(n_axes, vmem_bytes):
    limit = int(min(max(vmem_bytes * 5 // 4 + (4 << 20), 16 << 20), VMEM_CAP))
    return pltpu.CompilerParams(dimension_semantics=("arbitrary",) * n_axes, vmem_limit_bytes=limit)


def _sigmoid(x):
    return 1.0 / (1.0 + jnp.exp(-x))


def _sigmoid_tanh(x):
    return 0.5 * jnp.tanh(0.5 * x) + 0.5


def _dot(a, b):
    return jnp.dot(a, b, preferred_element_type=F32)


def _dot_nt(a, b):
    return lax.dot_general(a, b, (((1,), (1,)), ((), ())), preferred_element_type=F32)


def _row_select(row0, n_rows, n_ctx, mod_ref):
    rows = row0 + lax.broadcasted_iota(jnp.int32, (n_rows, 1), 0)
    return jnp.where(rows < n_ctx, mod_ref[1:2, :], mod_ref[0:1, :])


def _norm_modulate(x, g_ref, shift, scale):
    ms = jnp.mean(x * x, axis=-1, keepdims=True)
    y = x * lax.rsqrt(ms + RMS_EPS) * g_ref[...]
    return y * (1.0 + scale) + shift


def _ada_kernel(cb_ref, w_ref, b_ref, o_ref, *, tn):
    s = cb_ref[...]
    s = s * _sigmoid(s)
    outs = []
    for r in range(2):
        cols = [jnp.sum(w_ref[:, c * LANES:(c + 1) * LANES] * s[r], axis=0, keepdims=True)
                for c in range(tn // LANES)]
        outs.append(jnp.concatenate(cols, axis=1) + b_ref[...])
    o_ref[...] = jnp.concatenate(outs + [jnp.zeros((6, tn), F32)], axis=0)


def _ada(c, c_ctx, w_ada, b_ada):
    depth, d, w6 = w_ada.shape
    tn = _pick(w6, (1024, 512, 256, 128))
    cb = jnp.broadcast_to(jnp.stack([c[0], c_ctx])[:, :, None], (2, d, LANES))
    vmem = 2 * d * tn * 4 + 2 * d * LANES * 4 * 2 + d * LANES * 4 * 4
    return pl.pallas_call(
        functools.partial(_ada_kernel, tn=tn),
        grid=(depth, w6 // tn),
        in_specs=[pl.BlockSpec((2, d, LANES), lambda l, j: (0, 0, 0)),
                  pl.BlockSpec((None, d, tn), lambda l, j: (l, 0, j)),
                  pl.BlockSpec((None, 1, tn), lambda l, j: (l, 0, j))],
        out_specs=pl.BlockSpec((None, 8, tn), lambda l, j: (l, 0, j)),
        out_shape=jax.ShapeDtypeStruct((depth, 8, w6), F32),
        compiler_params=_params(2, vmem),
        name="ada",
    )(cb, w_ada, b_ada.reshape(depth, 1, w6))


def _rope(x, cos, sin_signed, first_half):
    rot = jnp.where(first_half, pltpu.roll(x, 3 * HEAD_DIM // 4, axis=1), pltpu.roll(x, HEAD_DIM // 4, axis=1))
    return x * cos + rot * sin_signed


def _head_rms(x, g_ref):
    ms = jnp.mean(x * x, axis=-1, keepdims=True)
    return x * lax.rsqrt(ms + RMS_EPS) * g_ref[...]


def _inproj_kernel(x_ref, shift_ref, scale_ref, g_ref, w_ref, cos_ref, sin_ref, qg_ref, kg_ref,
                   o_ref, h_sc, *, tm, tn, n_ctx, signatures):
    i = pl.program_id(0)
    j = pl.program_id(1)

    def normed():
        x = x_ref[...]
        return x * lax.rsqrt(jnp.mean(x * x, axis=-1, keepdims=True) + RMS_EPS)

    has_ctx_rows = i * tm < n_ctx

    @pl.when((j == 0) & has_ctx_rows)
    def _():
        gain = g_ref[...] * (1.0 + scale_ref[0:2, :])
        rows = i * tm + lax.broadcasted_iota(jnp.int32, (tm, 1), 0)
        is_ctx = rows < n_ctx
        h = normed() * jnp.where(is_ctx, gain[1:2], gain[0:1]) + jnp.where(is_ctx, shift_ref[1:2, :], shift_ref[0:1, :])
        h_sc[...] = h.astype(BF16)

    @pl.when((j == 0) & jnp.logical_not(has_ctx_rows))
    def _():
        h = normed() * (g_ref[...] * (1.0 + scale_ref[0:1, :])) + shift_ref[0:1, :]
        h_sc[...] = h.astype(BF16)

    acc = _dot(h_sc[...], w_ref[...].astype(BF16))

    lane = lax.broadcasted_iota(jnp.int32, (1, HEAD_DIM), 1)
    first_half = (lane % (HEAD_DIM // 2)) < (HEAD_DIM // 4)

    for sig, js in signatures:
        cond = functools.reduce(jnp.logical_or, [j == jj for jj in js])

        @pl.when(cond)
        def _(sig=sig):
            if all(t == "plain" for t in sig):
                o_ref[...] = acc.astype(BF16)
                return
            if all(t == "sig" for t in sig):
                o_ref[...] = _sigmoid_tanh(acc).astype(BF16)
                return
            for bi, typ in enumerate(sig):
                a = acc[:, bi * LANES:(bi + 1) * LANES]
                if typ == "sig":
                    a = _sigmoid_tanh(a)
                elif typ != "plain":
                    if typ in ("qnr", "knr"):
                        a = _head_rms(a, qg_ref if typ == "qnr" else kg_ref)
                    a = _rope(a, cos_ref[...], sin_ref[...], first_half)
                    if typ in ("qnr", "qr"):
                        a = a * (ATTN_SCALE * LOG2E)
                o_ref[:, bi * LANES:(bi + 1) * LANES] = a.astype(BF16)


def _col_types(mix, d):
    hq = mix // HEAD_DIM
    types = (["plain"] * hq + ["qnr"] * hq + ["knr"] * KV_HEADS + ["plain"] * KV_HEADS
             + ["qr"] * hq + ["kr"] * KV_HEADS + ["plain"] * KV_HEADS + ["sig"] * (3 * d // LANES))
    return types


def _inproj(xs, mod_l, g1, w_in, layer, cos, sin_signed, qg, kg, n_ctx):
    t, d = xs.shape
    in_w = w_in.shape[-1]
    mix = d // 2
    tm = _pick(t, (768, 512, 256, 128))
    tn = _pick(in_w, (1024, 512, 256))
    types = _col_types(mix, d)
    assert len(types) * LANES == in_w
    per = tn // LANES
    sigs = {}
    for jj in range(in_w // tn):
        sigs.setdefault(tuple(types[jj * per:(jj + 1) * per]), []).append(jj)
    signatures = tuple((s, tuple(js)) for s, js in sigs.items())
    vmem = (2 * tm * d * 4 + 2 * d * tn * 4 + tm * d * 2 + 2 * tm * tn * 2 + tm * tn * 4 * 2 + d * tn * 2
            + 4 * tm * LANES * 4)
    return pl.pallas_call(
        functools.partial(_inproj_kernel, tm=tm, tn=tn, n_ctx=n_ctx, signatures=signatures),
        grid=(t // tm, in_w // tn),
        in_specs=[pl.BlockSpec((tm, d), lambda i, j: (i, 0)),
                  pl.BlockSpec((8, d), lambda i, j: (0, 0)),
                  pl.BlockSpec((8, d), lambda i, j: (0, 1)),
                  pl.BlockSpec((1, d), lambda i, j: (0, 0)),
                  pl.BlockSpec((None, d, tn), lambda i, j: (layer, 0, j)),
                  pl.BlockSpec((tm, HEAD_DIM), lambda i, j: (i, 0)),
                  pl.BlockSpec((tm, HEAD_DIM), lambda i, j: (i, 0)),
                  pl.BlockSpec((1, HEAD_DIM), lambda i, j: (0, 0)),
                  pl.BlockSpec((1, HEAD_DIM), lambda i, j: (0, 0))],
        out_specs=pl.BlockSpec((tm, tn), lambda i, j: (i, j)),
        out_shape=jax.ShapeDtypeStruct((t, in_w), BF16),
        scratch_shapes=[pltpu.VMEM((tm, d), BF16)],
        compiler_params=_params(2, vmem),
        name="inproj",
    )(xs, mod_l, mod_l, g1, w_in, cos, sin_signed, qg, kg)


def _lane_repeat(x, n):
    return jnp.concatenate([x] * n, axis=1)


def _stack_heads(q_ref, g):
    return jnp.concatenate([q_ref[:, h * HEAD_DIM:(h + 1) * HEAD_DIM] for h in range(g)], axis=0)


def _unstack_store(o_ref, out, g, tq):
    for h in range(g):
        o_ref[:, h * HEAD_DIM:(h + 1) * HEAD_DIM] = out[h * tq:(h + 1) * tq, :].astype(o_ref.dtype)


def _gattn_kernel(q_ref, k_ref, v_ref, o_ref, vx_sc, kmax_sc, m_sc, acc_sc, *, tq, tk, g, n_ctx, n_lat):
    qi = pl.program_id(1)

    @pl.when(qi == 0)
    def _():
        vx_sc[:, 0:HEAD_DIM] = v_ref[...]
        vx_sc[:, HEAD_DIM:] = jnp.ones((vx_sc.shape[0], HEAD_DIM), BF16)
        kf = k_ref[...].astype(F32)
        k2 = jnp.max(jnp.sum(kf * kf, axis=-1, keepdims=True), axis=0, keepdims=True)
        kmax_sc[...] = jnp.broadcast_to(jnp.sqrt(k2), kmax_sc.shape)

    q = _stack_heads(q_ref, g)
    rows = q.shape[0]
    qf = q.astype(F32)
    bound = jnp.sqrt(jnp.sum(qf * qf, axis=-1, keepdims=True)) * kmax_sc[0:1, :]
    bounded = jnp.max(bound) <= SOFTMAX_SAFE_LOG2
    ctx_only = qi * tq < n_ctx
    ctx_chunk = [(0, n_ctx)]
    all_chunks = ctx_chunk + [(n_ctx + c * tk, tk) for c in range(n_lat // tk)]

    def finish():
        acc = acc_sc[...]
        _unstack_store(o_ref, acc[:, :HEAD_DIM] / acc[:, HEAD_DIM:], g, tq)

    def attend_bounded(chunks):
        for n, (lo, size) in enumerate(chunks):
            s = _dot_nt(q, k_ref[lo:lo + size, :])
            p = jnp.exp2(s - _lane_repeat(bound, size // LANES))
            pv = _dot(p.astype(BF16), vx_sc[lo:lo + size, :])
            acc_sc[...] = pv if n == 0 else acc_sc[...] + pv
        finish()

    def attend_online(chunks):
        for n, (lo, size) in enumerate(chunks):
            s = _dot_nt(q, k_ref[lo:lo + size, :])
            mx = jnp.max(s, axis=-1, keepdims=True)
            if n == 0:
                m_new = jnp.broadcast_to(mx, (rows, LANES))
            else:
                m_prev = m_sc[...]
                m_new = jnp.maximum(m_prev, mx)
            p = jnp.exp2(s - _lane_repeat(m_new, size // LANES))
            pv = _dot(p.astype(BF16), vx_sc[lo:lo + size, :])
            if n == 0:
                acc_sc[...] = pv
            else:
                acc_sc[...] = _lane_repeat(jnp.exp2(m_prev - m_new), 2) * acc_sc[...] + pv
            m_sc[...] = m_new
        finish()

    for use_bound, attend in ((True, attend_bounded), (False, attend_online)):
        path = bounded if use_bound else jnp.logical_not(bounded)
        pl.when(path & ctx_only)(functools.partial(attend, ctx_chunk))
        pl.when(path & jnp.logical_not(ctx_only))(functools.partial(attend, all_chunks))


def _gattn(p, n_ctx, mix):
    t = p.shape[0]
    hq = mix // HEAD_DIM
    g = hq // KV_HEADS
    n_lat = t - n_ctx
    tq = _pick(n_ctx, (256, 128))
    assert t % tq == 0
    tk = _pick(n_lat, (2048, 1024, 512, 256, 128))
    gw = g * HEAD_DIM
    off_q = mix // gw
    off_k = (mix + hq * HEAD_DIM) // HEAD_DIM
    off_v = off_k + KV_HEADS
    rows = g * tq
    vmem = (2 * tq * gw * 2 * 2 + 2 * 2 * t * HEAD_DIM * 2 + t * 2 * HEAD_DIM * 2 + rows * LANES * 4 * 4
            + rows * max(tk, n_ctx) * 4 * 3)
    return pl.pallas_call(
        functools.partial(_gattn_kernel, tq=tq, tk=tk, g=g, n_ctx=n_ctx, n_lat=n_lat),
        grid=(KV_HEADS, t // tq),
        in_specs=[pl.BlockSpec((tq, gw), lambda h, i: (i, off_q + h)),
                  pl.BlockSpec((t, HEAD_DIM), lambda h, i: (0, off_k + h)),
                  pl.BlockSpec((t, HEAD_DIM), lambda h, i: (0, off_v + h))],
        out_specs=pl.BlockSpec((tq, gw), lambda h, i: (i, h)),
        out_shape=jax.ShapeDtypeStruct((t, mix), BF16),
        scratch_shapes=[pltpu.VMEM((t, 2 * HEAD_DIM), BF16), pltpu.VMEM((8, LANES), F32),
                        pltpu.VMEM((rows, LANES), F32), pltpu.VMEM((rows, 2 * HEAD_DIM), F32)],
        compiler_params=_params(2, vmem),
        name="gattn",
    )(p, p, p)


def _wattn_kernel(*refs, tq, g, n_ctx, span, t):
    q_refs, k_refs, v_refs = refs[0:KV_HEADS], refs[KV_HEADS:2 * KV_HEADS], refs[2 * KV_HEADS:3 * KV_HEADS]
    sink_ref, o_ref, vx_sc = refs[3 * KV_HEADS:]
    qi = pl.program_id(0)

    @pl.when(qi == 0)
    def _():
        for h in range(KV_HEADS):
            vx_sc[h, :, 0:HEAD_DIM] = v_refs[h][...]
            vx_sc[h, :, HEAD_DIM:] = jnp.ones((t, HEAD_DIM), BF16)

    for h in range(KV_HEADS):
        _wattn_head(qi, q_refs[h], k_refs[h], vx_sc.at[h], sink_ref.at[h], o_ref.at[:, h * g * HEAD_DIM:(h + 1) * g * HEAD_DIM],
                    tq=tq, g=g, n_ctx=n_ctx, span=span, t=t)


def _wattn_head(qi, q_ref, k_ref, vx_sc, sink_ref, o_ref, *, tq, g, n_ctx, span, t):
    q = _stack_heads(q_ref, g)
    rows = q.shape[0]
    start = pl.multiple_of(jnp.clip(qi * tq - WINDOW, 0, t - span), LANES)

    far = t + 4 * WINDOW
    qrow = qi * tq + lax.broadcasted_iota(jnp.int32, (tq, 1), 0)
    qrow = jnp.concatenate([jnp.where(qrow >= n_ctx, qrow, -far)] * g, axis=0)
    krow = start + lax.broadcasted_iota(jnp.int32, (1, span), 1)
    krow = jnp.where(krow >= n_ctx, krow, far)
    keep = jnp.abs(krow - qrow) <= WINDOW

    s_c = _dot_nt(q, k_ref[0:n_ctx, :])
    s_w = jnp.where(keep, _dot_nt(q, k_ref[pl.ds(start, span), :]), NEG_INF)
    sink = sink_ref[...] * LOG2E
    mx = jnp.maximum(jnp.maximum(jnp.max(s_c, axis=-1, keepdims=True), jnp.max(s_w, axis=-1, keepdims=True)), sink)
    m = jnp.broadcast_to(mx, (rows, LANES))
    p_c = jnp.exp2(s_c - _lane_repeat(m, n_ctx // LANES))
    p_w = jnp.exp2(s_w - _lane_repeat(m, span // LANES))
    acc = _dot(p_c.astype(BF16), vx_sc[0:n_ctx, :]) + _dot(p_w.astype(BF16), vx_sc[pl.ds(start, span), :])
    denom = acc[:, HEAD_DIM:] + jnp.exp2(sink - m)
    _unstack_store(o_ref, acc[:, :HEAD_DIM] / denom, g, tq)


def _wattn(p, sink_l, n_ctx, mix):
    t = p.shape[0]
    hq = mix // HEAD_DIM
    g = hq // KV_HEADS
    tq = _pick(n_ctx, (256, 128))
    span = tq + 2 * WINDOW
    assert t % tq == 0 and t >= span
    gw = g * HEAD_DIM
    base = mix + hq * HEAD_DIM + 2 * KV_HEADS * HEAD_DIM
    off_q = base // gw
    off_k = (base + hq * HEAD_DIM) // HEAD_DIM
    off_v = off_k + KV_HEADS
    rows = g * tq
    sink_rows = jnp.repeat(sink_l.reshape(KV_HEADS, g), tq, axis=1).reshape(KV_HEADS, rows, 1)
    vmem = KV_HEADS * (2 * tq * gw * 2 * 2 + 2 * 2 * t * HEAD_DIM * 2 + 2 * rows * LANES * 4
                       + rows * (span + n_ctx) * 4 * 3 + t * 2 * HEAD_DIM * 2)
    heads = range(KV_HEADS)
    return pl.pallas_call(
        functools.partial(_wattn_kernel, tq=tq, g=g, n_ctx=n_ctx, span=span, t=t),
        grid=(t // tq,),
        in_specs=([pl.BlockSpec((tq, gw), lambda i, h=h: (i, off_q + h)) for h in heads]
                  + [pl.BlockSpec((t, HEAD_DIM), lambda i, h=h: (0, off_k + h)) for h in heads]
                  + [pl.BlockSpec((t, HEAD_DIM), lambda i, h=h: (0, off_v + h)) for h in heads]
                  + [pl.BlockSpec((KV_HEADS, rows, 1), lambda i: (0, 0, 0))]),
        out_specs=pl.BlockSpec((tq, mix), lambda i: (i, 0)),
        out_shape=jax.ShapeDtypeStruct((t, mix), BF16),
        scratch_shapes=[pltpu.VMEM((KV_HEADS, t, 2 * HEAD_DIM), BF16)],
        compiler_params=_params(1, vmem),
        name="wattn",
    )(*([p] * (3 * KV_HEADS)), sink_rows)


def _merge_kernel(u_ref, up_ref, un_ref, og_ref, ow_ref, g0_ref, g1_ref, g2_ref, pw_ref, ps_ref, wb_ref,
                  z_ref, ext_sc, pool_sc, *, tm, n_ctx, t, mix):
    i = pl.program_id(0)
    j = pl.program_id(1)
    gw = mix // len(POOL_WINDOWS)

    def pool(clipped):
        ext_sc[0:POOL_HALO, :] = up_ref[...].astype(F32)
        ext_sc[POOL_HALO:POOL_HALO + tm, :] = u_ref[...].astype(F32)
        ext_sc[POOL_HALO + tm:, :] = un_ref[...].astype(F32)
        r = i * tm + lax.broadcasted_iota(jnp.int32, (tm, 1), 0)
        r_ctx = r < n_ctx
        for gi, w in enumerate(POOL_WINDOWS):
            c0, c1 = gi * gw, (gi + 1) * gw
            tot = jnp.zeros((tm, gw), F32)
            cnt = jnp.zeros((tm, 1), F32)
            for off in range(-((w - 1) // 2), w // 2 + 1):
                part = ext_sc[POOL_HALO + off:POOL_HALO + off + tm, c0:c1]
                if clipped:
                    rr = r + off
                    ok = (rr >= 0) & (rr < t) & ((rr < n_ctx) == r_ctx)
                    part = jnp.where(ok, part, 0.0)
                    cnt = cnt + ok.astype(F32)
                tot = tot + part
            mean = tot / cnt if clipped else tot * (1.0 / w)
            dlt = mean - ext_sc[POOL_HALO:POOL_HALO + tm, c0:c1]
            y = _dot(dlt.astype(BF16), pw_ref[gi].astype(BF16)) * ps_ref[:, c0:c1]
            pool_sc[:, c0:c1] = y.astype(BF16)

    reach = max(POOL_WINDOWS) // 2
    interior = (i * tm - reach >= n_ctx) & ((i + 1) * tm + reach <= t)
    pl.when((j == 0) & interior)(functools.partial(pool, False))
    pl.when((j == 0) & jnp.logical_not(interior))(functools.partial(pool, True))

    z = g0_ref[...].astype(F32) * _dot(pool_sc[...], wb_ref[0].astype(BF16))
    z = z + g1_ref[...].astype(F32) * _dot(og_ref[...], wb_ref[1].astype(BF16))
    z = z + g2_ref[...].astype(F32) * _dot(ow_ref[...], wb_ref[2].astype(BF16))
    z_ref[...] = z.astype(BF16)


def _merge(p, og, ow, pool_w, pool_scale, w_branch, layer, n_ctx, d):
    t = p.shape[0]
    mix = d // 2
    tm = _pick(t, (768, 512, 256, 128))
    tn = _pick(d, (512, 256, 128))
    hb = tm // POOL_HALO
    n_hb = t // POOL_HALO
    gate0 = (p.shape[1] - 3 * d) // tn
    gs = pool_w.shape[-1]
    vmem = (2 * 3 * tm * mix * 2 + 2 * 3 * tm * tn * 2 + 2 * 3 * mix * tn * 4 + 3 * mix * tn * 2
            + (tm + 2 * POOL_HALO) * mix * 4 + tm * mix * 2 + 2 * tm * tn * 2 + 4 * tm * tn * 4
            + 2 * len(POOL_WINDOWS) * gs * gs * 4 + 6 * tm * gs * 4)
    return pl.pallas_call(
        functools.partial(_merge_kernel, tm=tm, n_ctx=n_ctx, t=t, mix=mix),
        grid=(t // tm, d // tn),
        in_specs=[pl.BlockSpec((tm, mix), lambda i, j: (i, 0)),
                  pl.BlockSpec((POOL_HALO, mix), lambda i, j: (jnp.maximum(i * hb - 1, 0), 0)),
                  pl.BlockSpec((POOL_HALO, mix), lambda i, j: (jnp.minimum((i + 1) * hb, n_hb - 1), 0)),
                  pl.BlockSpec((tm, mix), lambda i, j: (i, 0)),
                  pl.BlockSpec((tm, mix), lambda i, j: (i, 0)),
                  pl.BlockSpec((tm, tn), lambda i, j: (i, gate0 + j)),
                  pl.BlockSpec((tm, tn), lambda i, j: (i, gate0 + d // tn + j)),
                  pl.BlockSpec((tm, tn), lambda i, j: (i, gate0 + 2 * (d // tn) + j)),
                  pl.BlockSpec((None, len(POOL_WINDOWS), gs, gs), lambda i, j: (layer, 0, 0, 0)),
                  pl.BlockSpec((1, mix), lambda i, j: (0, 0)),
                  pl.BlockSpec((None, 3, mix, tn), lambda i, j: (layer, 0, 0, j))],
        out_specs=pl.BlockSpec((tm, tn), lambda i, j: (i, j)),
        out_shape=jax.ShapeDtypeStruct((t, d), BF16),
        scratch_shapes=[pltpu.VMEM((tm + 2 * POOL_HALO, mix), F32), pltpu.VMEM((tm, mix), BF16)],
        compiler_params=_params(2, vmem),
        name="merge",
    )(p, p, p, og, ow, p, p, p, pool_w, pool_scale, w_branch)


def _outproj_kernel(z_ref, w_ref, xs_ref, gate_ref, o_ref, *, tm, n_ctx):
    i = pl.program_id(0)
    gate = _row_select(i * tm, tm, n_ctx, gate_ref)
    o_ref[...] = xs_ref[...] + gate * _dot(z_ref[...], w_ref[...].astype(BF16))


def _outproj(z, w_out, layer, xs, mod_l, n_ctx):
    t, d = xs.shape
    tm = _largest_tile(t, 1536, BF16_ROWS)
    tn = _pick(d, (512, 256, 128))
    vmem = 2 * tm * d * 2 + 2 * d * tn * 4 + d * tn * 2 + 4 * tm * tn * 4 + 2 * tm * tn * 4
    return pl.pallas_call(
        functools.partial(_outproj_kernel, tm=tm, n_ctx=n_ctx),
        grid=(t // tm, d // tn),
        in_specs=[pl.BlockSpec((tm, d), lambda i, j: (i, 0)),
                  pl.BlockSpec((None, d, tn), lambda i, j: (layer, 0, j)),
                  pl.BlockSpec((tm, tn), lambda i, j: (i, j)),
                  pl.BlockSpec((8, tn), lambda i, j: (0, 2 * (d // tn) + j))],
        out_specs=pl.BlockSpec((tm, tn), lambda i, j: (i, j)),
        out_shape=jax.ShapeDtypeStruct((t, d), F32),
        compiler_params=_params(2, vmem),
        name="outproj",
    )(z, w_out, xs, mod_l)


def _router_kernel(xs_ref, shift_ref, scale_ref, g_ref, rwt_ref, rb_ref,
                   h_ref, ints_ref, ws_ref, cnt_ref, carry_sc, *, tm, n_ctx, n_exp):
    i = pl.program_id(0)

    @pl.when(i == 0)
    def _():
        carry_sc[...] = jnp.zeros(carry_sc.shape, F32)

    shift = _row_select(i * tm, tm, n_ctx, shift_ref)
    scale = _row_select(i * tm, tm, n_ctx, scale_ref)
    h = _norm_modulate(xs_ref[...], g_ref, shift, scale)
    h_ref[...] = h

    scores = _sigmoid(_dot_nt(rwt_ref[...].astype(BF16), h.astype(BF16)))
    sel = scores + rb_ref[...]
    per = n_exp // N_GROUPS
    sub = lax.broadcasted_iota(jnp.int32, (per, tm), 0)

    def top2(v):
        m1 = jnp.max(v, axis=0, keepdims=True)
        i1 = jnp.min(jnp.where(v == m1, sub, per), axis=0, keepdims=True)
        rest = jnp.where(sub == i1, -jnp.inf, v)
        m2 = jnp.max(rest, axis=0, keepdims=True)
        i2 = jnp.min(jnp.where(rest == m2, sub, per), axis=0, keepdims=True)
        return m1 + m2, i1, i2

    tops = [top2(sel[gi * per:(gi + 1) * per, :]) for gi in range(N_GROUPS)]
    best, l1, l2 = tops[0]
    grp = jnp.zeros((1, tm), jnp.int32)
    for gi in range(1, N_GROUPS):
        gs, a1, a2 = tops[gi]
        better = gs > best
        best = jnp.where(better, gs, best)
        grp = jnp.where(better, gi, grp)
        l1 = jnp.where(better, a1, l1)
        l2 = jnp.where(better, a2, l2)

    hot1 = [(grp == gi) & (sub == l1) for gi in range(N_GROUPS)]
    hot2 = [(grp == gi) & (sub == l2) for gi in range(N_GROUPS)]
    assign = jnp.concatenate([(a | b).astype(F32) for a, b in zip(hot1, hot2)], axis=0)

    before = (lax.broadcasted_iota(jnp.int32, (tm, tm), 0) < lax.broadcasted_iota(jnp.int32, (tm, tm), 1))
    pos = _dot(assign.astype(BF16), before.astype(F32).astype(BF16)) + carry_sc[...]

    def pick(hots, val):
        return sum(jnp.sum(jnp.where(hots[gi], val[gi * per:(gi + 1) * per, :], 0.0), axis=0, keepdims=True)
                   for gi in range(N_GROUPS))

    s1 = pick(hot1, scores)
    s2 = pick(hot2, scores)
    r1 = pick(hot1, pos)
    r2 = pick(hot2, pos)
    tot = s1 + s2
    ints_ref[0:1, :] = grp * per + l1
    ints_ref[1:2, :] = grp * per + l2
    ints_ref[2:3, :] = r1.astype(jnp.int32)
    ints_ref[3:4, :] = r2.astype(jnp.int32)
    ints_ref[4:8, :] = jnp.zeros((4, tm), jnp.int32)
    ws_ref[0:1, :] = s1 / tot
    ws_ref[1:2, :] = s2 / tot
    ws_ref[2:8, :] = jnp.zeros((6, tm), F32)
    carry_sc[...] = carry_sc[...] + jnp.sum(assign, axis=1, keepdims=True)
    cnt_ref[...] = jnp.broadcast_to(carry_sc[...], cnt_ref.shape)


def _router(xs, mod_l, g2, router_w, router_b, n_ctx):
    t, d = xs.shape
    n_exp = router_w.shape[1]
    tm = _pick(t, (768, 512, 256, 128))
    vmem = 2 * tm * d * 4 * 2 + 4 * tm * d * 4 + 3 * tm * tm * 4 + 2 * n_exp * d * 4
    return pl.pallas_call(
        functools.partial(_router_kernel, tm=tm, n_ctx=n_ctx, n_exp=n_exp),
        grid=(t // tm,),
        in_specs=[pl.BlockSpec((tm, d), lambda i: (i, 0)),
                  pl.BlockSpec((8, d), lambda i: (0, 3)),
                  pl.BlockSpec((8, d), lambda i: (0, 4)),
                  pl.BlockSpec((1, d), lambda i: (0, 0)),
                  pl.BlockSpec((n_exp, d), lambda i: (0, 0)),
                  pl.BlockSpec((n_exp, 1), lambda i: (0, 0))],
        out_specs=[pl.BlockSpec((tm, d), lambda i: (i, 0)),
                   pl.BlockSpec((8, tm), lambda i: (0, i)),
                   pl.BlockSpec((8, tm), lambda i: (0, i)),
                   pl.BlockSpec((n_exp, LANES), lambda i: (0, 0))],
        out_shape=[jax.ShapeDtypeStruct((t, d), F32),
                   jax.ShapeDtypeStruct((8, t), jnp.int32),
                   jax.ShapeDtypeStruct((8, t), F32),
                   jax.ShapeDtypeStruct((n_exp, LANES), F32)],
        scratch_shapes=[pltpu.VMEM((n_exp, 1), F32)],
        compiler_params=_params(1, vmem),
        name="router",
    )(xs, mod_l, mod_l, g2, router_w.T, router_b.reshape(n_exp, 1))


def _row_copy(src_ref, src_row, dst_ref, dst_row, sem):
    return pltpu.make_async_copy(src_ref.at[pl.ds(src_row, 1)], dst_ref.at[pl.ds(dst_row, 1)], sem)


def _dispatch_kernel(dst_ref, h_ref, xs_hbm, sem, *, tm):
    def issue(r, carry):
        for k in range(2):
            _row_copy(h_ref, r, xs_hbm, dst_ref[0, 0, k * tm + r], sem).start()
        return carry

    lax.fori_loop(0, tm, issue, 0, unroll=DMA_UNROLL)
    for k in range(2):
        pltpu.make_async_copy(h_ref, xs_hbm.at[pl.ds(0, tm)], sem).wait()


def _dest_blocks(ints, pad_start, tm):
    t = ints.shape[1]
    experts = jnp.arange(pad_start.shape[0], dtype=jnp.int32)
    start = jnp.sum(jnp.where(ints[0:2, :, None] == experts, pad_start, 0), axis=-1)
    dest = start + ints[2:4]
    return dest.reshape(2, t // tm, tm).transpose(1, 0, 2).reshape(t // tm, 1, 2 * tm)


def _dispatch(h2, dest_blocks, n_rows, tm):
    t, d = h2.shape
    return pl.pallas_call(
        functools.partial(_dispatch_kernel, tm=tm),
        grid=(t // tm,),
        in_specs=[pl.BlockSpec((1, 1, 2 * tm), lambda i: (i, 0, 0), memory_space=pltpu.SMEM),
                  pl.BlockSpec((tm, d), lambda i: (i, 0))],
        out_specs=pl.BlockSpec(memory_space=pl.ANY),
        out_shape=jax.ShapeDtypeStruct((n_rows, d), F32),
        scratch_shapes=[pltpu.SemaphoreType.DMA(())],
        compiler_params=_params(1, 2 * tm * d * 4),
        name="dispatch",
    )(dest_blocks, h2)


def _expert_weights(b, plan_refs, w_hbm, stage, cast, sems, layer):
    blk_seg_ref, blk_first_ref, seg_e_ref, cnt_ref = plan_refs
    n_seg = cnt_ref[1]

    def copies(k, slot):
        e = seg_e_ref[k]
        return [pltpu.make_async_copy(w.at[layer, e], st.at[slot], sm.at[slot])
                for w, st, sm in zip(w_hbm, stage, sems)]

    @pl.when(b == 0)
    def _():
        for cp in copies(0, 0):
            cp.start()

        @pl.when(n_seg > 1)
        def _():
            for cp in copies(1, 1):
                cp.start()

    @pl.when(blk_first_ref[b] == 1)
    def _():
        k = blk_seg_ref[b]
        slot = k % 2
        for cp in copies(k, slot):
            cp.wait()
        for st, dst in zip(stage, cast):
            dst[...] = st[slot].astype(BF16)

        @pl.when(k + 2 < n_seg)
        def _():
            for cp in copies(k + 2, slot):
                cp.start()


def _ffn_up_kernel(blk_valid_ref, blk_seg_ref, blk_first_ref, seg_e_ref, cnt_ref, x_ref, wg_hbm, wu_hbm, h_ref,
                   wg_st, wu_st, wg_sc, wu_sc, sem_g, sem_u, *, layer):
    b = pl.program_id(0)

    @pl.when(b < cnt_ref[0])
    def _():
        _expert_weights(b, (blk_seg_ref, blk_first_ref, seg_e_ref, cnt_ref), (wg_hbm, wu_hbm), (wg_st, wu_st),
                        (wg_sc, wu_sc), (sem_g, sem_u), layer)
        rows = lax.broadcasted_iota(jnp.int32, (x_ref.shape[0], 1), 0)
        x = jnp.where(rows < blk_valid_ref[b], x_ref[...], 0.0).astype(BF16)
        gte = _dot(x, wg_sc[...])
        up = _dot(x, wu_sc[...])
        h_ref[...] = (gte * _sigmoid(gte) * up).astype(BF16)


def _ffn_down_kernel(blk_valid_ref, blk_seg_ref, blk_first_ref, seg_e_ref, cnt_ref, h_ref, wd_hbm, y_ref,
                     wd_st, wd_sc, sem_d, *, layer):
    b = pl.program_id(0)

    @pl.when(b < cnt_ref[0])
    def _():
        _expert_weights(b, (blk_seg_ref, blk_first_ref, seg_e_ref, cnt_ref), (wd_hbm,), (wd_st,), (wd_sc,),
                        (sem_d,), layer)
        y_ref[...] = _dot(h_ref[...], wd_sc[...])


def _ffn(xs_sorted, plan, w_gate, w_up, w_down, layer):
    n_rows, d = xs_sorted.shape
    ff = w_gate.shape[-1]
    blk = MOE_BLOCK
    n_blk = n_rows // blk
    prefetch = (plan["blk_valid"], plan["blk_seg"], plan["blk_first"], plan["seg_e"], plan["cnt"])
    row_map = lambda b, v, s, f, e, c: (jnp.minimum(b, c[0] - 1), 0)
    hbm = pl.BlockSpec(memory_space=pl.ANY)
    dma2 = pltpu.SemaphoreType.DMA((2,))
    up_spec = pltpu.PrefetchScalarGridSpec(
        num_scalar_prefetch=len(prefetch),
        grid=(n_blk,),
        in_specs=[pl.BlockSpec((blk, d), row_map), hbm, hbm],
        out_specs=pl.BlockSpec((blk, ff), row_map),
        scratch_shapes=[pltpu.VMEM((2, d, ff), F32), pltpu.VMEM((2, d, ff), F32),
                        pltpu.VMEM((d, ff), BF16), pltpu.VMEM((d, ff), BF16), dma2, dma2],
    )
    hidden = pl.pallas_call(
        functools.partial(_ffn_up_kernel, layer=layer),
        grid_spec=up_spec,
        out_shape=jax.ShapeDtypeStruct((n_rows, ff), BF16),
        compiler_params=_params(1, 2 * blk * d * 4 + 4 * d * ff * 4 + 2 * d * ff * 2 + 2 * blk * ff * 2
                                + 4 * blk * ff * 4 + blk * d * 2),
        name="ffn_up",
    )(*prefetch, xs_sorted, w_gate, w_up)
    down_spec = pltpu.PrefetchScalarGridSpec(
        num_scalar_prefetch=len(prefetch),
        grid=(n_blk,),
        in_specs=[pl.BlockSpec((blk, ff), row_map), hbm],
        out_specs=pl.BlockSpec((blk, d), row_map),
        scratch_shapes=[pltpu.VMEM((2, ff, d), F32), pltpu.VMEM((ff, d), BF16), dma2],
    )
    return pl.pallas_call(
        functools.partial(_ffn_down_kernel, layer=layer),
        grid_spec=down_spec,
        out_shape=jax.ShapeDtypeStruct((n_rows, d), F32),
        compiler_params=_params(1, 2 * blk * ff * 2 + 2 * d * ff * 4 + d * ff * 2 + 3 * blk * d * 4),
        name="ffn_down",
    )(*prefetch, hidden, w_down)


def _combine_kernel(src_ref, y_hbm, xs_ref, ws_ref, gate_ref, fg_ref, o_ref, ybuf, sem, *, tm, n_ctx, final):
    i = pl.program_id(0)

    def issue(r, carry):
        for k in range(2):
            _row_copy(y_hbm, src_ref[0, 0, k * tm + r], ybuf.at[k], r, sem).start()
        return carry

    lax.fori_loop(0, tm, issue, 0, unroll=DMA_UNROLL)
    for k in range(2):
        pltpu.make_async_copy(y_hbm.at[pl.ds(0, tm)], ybuf.at[k], sem).wait()

    wcol = jnp.transpose(ws_ref[...])
    y = wcol[:, 0:1] * ybuf[0] + wcol[:, 1:2] * ybuf[1]
    x = xs_ref[...] + _row_select(i * tm, tm, n_ctx, gate_ref) * y
    if final:
        ms = jnp.mean(x * x, axis=-1, keepdims=True)
        x = x * lax.rsqrt(ms + RMS_EPS) * fg_ref[...]
    o_ref[...] = x


def _combine(y_sorted, dest_blocks, xs, ws, mod_l, final_g, n_ctx, tm, final):
    t, d = xs.shape
    if final:
        skip = n_ctx // tm
        out_rows = t - n_ctx
        out_map = lambda i: (jnp.maximum(i - skip, 0), 0)
    else:
        out_rows = t
        out_map = lambda i: (i, 0)
    return pl.pallas_call(
        functools.partial(_combine_kernel, tm=tm, n_ctx=n_ctx, final=final),
        grid=(t // tm,),
        in_specs=[pl.BlockSpec((1, 1, 2 * tm), lambda i: (i, 0, 0), memory_space=pltpu.SMEM),
                  pl.BlockSpec(memory_space=pl.ANY),
                  pl.BlockSpec((tm, d), lambda i: (i, 0)),
                  pl.BlockSpec((8, tm), lambda i: (0, i)),
                  pl.BlockSpec((8, d), lambda i: (0, 5)),
                  pl.BlockSpec((1, d), lambda i: (0, 0))],
        out_specs=pl.BlockSpec((tm, d), out_map),
        out_shape=jax.ShapeDtypeStruct((out_rows, d), F32),
        scratch_shapes=[pltpu.VMEM((2, tm, d), F32), pltpu.SemaphoreType.DMA(())],
        compiler_params=_params(1, 2 * tm * d * 4 + 4 * tm * d * 4 + 4 * tm * d * 4),
        name="combine",
    )(dest_blocks, y_sorted, xs, ws, mod_l, final_g)


def _rope_tables(n_lat, n_ctx):
    rows = n_lat // GRID_W
    row_ids = jnp.repeat(jnp.arange(rows, dtype=F32), GRID_W)
    col_ids = jnp.tile(jnp.arange(GRID_W, dtype=F32), rows)
    n_freq = HEAD_DIM // 4
    inv = ROPE_THETA ** (-jnp.arange(n_freq, dtype=F32) / n_freq)
    ar = row_ids[:, None] * inv
    ac = col_ids[:, None] * inv
    ang = jnp.concatenate([ar, ar, ac, ac], axis=-1)
    sign = jnp.where((jnp.arange(HEAD_DIM) % (HEAD_DIM // 2)) < HEAD_DIM // 4, -1.0, 1.0).astype(F32)
    cos = jnp.concatenate([jnp.ones((n_ctx, HEAD_DIM), F32), jnp.cos(ang)], axis=0)
    sin = jnp.concatenate([jnp.zeros((n_ctx, HEAD_DIM), F32), jnp.sin(ang) * sign], axis=0)
    return cos, sin


def _moe_plan(counts, n_blk):
    blk = MOE_BLOCK
    n_exp = counts.shape[0]
    padded = (counts + blk - 1) // blk * blk
    pad_end = jnp.cumsum(padded)
    pad_start = pad_end - padded
    n_used = jnp.maximum(pad_end[-1] // blk, 1)
    b = jnp.minimum(jnp.arange(n_blk, dtype=jnp.int32), n_used - 1)
    blk_e = jnp.sum((pad_end[None, :] <= (b * blk)[:, None]).astype(jnp.int32), axis=1)
    blk_e = jnp.minimum(blk_e, n_exp - 1)
    blk_valid = jnp.clip(counts[blk_e] - (b * blk - pad_start[blk_e]), 0, blk).astype(jnp.int32)
    live = counts > 0
    seg_of_e = jnp.cumsum(live.astype(jnp.int32)) - 1
    n_seg = jnp.maximum(seg_of_e[-1] + 1, 1)
    ks = jnp.arange(n_exp, dtype=jnp.int32)
    seg_e = jnp.sum(jnp.where(live[None, :] & (seg_of_e[None, :] == ks[:, None]), ks[None, :], 0), axis=1)
    blk_first = jnp.concatenate([jnp.ones((1,), jnp.int32), (blk_e[1:] != blk_e[:-1]).astype(jnp.int32)])
    return {
        "pad_start": pad_start.astype(jnp.int32),
        "blk_valid": blk_valid,
        "blk_seg": seg_of_e[blk_e].astype(jnp.int32),
        "blk_first": blk_first,
        "seg_e": seg_e.astype(jnp.int32),
        "cnt": jnp.stack([n_used, n_seg]).astype(jnp.int32),
    }


def kernel(x, c, ctx, c_ctx, w_ada, b_ada, norm1_g, norm2_g, w_in, pool_w, pool_scale, q_norm_g, k_norm_g,
           sink, w_branch, w_out, router_w, router_b, w_gate, w_up, w_down, final_g):
    assert x.shape[0] == 1, "single-sequence kernel"
    n_lat, d = x.shape[1], x.shape[2]
    n_ctx = ctx.shape[1]
    t = n_ctx + n_lat
    mix = d // 2
    depth = w_in.shape[0]
    n_exp = router_w.shape[1]
    tok = _pick(n_ctx, (256, 128))
    assert (2 * t) % MOE_BLOCK == 0
    n_blk = 2 * t // MOE_BLOCK + n_exp
    cos, sin_signed = _rope_tables(n_lat, n_ctx)
    mod = _ada(c, c_ctx, w_ada, b_ada)
    xs = jnp.concatenate([ctx[0], x[0]], axis=0)
    for l in range(depth):
        last = l == depth - 1
        p = _inproj(xs, mod[l], norm1_g[l][None], w_in, l, cos, sin_signed,
                    q_norm_g[l][None], k_norm_g[l][None], n_ctx)
        og = _gattn(p, n_ctx, mix)
        ow = _wattn(p, sink[l], n_ctx, mix)
        z = _merge(p, og, ow, pool_w, pool_scale[l][None], w_branch, l, n_ctx, d)
        xs = _outproj(z, w_out, l, xs, mod[l], n_ctx)
        h2, ints, ws, cnt = _router(xs, mod[l], norm2_g[l][None], router_w, router_b, n_ctx)
        plan = _moe_plan(cnt[:, 0].astype(jnp.int32), n_blk)
        dest_blocks = _dest_blocks(ints, plan["pad_start"], tok)
        xs_sorted = _dispatch(h2, dest_blocks, n_blk * MOE_BLOCK, tok)
        y_sorted = _ffn(xs_sorted, plan, w_gate, w_up, w_down, l)
        xs = _combine(y_sorted, dest_blocks, xs, ws, mod[l], final_g[None], n_ctx, tok, last)
    return xs[None]
```

```python
import functools

import jax
import jax.numpy as jnp
from jax import lax
from jax.experimental import pallas as pl
from jax.experimental.pallas import tpu as pltpu

HEAD_DIM = 128
LANES = 128
BF16_ROWS = 16
KV_HEADS = 2
GRID_W = 64
WINDOW = 128
POOL_WINDOWS = (2, 4, 8, 16)
POOL_HALO = 16
N_GROUPS = 4
ROPE_THETA = 10000.0
RMS_EPS = 1e-6
NEG_INF = -1e30
ATTN_SCALE = HEAD_DIM ** -0.5
LOG2E = 1.4426950408889634
SOFTMAX_SAFE_LOG2 = 50.0
MOE_BLOCK = 256
DMA_UNROLL = 8
VMEM_CAP = 60 * 1024 * 1024

BF16 = jnp.bfloat16
F32 = jnp.float32


def _pick(n, prefs):
    for p in prefs:
        if n % p == 0:
            return p
    raise ValueError(f"no tile in {prefs} divides {n}")


def _largest_tile(n, cap, mult):
    return max(k for k in range(mult, min(n, cap) + 1, mult) if n % k == 0)


def _params(n_axes, vmem_bytes):
    limit = int(min(max(vmem_bytes * 5 // 4 + (4 << 20), 16 << 20), VMEM_CAP))
    return pltpu.CompilerParams(dimension_semantics=("arbitrary",) * n_axes, vmem_limit_bytes=limit)


def _sigmoid(x):
    return 1.0 / (1.0 + jnp.exp(-x))


def _sigmoid_tanh(x):
    return 0.5 * jnp.tanh(0.5 * x) + 0.5


def _dot(a, b):
    return jnp.dot(a, b, preferred_element_type=F32)


def _dot_nt(a, b):
    return lax.dot_general(a, b, (((1,), (1,)), ((), ())), preferred_element_type=F32)


def _row_select(row0, n_rows, n_ctx, mod_ref):
    rows = row0 + lax.broadcasted_iota(jnp.int32, (n_rows, 1), 0)
    return jnp.where(rows < n_ctx, mod_ref[1:2, :], mod_ref[0:1, :])


def _norm_modulate(x, g_ref, shift, scale):
    ms = jnp.mean(x * x, axis=-1, keepdims=True)
    y = x * lax.rsqrt(ms + RMS_EPS) * g_ref[...]
    return y * (1.0 + scale) + shift


def _ada_kernel(cb_ref, w_ref, b_ref, o_ref, *, tn):
    s = cb_ref[...]
    s = s * _sigmoid(s)
    outs = []
    for r in range(2):
        cols = [jnp.sum(w_ref[:, c * LANES:(c + 1) * LANES] * s[r], axis=0, keepdims=True)
                for c in range(tn // LANES)]
        outs.append(jnp.concatenate(cols, axis=1) + b_ref[...])
    o_ref[...] = jnp.concatenate(outs + [jnp.zeros((6, tn), F32)], axis=0)


def _ada(c, c_ctx, w_ada, b_ada):
    depth, d, w6 = w_ada.shape
    tn = _pick(w6, (1024, 512, 256, 128))
    cb = jnp.broadcast_to(jnp.stack([c[0], c_ctx])[:, :, None], (2, d, LANES))
    vmem = 2 * d * tn * 4 + 2 * d * LANES * 4 * 2 + d * LANES * 4 * 4
    return pl.pallas_call(
        functools.partial(_ada_kernel, tn=tn),
        grid=(depth, w6 // tn),
        in_specs=[pl.BlockSpec((2, d, LANES), lambda l, j: (0, 0, 0)),
                  pl.BlockSpec((None, d, tn), lambda l, j: (l, 0, j)),
                  pl.BlockSpec((None, 1, tn), lambda l, j: (l, 0, j))],
        out_specs=pl.BlockSpec((None, 8, tn), lambda l, j: (l, 0, j)),
        out_shape=jax.ShapeDtypeStruct((depth, 8, w6), F32),
        compiler_params=_params(2, vmem),
        name="ada",
    )(cb, w_ada, b_ada.reshape(depth, 1, w6))


def _rope(x, cos, sin_signed, first_half):
    rot = jnp.where(first_half, pltpu.roll(x, 3 * HEAD_DIM // 4, axis=1), pltpu.roll(x, HEAD_DIM // 4, axis=1))
    return x * cos + rot * sin_signed


def _head_rms(x, g_ref):
    ms = jnp.mean(x * x, axis=-1, keepdims=True)
    return x * lax.rsqrt(ms + RMS_EPS) * g_ref[...]


def _inproj_kernel(x_ref, shift_ref, scale_ref, g_ref, w_ref, cos_ref, sin_ref, qg_ref, kg_ref,
                   o_ref, h_sc, *, tm, tn, n_ctx, signatures):
    i = pl.program_id(0)
    j = pl.program_id(1)

    def normed():
        x = x_ref[...]
        return x * lax.rsqrt(jnp.mean(x * x, axis=-1, keepdims=True) + RMS_EPS)

    has_ctx_rows = i * tm < n_ctx

    @pl.when((j == 0) & has_ctx_rows)
    def _():
        gain = g_ref[...] * (1.0 + scale_ref[0:2, :])
        rows = i * tm + lax.broadcasted_iota(jnp.int32, (tm, 1), 0)
        is_ctx = rows < n_ctx
        h = normed() * jnp.where(is_ctx, gain[1:2], gain[0:1]) + jnp.where(is_ctx, shift_ref[1:2, :], shift_ref[0:1, :])
        h_sc[...] = h.astype(BF16)

    @pl.when((j == 0) & jnp.logical_not(has_ctx_rows))
    def _():
        h = normed() * (g_ref[...] * (1.0 + scale_ref[0:1, :])) + shift_ref[0:1, :]
        h_sc[...] = h.astype(BF16)

    acc = _dot(h_sc[...], w_ref[...].astype(BF16))

    lane = lax.broadcasted_iota(jnp.int32, (1, HEAD_DIM), 1)
    first_half = (lane % (HEAD_DIM // 2)) < (HEAD_DIM // 4)

    for sig, js in signatures:
        cond = functools.reduce(jnp.logical_or, [j == jj for jj in js])

        @pl.when(cond)
        def _(sig=sig):
            if all(t == "plain" for t in sig):
                o_ref[...] = acc.astype(BF16)
                return
            for bi, typ in enumerate(sig):
                a = acc[:, bi * LANES:(bi + 1) * LANES]
                if typ != "plain":
                    if typ in ("qnr", "knr"):
                        a = _head_rms(a, qg_ref if typ == "qnr" else kg_ref)
                    a = _rope(a, cos_ref[...], sin_ref[...], first_half)
                    if typ in ("qnr", "qr"):
                        a = a * (ATTN_SCALE * LOG2E)
                o_ref[:, bi * LANES:(bi + 1) * LANES] = a.astype(BF16)


def _col_types(mix, d):
    hq = mix // HEAD_DIM
    types = (["plain"] * hq + ["qnr"] * hq + ["knr"] * KV_HEADS + ["plain"] * KV_HEADS
             + ["qr"] * hq + ["kr"] * KV_HEADS + ["plain"] * KV_HEADS + ["plain"] * (3 * d // LANES))
    return types


def _inproj(xs, mod_l, g1, w_in, layer, cos, sin_signed, qg, kg, n_ctx):
    t, d = xs.shape
    in_w = w_in.shape[-1]
    mix = d // 2
    tm = _pick(t, (768, 512, 256, 128))
    tn = _pick(in_w, (1024, 512, 256))
    types = _col_types(mix, d)
    assert len(types) * LANES == in_w
    per = tn // LANES
    sigs = {}
    for jj in range(in_w // tn):
        sigs.setdefault(tuple(types[jj * per:(jj + 1) * per]), []).append(jj)
    signatures = tuple((s, tuple(js)) for s, js in sigs.items())
    vmem = (2 * tm * d * 4 + 2 * d * tn * 4 + tm * d * 2 + 2 * tm * tn * 2 + tm * tn * 4 * 2 + d * tn * 2
            + 4 * tm * LANES * 4)
    return pl.pallas_call(
        functools.partial(_inproj_kernel, tm=tm, tn=tn, n_ctx=n_ctx, signatures=signatures),
        grid=(t // tm, in_w // tn),
        in_specs=[pl.BlockSpec((tm, d), lambda i, j: (i, 0)),
                  pl.BlockSpec((8, d), lambda i, j: (0, 0)),
                  pl.BlockSpec((8, d), lambda i, j: (0, 1)),
                  pl.BlockSpec((1, d), lambda i, j: (0, 0)),
                  pl.BlockSpec((None, d, tn), lambda i, j: (layer, 0, j)),
                  pl.BlockSpec((tm, HEAD_DIM), lambda i, j: (i, 0)),
                  pl.BlockSpec((tm, HEAD_DIM), lambda i, j: (i, 0)),
                  pl.BlockSpec((1, HEAD_DIM), lambda i, j: (0, 0)),
                  pl.BlockSpec((1, HEAD_DIM), lambda i, j: (0, 0))],
        out_specs=pl.BlockSpec((tm, tn), lambda i, j: (i, j)),
        out_shape=jax.ShapeDtypeStruct((t, in_w), BF16),
        scratch_shapes=[pltpu.VMEM((tm, d), BF16)],
        compiler_params=_params(2, vmem),
        name="inproj",
    )(xs, mod_l, mod_l, g1, w_in, cos, sin_signed, qg, kg)


def _lane_repeat(x, n):
    return jnp.concatenate([x] * n, axis=1)


def _stack_heads(q_ref, g):
    return jnp.concatenate([q_ref[:, h * HEAD_DIM:(h + 1) * HEAD_DIM] for h in range(g)], axis=0)


def _unstack_store(o_ref, out, g, tq):
    for h in range(g):
        o_ref[:, h * HEAD_DIM:(h + 1) * HEAD_DIM] = out[h * tq:(h + 1) * tq, :].astype(o_ref.dtype)


def _gattn_kernel(q_ref, k_ref, v_ref, o_ref, vx_sc, kmax_sc, m_sc, acc_sc, *, tq, tk, g, n_ctx, n_lat):
    qi = pl.program_id(1)

    @pl.when(qi == 0)
    def _():
        vx_sc[:, 0:HEAD_DIM] = v_ref[...]
        vx_sc[:, HEAD_DIM:] = jnp.ones((vx_sc.shape[0], HEAD_DIM), BF16)
        kf = k_ref[...].astype(F32)
        k2 = jnp.max(jnp.sum(kf * kf, axis=-1, keepdims=True), axis=0, keepdims=True)
        kmax_sc[...] = jnp.broadcast_to(jnp.sqrt(k2), kmax_sc.shape)

    q = _stack_heads(q_ref, g)
    rows = q.shape[0]
    qf = q.astype(F32)
    bound = jnp.sqrt(jnp.sum(qf * qf, axis=-1, keepdims=True)) * kmax_sc[0:1, :]
    bounded = jnp.max(bound) <= SOFTMAX_SAFE_LOG2
    ctx_only = qi * tq < n_ctx
    ctx_chunk = [(0, n_ctx)]
    all_chunks = ctx_chunk + [(n_ctx + c * tk, tk) for c in range(n_lat // tk)]

    def finish():
        acc = acc_sc[...]
        _unstack_store(o_ref, acc[:, :HEAD_DIM] / acc[:, HEAD_DIM:], g, tq)

    def attend_bounded(chunks):
        for n, (lo, size) in enumerate(chunks):
            s = _dot_nt(q, k_ref[lo:lo + size, :])
            p = jnp.exp2(s - _lane_repeat(bound, size // LANES))
            pv = _dot(p.astype(BF16), vx_sc[lo:lo + size, :])
            acc_sc[...] = pv if n == 0 else acc_sc[...] + pv
        finish()

    def attend_online(chunks):
        for n, (lo, size) in enumerate(chunks):
            s = _dot_nt(q, k_ref[lo:lo + size, :])
            mx = jnp.max(s, axis=-1, keepdims=True)
            if n == 0:
                m_new = jnp.broadcast_to(mx, (rows, LANES))
            else:
                m_prev = m_sc[...]
                m_new = jnp.maximum(m_prev, mx)
            p = jnp.exp2(s - _lane_repeat(m_new, size // LANES))
            pv = _dot(p.astype(BF16), vx_sc[lo:lo + size, :])
            if n == 0:
                acc_sc[...] = pv
            else:
                acc_sc[...] = _lane_repeat(jnp.exp2(m_prev - m_new), 2) * acc_sc[...] + pv
            m_sc[...] = m_new
        finish()

    for use_bound, attend in ((True, attend_bounded), (False, attend_online)):
        path = bounded if use_bound else jnp.logical_not(bounded)
        pl.when(path & ctx_only)(functools.partial(attend, ctx_chunk))
        pl.when(path & jnp.logical_not(ctx_only))(functools.partial(attend, all_chunks))


def _gattn(p, n_ctx, mix):
    t = p.shape[0]
    hq = mix // HEAD_DIM
    g = hq // KV_HEADS
    n_lat = t - n_ctx
    tq = _pick(n_ctx, (256, 128))
    assert t % tq == 0
    tk = _pick(n_lat, (2048, 1024, 512, 256, 128))
    gw = g * HEAD_DIM
    off_q = mix // gw
    off_k = (mix + hq * HEAD_DIM) // HEAD_DIM
    off_v = off_k + KV_HEADS
    rows = g * tq
    vmem = (2 * tq * gw * 2 * 2 + 2 * 2 * t * HEAD_DIM * 2 + t * 2 * HEAD_DIM * 2 + rows * LANES * 4 * 4
            + rows * max(tk, n_ctx) * 4 * 3)
    return pl.pallas_call(
        functools.partial(_gattn_kernel, tq=tq, tk=tk, g=g, n_ctx=n_ctx, n_lat=n_lat),
        grid=(KV_HEADS, t // tq),
        in_specs=[pl.BlockSpec((tq, gw), lambda h, i: (i, off_q + h)),
                  pl.BlockSpec((t, HEAD_DIM), lambda h, i: (0, off_k + h)),
                  pl.BlockSpec((t, HEAD_DIM), lambda h, i: (0, off_v + h))],
        out_specs=pl.BlockSpec((tq, gw), lambda h, i: (i, h)),
        out_shape=jax.ShapeDtypeStruct((t, mix), BF16),
        scratch_shapes=[pltpu.VMEM((t, 2 * HEAD_DIM), BF16), pltpu.VMEM((8, LANES), F32),
                        pltpu.VMEM((rows, LANES), F32), pltpu.VMEM((rows, 2 * HEAD_DIM), F32)],
        compiler_params=_params(2, vmem),
        name="gattn",
    )(p, p, p)


def _wattn_kernel(*refs, tq, g, n_ctx, span, t):
    q_refs, k_refs, v_refs = refs[0:KV_HEADS], refs[KV_HEADS:2 * KV_HEADS], refs[2 * KV_HEADS:3 * KV_HEADS]
    sink_ref, o_ref, vx_sc = refs[3 * KV_HEADS:]
    qi = pl.program_id(0)

    @pl.when(qi == 0)
    def _():
        for h in range(KV_HEADS):
            vx_sc[h, :, 0:HEAD_DIM] = v_refs[h][...]
            vx_sc[h, :, HEAD_DIM:] = jnp.ones((t, HEAD_DIM), BF16)

    for h in range(KV_HEADS):
        _wattn_head(qi, q_refs[h], k_refs[h], vx_sc.at[h], sink_ref.at[h], o_ref.at[:, h * g * HEAD_DIM:(h + 1) * g * HEAD_DIM],
                    tq=tq, g=g, n_ctx=n_ctx, span=span, t=t)


def _wattn_head(qi, q_ref, k_ref, vx_sc, sink_ref, o_ref, *, tq, g, n_ctx, span, t):
    q = _stack_heads(q_ref, g)
    rows = q.shape[0]
    start = pl.multiple_of(jnp.clip(qi * tq - WINDOW, 0, t - span), LANES)

    far = t + 4 * WINDOW
    qrow = qi * tq + lax.broadcasted_iota(jnp.int32, (tq, 1), 0)
    qrow = jnp.concatenate([jnp.where(qrow >= n_ctx, qrow, -far)] * g, axis=0)
    krow = start + lax.broadcasted_iota(jnp.int32, (1, span), 1)
    krow = jnp.where(krow >= n_ctx, krow, far)
    keep = jnp.abs(krow - qrow) <= WINDOW

    s_c = _dot_nt(q, k_ref[0:n_ctx, :])
    s_w = jnp.where(keep, _dot_nt(q, k_ref[pl.ds(start, span), :]), NEG_INF)
    sink = sink_ref[...] * LOG2E
    mx = jnp.maximum(jnp.maximum(jnp.max(s_c, axis=-1, keepdims=True), jnp.max(s_w, axis=-1, keepdims=True)), sink)
    m = jnp.broadcast_to(mx, (rows, LANES))
    p_c = jnp.exp2(s_c - _lane_repeat(m, n_ctx // LANES))
    p_w = jnp.exp2(s_w - _lane_repeat(m, span // LANES))
    acc = _dot(p_c.astype(BF16), vx_sc[0:n_ctx, :]) + _dot(p_w.astype(BF16), vx_sc[pl.ds(start, span), :])
    denom = acc[:, HEAD_DIM:] + jnp.exp2(sink - m)
    _unstack_store(o_ref, acc[:, :HEAD_DIM] / denom, g, tq)


def _wattn(p, sink_l, n_ctx, mix):
    t = p.shape[0]
    hq = mix // HEAD_DIM
    g = hq // KV_HEADS
    tq = _pick(n_ctx, (256, 128))
    span = tq + 2 * WINDOW
    assert t % tq == 0 and t >= span
    gw = g * HEAD_DIM
    base = mix + hq * HEAD_DIM + 2 * KV_HEADS * HEAD_DIM
    off_q = base // gw
    off_k = (base + hq * HEAD_DIM) // HEAD_DIM
    off_v = off_k + KV_HEADS
    rows = g * tq
    sink_rows = jnp.repeat(sink_l.reshape(KV_HEADS, g), tq, axis=1).reshape(KV_HEADS, rows, 1)
    vmem = KV_HEADS * (2 * tq * gw * 2 * 2 + 2 * 2 * t * HEAD_DIM * 2 + 2 * rows * LANES * 4
                       + rows * (span + n_ctx) * 4 * 3 + t * 2 * HEAD_DIM * 2)
    heads = range(KV_HEADS)
    return pl.pallas_call(
        functools.partial(_wattn_kernel, tq=tq, g=g, n_ctx=n_ctx, span=span, t=t),
        grid=(t // tq,),
        in_specs=([pl.BlockSpec((tq, gw), lambda i, h=h: (i, off_q + h)) for h in heads]
                  + [pl.BlockSpec((t, HEAD_DIM), lambda i, h=h: (0, off_k + h)) for h in heads]
                  + [pl.BlockSpec((t, HEAD_DIM), lambda i, h=h: (0, off_v + h)) for h in heads]
                  + [pl.BlockSpec((KV_HEADS, rows, 1), lambda i: (0, 0, 0))]),
        out_specs=pl.BlockSpec((tq, mix), lambda i: (i, 0)),
        out_shape=jax.ShapeDtypeStruct((t, mix), BF16),
        scratch_shapes=[pltpu.VMEM((KV_HEADS, t, 2 * HEAD_DIM), BF16)],
        compiler_params=_params(1, vmem),
        name="wattn",
    )(*([p] * (3 * KV_HEADS)), sink_rows)


def _merge_kernel(u_ref, up_ref, un_ref, og_ref, ow_ref, g0_ref, g1_ref, g2_ref, pw_ref, ps_ref, wb_ref,
                  z_ref, ext_sc, pool_sc, *, tm, n_ctx, t, mix):
    i = pl.program_id(0)
    j = pl.program_id(1)
    gw = mix // len(POOL_WINDOWS)

    def pool(clipped):
        ext_sc[0:POOL_HALO, :] = up_ref[...].astype(F32)
        ext_sc[POOL_HALO:POOL_HALO + tm, :] = u_ref[...].astype(F32)
        ext_sc[POOL_HALO + tm:, :] = un_ref[...].astype(F32)
        r = i * tm + lax.broadcasted_iota(jnp.int32, (tm, 1), 0)
        r_ctx = r < n_ctx
        for gi, w in enumerate(POOL_WINDOWS):
            c0, c1 = gi * gw, (gi + 1) * gw
            tot = jnp.zeros((tm, gw), F32)
            cnt = jnp.zeros((tm, 1), F32)
            for off in range(-((w - 1) // 2), w // 2 + 1):
                part = ext_sc[POOL_HALO + off:POOL_HALO + off + tm, c0:c1]
                if clipped:
                    rr = r + off
                    ok = (rr >= 0) & (rr < t) & ((rr < n_ctx) == r_ctx)
                    part = jnp.where(ok, part, 0.0)
                    cnt = cnt + ok.astype(F32)
                tot = tot + part
            mean = tot / cnt if clipped else tot * (1.0 / w)
            dlt = mean - ext_sc[POOL_HALO:POOL_HALO + tm, c0:c1]
            y = _dot(dlt.astype(BF16), pw_ref[gi].astype(BF16)) * ps_ref[:, c0:c1]
            pool_sc[:, c0:c1] = y.astype(BF16)

    reach = max(POOL_WINDOWS) // 2
    interior = (i * tm - reach >= n_ctx) & ((i + 1) * tm + reach <= t)
    pl.when((j == 0) & interior)(functools.partial(pool, False))
    pl.when((j == 0) & jnp.logical_not(interior))(functools.partial(pool, True))

    def gate(g_ref):
        return _sigmoid_tanh(g_ref[...].astype(F32))

    z = gate(g0_ref) * _dot(pool_sc[...], wb_ref[0].astype(BF16))
    z = z + gate(g1_ref) * _dot(og_ref[...], wb_ref[1].astype(BF16))
    z = z + gate(g2_ref) * _dot(ow_ref[...], wb_ref[2].astype(BF16))
    z_ref[...] = z.astype(BF16)


def _merge(p, og, ow, pool_w, pool_scale, w_branch, layer, n_ctx, d):
    t = p.shape[0]
    mix = d // 2
    tm = _largest_tile(t, 1100, POOL_HALO)
    tn = _pick(d, (512, 256, 128))
    hb = tm // POOL_HALO
    n_hb = t // POOL_HALO
    gate0 = (p.shape[1] - 3 * d) // tn
    gs = pool_w.shape[-1]
    vmem = (2 * 3 * tm * mix * 2 + 2 * 3 * tm * tn * 2 + 2 * 3 * mix * tn * 4 + 3 * mix * tn * 2
            + (tm + 2 * POOL_HALO) * mix * 4 + tm * mix * 2 + 2 * tm * tn * 2 + 4 * tm * tn * 4
            + 2 * len(POOL_WINDOWS) * gs * gs * 4 + 6 * tm * gs * 4)
    return pl.pallas_call(
        functools.partial(_merge_kernel, tm=tm, n_ctx=n_ctx, t=t, mix=mix),
        grid=(t // tm, d // tn),
        in_specs=[pl.BlockSpec((tm, mix), lambda i, j: (i, 0)),
                  pl.BlockSpec((POOL_HALO, mix), lambda i, j: (jnp.maximum(i * hb - 1, 0), 0)),
                  pl.BlockSpec((POOL_HALO, mix), lambda i, j: (jnp.minimum((i + 1) * hb, n_hb - 1), 0)),
                  pl.BlockSpec((tm, mix), lambda i, j: (i, 0)),
                  pl.BlockSpec((tm, mix), lambda i, j: (i, 0)),
                  pl.BlockSpec((tm, tn), lambda i, j: (i, gate0 + j)),
                  pl.BlockSpec((tm, tn), lambda i, j: (i, gate0 + d // tn + j)),
                  pl.BlockSpec((tm, tn), lambda i, j: (i, gate0 + 2 * (d // tn) + j)),
                  pl.BlockSpec((None, len(POOL_WINDOWS), gs, gs), lambda i, j: (layer, 0, 0, 0)),
                  pl.BlockSpec((1, mix), lambda i, j: (0, 0)),
                  pl.BlockSpec((None, 3, mix, tn), lambda i, j: (layer, 0, 0, j))],
        out_specs=pl.BlockSpec((tm, tn), lambda i, j: (i, j)),
        out_shape=jax.ShapeDtypeStruct((t, d), BF16),
        scratch_shapes=[pltpu.VMEM((tm + 2 * POOL_HALO, mix), F32), pltpu.VMEM((tm, mix), BF16)],
        compiler_params=_params(2, vmem),
        name="merge",
    )(p, p, p, og, ow, p, p, p, pool_w, pool_scale, w_branch)


def _outproj_kernel(z_ref, w_ref, xs_ref, gate_ref, o_ref, *, tm, n_ctx):
    i = pl.program_id(0)
    gate = _row_select(i * tm, tm, n_ctx, gate_ref)
    o_ref[...] = xs_ref[...] + gate * _dot(z_ref[...], w_ref[...].astype(BF16))


def _outproj(z, w_out, layer, xs, mod_l, n_ctx):
    t, d = xs.shape
    tm = _largest_tile(t, 1536, BF16_ROWS)
    tn = _pick(d, (512, 256, 128))
    vmem = 2 * tm * d * 2 + 2 * d * tn * 4 + d * tn * 2 + 4 * tm * tn * 4 + 2 * tm * tn * 4
    return pl.pallas_call(
        functools.partial(_outproj_kernel, tm=tm, n_ctx=n_ctx),
        grid=(t // tm, d // tn),
        in_specs=[pl.BlockSpec((tm, d), lambda i, j: (i, 0)),
                  pl.BlockSpec((None, d, tn), lambda i, j: (layer, 0, j)),
                  pl.BlockSpec((tm, tn), lambda i, j: (i, j)),
                  pl.BlockSpec((8, tn), lambda i, j: (0, 2 * (d // tn) + j))],
        out_specs=pl.BlockSpec((tm, tn), lambda i, j: (i, j)),
        out_shape=jax.ShapeDtypeStruct((t, d), F32),
        compiler_params=_params(2, vmem),
        name="outproj",
    )(z, w_out, xs, mod_l)


def _router_kernel(xs_ref, shift_ref, scale_ref, g_ref, rwt_ref, rb_ref,
                   h_ref, ints_ref, ws_ref, cnt_ref, carry_sc, *, tm, n_ctx, n_exp):
    i = pl.program_id(0)

    @pl.when(i == 0)
    def _():
        carry_sc[...] = jnp.zeros(carry_sc.shape, F32)

    shift = _row_select(i * tm, tm, n_ctx, shift_ref)
    scale = _row_select(i * tm, tm, n_ctx, scale_ref)
    h = _norm_modulate(xs_ref[...], g_ref, shift, scale)
    h_ref[...] = h

    scores = _sigmoid(_dot_nt(rwt_ref[...].astype(BF16), h.astype(BF16)))
    sel = scores + rb_ref[...]
    per = n_exp // N_GROUPS
    sub = lax.broadcasted_iota(jnp.int32, (per, tm), 0)

    def top2(v):
        m1 = jnp.max(v, axis=0, keepdims=True)
        i1 = jnp.min(jnp.where(v == m1, sub, per), axis=0, keepdims=True)
        rest = jnp.where(sub == i1, -jnp.inf, v)
        m2 = jnp.max(rest, axis=0, keepdims=True)
        i2 = jnp.min(jnp.where(rest == m2, sub, per), axis=0, keepdims=True)
        return m1 + m2, i1, i2

    tops = [top2(sel[gi * per:(gi + 1) * per, :]) for gi in range(N_GROUPS)]
    best, l1, l2 = tops[0]
    grp = jnp.zeros((1, tm), jnp.int32)
    for gi in range(1, N_GROUPS):
        gs, a1, a2 = tops[gi]
        better = gs > best
        best = jnp.where(better, gs, best)
        grp = jnp.where(better, gi, grp)
        l1 = jnp.where(better, a1, l1)
        l2 = jnp.where(better, a2, l2)

    hot1 = [(grp == gi) & (sub == l1) for gi in range(N_GROUPS)]
    hot2 = [(grp == gi) & (sub == l2) for gi in range(N_GROUPS)]
    assign = jnp.concatenate([(a | b).astype(F32) for a, b in zip(hot1, hot2)], axis=0)

    before = (lax.broadcasted_iota(jnp.int32, (tm, tm), 0) < lax.broadcasted_iota(jnp.int32, (tm, tm), 1))
    pos = _dot(assign.astype(BF16), before.astype(F32).astype(BF16)) + carry_sc[...]

    def pick(hots, val):
        return sum(jnp.sum(jnp.where(hots[gi], val[gi * per:(gi + 1) * per, :], 0.0), axis=0, keepdims=True)
                   for gi in range(N_GROUPS))

    s1 = pick(hot1, scores)
    s2 = pick(hot2, scores)
    r1 = pick(hot1, pos)
    r2 = pick(hot2, pos)
    tot = s1 + s2
    ints_ref[0:1, :] = grp * per + l1
    ints_ref[1:2, :] = grp * per + l2
    ints_ref[2:3, :] = r1.astype(jnp.int32)
    ints_ref[3:4, :] = r2.astype(jnp.int32)
    ints_ref[4:8, :] = jnp.zeros((4, tm), jnp.int32)
    ws_ref[0:1, :] = s1 / tot
    ws_ref[1:2, :] = s2 / tot
    ws_ref[2:8, :] = jnp.zeros((6, tm), F32)
    carry_sc[...] = carry_sc[...] + jnp.sum(assign, axis=1, keepdims=True)
    cnt_ref[...] = jnp.broadcast_to(carry_sc[...], cnt_ref.shape)


def _router(xs, mod_l, g2, router_w, router_b, n_ctx):
    t, d = xs.shape
    n_exp = router_w.shape[1]
    tm = _pick(t, (768, 512, 256, 128))
    vmem = 2 * tm * d * 4 * 2 + 4 * tm * d * 4 + 3 * tm * tm * 4 + 2 * n_exp * d * 4
    return pl.pallas_call(
        functools.partial(_router_kernel, tm=tm, n_ctx=n_ctx, n_exp=n_exp),
        grid=(t // tm,),
        in_specs=[pl.BlockSpec((tm, d), lambda i: (i, 0)),
                  pl.BlockSpec((8, d), lambda i: (0, 3)),
                  pl.BlockSpec((8, d), lambda i: (0, 4)),
                  pl.BlockSpec((1, d), lambda i: (0, 0)),
                  pl.BlockSpec((n_exp, d), lambda i: (0, 0)),
                  pl.BlockSpec((n_exp, 1), lambda i: (0, 0))],
        out_specs=[pl.BlockSpec((tm, d), lambda i: (i, 0)),
                   pl.BlockSpec((8, tm), lambda i: (0, i)),
                   pl.BlockSpec((8, tm), lambda i: (0, i)),
                   pl.BlockSpec((n_exp, LANES), lambda i: (0, 0))],
        out_shape=[jax.ShapeDtypeStruct((t, d), F32),
                   jax.ShapeDtypeStruct((8, t), jnp.int32),
                   jax.ShapeDtypeStruct((8, t), F32),
                   jax.ShapeDtypeStruct((n_exp, LANES), F32)],
        scratch_shapes=[pltpu.VMEM((n_exp, 1), F32)],
        compiler_params=_params(1, vmem),
        name="router",
    )(xs, mod_l, mod_l, g2, router_w.T, router_b.reshape(n_exp, 1))


def _row_copy(src_ref, src_row, dst_ref, dst_row, sem):
    return pltpu.make_async_copy(src_ref.at[pl.ds(src_row, 1)], dst_ref.at[pl.ds(dst_row, 1)], sem)


def _dispatch_kernel(dst_ref, h_ref, xs_hbm, sem, *, tm):
    def issue(r, carry):
        for k in range(2):
            _row_copy(h_ref, r, xs_hbm, dst_ref[0, 0, k * tm + r], sem).start(priority=k)
        return carry

    lax.fori_loop(0, tm, issue, 0, unroll=DMA_UNROLL)
    for k in range(2):
        pltpu.make_async_copy(h_ref, xs_hbm.at[pl.ds(0, tm)], sem).wait()


def _dest_blocks(ints, pad_start, tm):
    t = ints.shape[1]
    experts = jnp.arange(pad_start.shape[0], dtype=jnp.int32)
    start = jnp.sum(jnp.where(ints[0:2, :, None] == experts, pad_start, 0), axis=-1)
    dest = start + ints[2:4]
    return dest.reshape(2, t // tm, tm).transpose(1, 0, 2).reshape(t // tm, 1, 2 * tm)


def _dispatch(h2, dest_blocks, n_rows, tm):
    t, d = h2.shape
    return pl.pallas_call(
        functools.partial(_dispatch_kernel, tm=tm),
        grid=(t // tm,),
        in_specs=[pl.BlockSpec((1, 1, 2 * tm), lambda i: (i, 0, 0), memory_space=pltpu.SMEM),
                  pl.BlockSpec((tm, d), lambda i: (i, 0))],
        out_specs=pl.BlockSpec(memory_space=pl.ANY),
        out_shape=jax.ShapeDtypeStruct((n_rows, d), F32),
        scratch_shapes=[pltpu.SemaphoreType.DMA(())],
        compiler_params=_params(1, 2 * tm * d * 4),
        name="dispatch",
    )(dest_blocks, h2)


def _expert_weights(b, plan_refs, w_hbm, stage, cast, sems, layer):
    blk_seg_ref, blk_first_ref, seg_e_ref, cnt_ref = plan_refs
    n_seg = cnt_ref[1]

    def copies(k, slot):
        e = seg_e_ref[k]
        return [pltpu.make_async_copy(w.at[layer, e], st.at[slot], sm.at[slot])
                for w, st, sm in zip(w_hbm, stage, sems)]

    @pl.when(b == 0)
    def _():
        for cp in copies(0, 0):
            cp.start()

        @pl.when(n_seg > 1)
        def _():
            for cp in copies(1, 1):
                cp.start()

    @pl.when(blk_first_ref[b] == 1)
    def _():
        k = blk_seg_ref[b]
        slot = k % 2
        for cp in copies(k, slot):
            cp.wait()
        for st, dst in zip(stage, cast):
            dst[...] = st[slot].astype(BF16)

        @pl.when(k + 2 < n_seg)
        def _():
            for cp in copies(k + 2, slot):
                cp.start()


def _ffn_up_kernel(blk_valid_ref, blk_seg_ref, blk_first_ref, seg_e_ref, cnt_ref, x_ref, wg_hbm, wu_hbm, h_ref,
                   wg_st, wu_st, wg_sc, wu_sc, sem_g, sem_u, *, layer):
    b = pl.program_id(0)

    @pl.when(b < cnt_ref[0])
    def _():
        _expert_weights(b, (blk_seg_ref, blk_first_ref, seg_e_ref, cnt_ref), (wg_hbm, wu_hbm), (wg_st, wu_st),
                        (wg_sc, wu_sc), (sem_g, sem_u), layer)
        rows = lax.broadcasted_iota(jnp.int32, (x_ref.shape[0], 1), 0)
        x = jnp.where(rows < blk_valid_ref[b], x_ref[...], 0.0).astype(BF16)
        gte = _dot(x, wg_sc[...])
        up = _dot(x, wu_sc[...])
        h_ref[...] = (gte * _sigmoid(gte) * up).astype(BF16)


def _ffn_down_kernel(blk_valid_ref, blk_seg_ref, blk_first_ref, seg_e_ref, cnt_ref, h_ref, wd_hbm, y_ref,
                     wd_st, wd_sc, sem_d, *, layer):
    b = pl.program_id(0)

    @pl.when(b < cnt_ref[0])
    def _():
        _expert_weights(b, (blk_seg_ref, blk_first_ref, seg_e_ref, cnt_ref), (wd_hbm,), (wd_st,), (wd_sc,),
                        (sem_d,), layer)
        y_ref[...] = _dot(h_ref[...], wd_sc[...])


def _ffn(xs_sorted, plan, w_gate, w_up, w_down, layer):
    n_rows, d = xs_sorted.shape
    ff = w_gate.shape[-1]
    blk = MOE_BLOCK
    n_blk = n_rows // blk
    prefetch = (plan["blk_valid"], plan["blk_seg"], plan["blk_first"], plan["seg_e"], plan["cnt"])
    row_map = lambda b, v, s, f, e, c: (jnp.minimum(b, c[0] - 1), 0)
    hbm = pl.BlockSpec(memory_space=pl.ANY)
    dma2 = pltpu.SemaphoreType.DMA((2,))
    up_spec = pltpu.PrefetchScalarGridSpec(
        num_scalar_prefetch=len(prefetch),
        grid=(n_blk,),
        in_specs=[pl.BlockSpec((blk, d), row_map), hbm, hbm],
        out_specs=pl.BlockSpec((blk, ff), row_map),
        scratch_shapes=[pltpu.VMEM((2, d, ff), F32), pltpu.VMEM((2, d, ff), F32),
                        pltpu.VMEM((d, ff), BF16), pltpu.VMEM((d, ff), BF16), dma2, dma2],
    )
    hidden = pl.pallas_call(
        functools.partial(_ffn_up_kernel, layer=layer),
        grid_spec=up_spec,
        out_shape=jax.ShapeDtypeStruct((n_rows, ff), BF16),
        compiler_params=_params(1, 2 * blk * d * 4 + 4 * d * ff * 4 + 2 * d * ff * 2 + 2 * blk * ff * 2
                                + 4 * blk * ff * 4 + blk * d * 2),
        name="ffn_up",
    )(*prefetch, xs_sorted, w_gate, w_up)
    down_spec = pltpu.PrefetchScalarGridSpec(
        num_scalar_prefetch=len(prefetch),
        grid=(n_blk,),
        in_specs=[pl.BlockSpec((blk, ff), row_map), hbm],
        out_specs=pl.BlockSpec((blk, d), row_map),
        scratch_shapes=[pltpu.VMEM((2, ff, d), F32), pltpu.VMEM((ff, d), BF16), dma2],
    )
    return pl.pallas_call(
        functools.partial(_ffn_down_kernel, layer=layer),
        grid_spec=down_spec,
        out_shape=jax.ShapeDtypeStruct((n_rows, d), F32),
        compiler_params=_params(1, 2 * blk * ff * 2 + 2 * d * ff * 4 + d * ff * 2 + 3 * blk * d * 4),
        name="ffn_down",
    )(*prefetch, hidden, w_down)


def _combine_kernel(src_ref, y_hbm, xs_ref, ws_ref, gate_ref, fg_ref, o_ref, ybuf, sem, *, tm, n_ctx, final):
    i = pl.program_id(0)

    def issue(r, carry):
        for k in range(2):
            _row_copy(y_hbm, src_ref[0, 0, k * tm + r], ybuf.at[k], r, sem).start(priority=k)
        return carry

    lax.fori_loop(0, tm, issue, 0, unroll=DMA_UNROLL)
    for k in range(2):
        pltpu.make_async_copy(y_hbm.at[pl.ds(0, tm)], ybuf.at[k], sem).wait()

    wcol = jnp.transpose(ws_ref[...])
    y = wcol[:, 0:1] * ybuf[0] + wcol[:, 1:2] * ybuf[1]
    x = xs_ref[...] + _row_select(i * tm, tm, n_ctx, gate_ref) * y
    if final:
        ms = jnp.mean(x * x, axis=-1, keepdims=True)
        x = x * lax.rsqrt(ms + RMS_EPS) * fg_ref[...]
    o_ref[...] = x


def _combine(y_sorted, dest_blocks, xs, ws, mod_l, final_g, n_ctx, tm, final):
    t, d = xs.shape
    if final:
        skip = n_ctx // tm
        out_rows = t - n_ctx
        out_map = lambda i: (jnp.maximum(i - skip, 0), 0)
    else:
        out_rows = t
        out_map = lambda i: (i, 0)
    return pl.pallas_call(
        functools.partial(_combine_kernel, tm=tm, n_ctx=n_ctx, final=final),
        grid=(t // tm,),
        in_specs=[pl.BlockSpec((1, 1, 2 * tm), lambda i: (i, 0, 0), memory_space=pltpu.SMEM),
                  pl.BlockSpec(memory_space=pl.ANY),
                  pl.BlockSpec((tm, d), lambda i: (i, 0)),
                  pl.BlockSpec((8, tm), lambda i: (0, i)),
                  pl.BlockSpec((8, d), lambda i: (0, 5)),
                  pl.BlockSpec((1, d), lambda i: (0, 0))],
        out_specs=pl.BlockSpec((tm, d), out_map),
        out_shape=jax.ShapeDtypeStruct((out_rows, d), F32),
        scratch_shapes=[pltpu.VMEM((2, tm, d), F32), pltpu.SemaphoreType.DMA(())],
        compiler_params=_params(1, 2 * tm * d * 4 + 4 * tm * d * 4 + 4 * tm * d * 4),
        name="combine",
    )(dest_blocks, y_sorted, xs, ws, mod_l, final_g)


def _rope_tables(n_lat, n_ctx):
    rows = n_lat // GRID_W
    row_ids = jnp.repeat(jnp.arange(rows, dtype=F32), GRID_W)
    col_ids = jnp.tile(jnp.arange(GRID_W, dtype=F32), rows)
    n_freq = HEAD_DIM // 4
    inv = ROPE_THETA ** (-jnp.arange(n_freq, dtype=F32) / n_freq)
    ar = row_ids[:, None] * inv
    ac = col_ids[:, None] * inv
    ang = jnp.concatenate([ar, ar, ac, ac], axis=-1)
    sign = jnp.where((jnp.arange(HEAD_DIM) % (HEAD_DIM // 2)) < HEAD_DIM // 4, -1.0, 1.0).astype(F32)
    cos = jnp.concatenate([jnp.ones((n_ctx, HEAD_DIM), F32), jnp.cos(ang)], axis=0)
    sin = jnp.concatenate([jnp.zeros((n_ctx, HEAD_DIM), F32), jnp.sin(ang) * sign], axis=0)
    return cos, sin


def _moe_plan(counts, n_blk):
    blk = MOE_BLOCK
    n_exp = counts.shape[0]
    padded = (counts + blk - 1) // blk * blk
    pad_end = jnp.cumsum(padded)
    pad_start = pad_end - padded
    n_used = jnp.maximum(pad_end[-1] // blk, 1)
    b = jnp.minimum(jnp.arange(n_blk, dtype=jnp.int32), n_used - 1)
    blk_e = jnp.sum((pad_end[None, :] <= (b * blk)[:, None]).astype(jnp.int32), axis=1)
    blk_e = jnp.minimum(blk_e, n_exp - 1)
    blk_valid = jnp.clip(counts[blk_e] - (b * blk - pad_start[blk_e]), 0, blk).astype(jnp.int32)
    live = counts > 0
    seg_of_e = jnp.cumsum(live.astype(jnp.int32)) - 1
    n_seg = jnp.maximum(seg_of_e[-1] + 1, 1)
    ks = jnp.arange(n_exp, dtype=jnp.int32)
    seg_e = jnp.sum(jnp.where(live[None, :] & (seg_of_e[None, :] == ks[:, None]), ks[None, :], 0), axis=1)
    blk_first = jnp.concatenate([jnp.ones((1,), jnp.int32), (blk_e[1:] != blk_e[:-1]).astype(jnp.int32)])
    return {
        "pad_start": pad_start.astype(jnp.int32),
        "blk_valid": blk_valid,
        "blk_seg": seg_of_e[blk_e].astype(jnp.int32),
        "blk_first": blk_first,
        "seg_e": seg_e.astype(jnp.int32),
        "cnt": jnp.stack([n_used, n_seg]).astype(jnp.int32),
    }


def kernel(x, c, ctx, c_ctx, w_ada, b_ada, norm1_g, norm2_g, w_in, pool_w, pool_scale, q_norm_g, k_norm_g,
           sink, w_branch, w_out, router_w, router_b, w_gate, w_up, w_down, final_g):
    assert x.shape[0] == 1, "single-sequence kernel"
    n_lat, d = x.shape[1], x.shape[2]
    n_ctx = ctx.shape[1]
    t = n_ctx + n_lat
    mix = d // 2
    depth = w_in.shape[0]
    n_exp = router_w.shape[1]
    tok = _pick(n_ctx, (256, 128))
    assert (2 * t) % MOE_BLOCK == 0
    n_blk = 2 * t // MOE_BLOCK + n_exp
    cos, sin_signed = _rope_tables(n_lat, n_ctx)
    mod = _ada(c, c_ctx, w_ada, b_ada)
    xs = jnp.concatenate([ctx[0], x[0]], axis=0)
    for l in range(depth):
        last = l == depth - 1
        p = _inproj(xs, mod[l], norm1_g[l][None], w_in, l, cos, sin_signed,
                    q_norm_g[l][None], k_norm_g[l][None], n_ctx)
        og = _gattn(p, n_ctx, mix)
        ow = _wattn(p, sink[l], n_ctx, mix)
        z = _merge(p, og, ow, pool_w, pool_scale[l][None], w_branch, l, n_ctx, d)
        xs = _outproj(z, w_out, l, xs, mod[l], n_ctx)
        h2, ints, ws, cnt = _router(xs, mod[l], norm2_g[l][None], router_w, router_b, n_ctx)
        plan = _moe_plan(cnt[:, 0].astype(jnp.int32), n_blk)
        dest_blocks = _dest_blocks(ints, plan["pad_start"], tok)
        xs_sorted = _dispatch(h2, dest_blocks, n_blk * MOE_BLOCK, tok)
        y_sorted = _ffn(xs_sorted, plan, w_gate, w_up, w_down, l)
        xs = _combine(y_sorted, dest_blocks, xs, ws, mod[l], final_g[None], n_ctx, tok, last)
    return xs[None]
```

```python
import functools

import jax
import jax.numpy as jnp
from jax import lax
from jax.experimental import pallas as pl
from jax.experimental.pallas import tpu as pltpu

HEAD_DIM = 128
LANES = 128
BF16_ROWS = 16
KV_HEADS = 2
GRID_W = 64
WINDOW = 128
POOL_WINDOWS = (2, 4, 8, 16)
POOL_HALO = 16
N_GROUPS = 4
ROPE_THETA = 10000.0
RMS_EPS = 1e-6
NEG_INF = -1e30
ATTN_SCALE = HEAD_DIM ** -0.5
LOG2E = 1.4426950408889634
SOFTMAX_SAFE_LOG2 = 50.0
MOE_BLOCK = 256
DMA_UNROLL = 8
VMEM_CAP = 60 * 1024 * 1024

BF16 = jnp.bfloat16
F32 = jnp.float32


def _pick(n, prefs):
    for p in prefs:
        if n % p == 0:
            return p
    raise ValueError(f"no tile in {prefs} divides {n}")


def _largest_tile(n, cap, mult):
    return max(k for k in range(mult, min(n, cap) + 1, mult) if n % k == 0)


def _params(n_axes, vmem_bytes):
    limit = int(min(max(vmem_bytes * 5 // 4 + (4 << 20), 16 << 20), VMEM_CAP))
    return pltpu.CompilerParams(dimension_semantics=("arbitrary",) * n_axes, vmem_limit_bytes=limit)


def _sigmoid(x):
    return 1.0 / (1.0 + jnp.exp(-x))


def _sigmoid_tanh(x):
    return 0.5 * jnp.tanh(0.5 * x) + 0.5


def _dot(a, b):
    return jnp.dot(a, b, preferred_element_type=F32)


def _dot_nt(a, b):
    return lax.dot_general(a, b, (((1,), (1,)), ((), ())), preferred_element_type=F32)


def _row_select(row0, n_rows, n_ctx, mod_ref):
    rows = row0 + lax.broadcasted_iota(jnp.int32, (n_rows, 1), 0)
    return jnp.where(rows < n_ctx, mod_ref[1:2, :], mod_ref[0:1, :])


def _norm_modulate(x, g_ref, shift, scale):
    ms = jnp.mean(x * x, axis=-1, keepdims=True)
    y = x * lax.rsqrt(ms + RMS_EPS) * g_ref[...]
    return y * (1.0 + scale) + shift


def _ada_kernel(cb_ref, w_ref, b_ref, o_ref, *, tn):
    s = cb_ref[...]
    s = s * _sigmoid(s)
    outs = []
    for r in range(2):
        cols = [jnp.sum(w_ref[:, c * LANES:(c + 1) * LANES] * s[r], axis=0, keepdims=True)
                for c in range(tn // LANES)]
        outs.append(jnp.concatenate(cols, axis=1) + b_ref[...])
    o_ref[...] = jnp.concatenate(outs + [jnp.zeros((6, tn), F32)], axis=0)


def _ada(c, c_ctx, w_ada, b_ada):
    depth, d, w6 = w_ada.shape
    tn = _pick(w6, (1024, 512, 256, 128))
    cb = jnp.broadcast_to(jnp.stack([c[0], c_ctx])[:, :, None], (2, d, LANES))
    vmem = 2 * d * tn * 4 + 2 * d * LANES * 4 * 2 + d * LANES * 4 * 4
    return pl.pallas_call(
        functools.partial(_ada_kernel, tn=tn),
        grid=(depth, w6 // tn),
        in_specs=[pl.BlockSpec((2, d, LANES), lambda l, j: (0, 0, 0)),
                  pl.BlockSpec((None, d, tn), lambda l, j: (l, 0, j)),
                  pl.BlockSpec((None, 1, tn), lambda l, j: (l, 0, j))],
        out_specs=pl.BlockSpec((None, 8, tn), lambda l, j: (l, 0, j)),
        out_shape=jax.ShapeDtypeStruct((depth, 8, w6), F32),
        compiler_params=_params(2, vmem),
        name="ada",
    )(cb, w_ada, b_ada.reshape(depth, 1, w6))


def _rope(x, cos, sin_signed, first_half):
    rot = jnp.where(first_half, pltpu.roll(x, 3 * HEAD_DIM // 4, axis=1), pltpu.roll(x, HEAD_DIM // 4, axis=1))
    return x * cos + rot * sin_signed


def _head_rms(x, g_ref):
    ms = jnp.mean(x * x, axis=-1, keepdims=True)
    return x * lax.rsqrt(ms + RMS_EPS) * g_ref[...]


def _inproj_kernel(x_ref, shift_ref, scale_ref, g_ref, w_ref, cos_ref, sin_ref, qg_ref, kg_ref,
                   o_ref, h_sc, acc_sc, *, tm, tn, n_ctx, n_tiles, signatures):
    i = pl.program_id(0)
    j = pl.program_id(1)

    def normed():
        x = x_ref[...]
        return x * lax.rsqrt(jnp.mean(x * x, axis=-1, keepdims=True) + RMS_EPS)

    has_ctx_rows = i * tm < n_ctx

    @pl.when((j == 0) & has_ctx_rows)
    def _():
        gain = g_ref[...] * (1.0 + scale_ref[0:2, :])
        rows = i * tm + lax.broadcasted_iota(jnp.int32, (tm, 1), 0)
        is_ctx = rows < n_ctx
        h = normed() * jnp.where(is_ctx, gain[1:2], gain[0:1]) + jnp.where(is_ctx, shift_ref[1:2, :], shift_ref[0:1, :])
        h_sc[...] = h.astype(BF16)

    @pl.when((j == 0) & jnp.logical_not(has_ctx_rows))
    def _():
        h = normed() * (g_ref[...] * (1.0 + scale_ref[0:1, :])) + shift_ref[0:1, :]
        h_sc[...] = h.astype(BF16)

    def matmul():
        return _dot(h_sc[...], w_ref[...].astype(BF16))

    lane = lax.broadcasted_iota(jnp.int32, (1, HEAD_DIM), 1)
    first_half = (lane % (HEAD_DIM // 2)) < (HEAD_DIM // 4)

    def epilogue(sig, acc):
        if all(t == "plain" for t in sig):
            o_ref[...] = acc.astype(BF16)
            return
        for bi, typ in enumerate(sig):
            a = acc[:, bi * LANES:(bi + 1) * LANES]
            if typ != "plain":
                if typ in ("qnr", "knr"):
                    a = _head_rms(a, qg_ref if typ == "qnr" else kg_ref)
                a = _rope(a, cos_ref[...], sin_ref[...], first_half)
                if typ in ("qnr", "qr"):
                    a = a * (ATTN_SCALE * LOG2E)
            o_ref[:, bi * LANES:(bi + 1) * LANES] = a.astype(BF16)

    cur = j % 2
    prev = 1 - cur

    @pl.when(j == 0)
    def _():
        acc_sc[0] = matmul()

    for sig, js in signatures:
        steps = [jj + 1 for jj in js if jj + 1 < n_tiles]
        if steps:
            @pl.when(functools.reduce(jnp.logical_or, [j == s for s in steps]))
            def _(sig=sig):
                new = matmul()
                epilogue(sig, acc_sc[prev])
                acc_sc[cur] = new

        if n_tiles - 1 in js:
            @pl.when(j == n_tiles)
            def _(sig=sig):
                epilogue(sig, acc_sc[prev])


def _col_types(mix, d):
    hq = mix // HEAD_DIM
    types = (["plain"] * hq + ["qnr"] * hq + ["knr"] * KV_HEADS + ["plain"] * KV_HEADS
             + ["qr"] * hq + ["kr"] * KV_HEADS + ["plain"] * KV_HEADS + ["plain"] * (3 * d // LANES))
    return types


def _inproj(xs, mod_l, g1, w_in, layer, cos, sin_signed, qg, kg, n_ctx):
    t, d = xs.shape
    in_w = w_in.shape[-1]
    mix = d // 2
    tm = _pick(t, (768, 512, 256, 128))
    tn = _pick(in_w, (1024, 512, 256))
    types = _col_types(mix, d)
    assert len(types) * LANES == in_w
    per = tn // LANES
    sigs = {}
    for jj in range(in_w // tn):
        sigs.setdefault(tuple(types[jj * per:(jj + 1) * per]), []).append(jj)
    signatures = tuple((s, tuple(js)) for s, js in sigs.items())
    n_tiles = in_w // tn
    vmem = (2 * tm * d * 4 + 2 * d * tn * 4 + tm * d * 2 + 2 * tm * tn * 2 + tm * tn * 4 * 4 + d * tn * 2
            + 4 * tm * LANES * 4)
    return pl.pallas_call(
        functools.partial(_inproj_kernel, tm=tm, tn=tn, n_ctx=n_ctx, n_tiles=n_tiles, signatures=signatures),
        grid=(t // tm, n_tiles + 1),
        in_specs=[pl.BlockSpec((tm, d), lambda i, j: (i, 0)),
                  pl.BlockSpec((8, d), lambda i, j: (0, 0)),
                  pl.BlockSpec((8, d), lambda i, j: (0, 1)),
                  pl.BlockSpec((1, d), lambda i, j: (0, 0)),
                  pl.BlockSpec((None, d, tn), lambda i, j: (layer, 0, jnp.minimum(j, n_tiles - 1))),
                  pl.BlockSpec((tm, HEAD_DIM), lambda i, j: (i, 0)),
                  pl.BlockSpec((tm, HEAD_DIM), lambda i, j: (i, 0)),
                  pl.BlockSpec((1, HEAD_DIM), lambda i, j: (0, 0)),
                  pl.BlockSpec((1, HEAD_DIM), lambda i, j: (0, 0))],
        out_specs=pl.BlockSpec((tm, tn), lambda i, j: (i, jnp.maximum(j - 1, 0))),
        out_shape=jax.ShapeDtypeStruct((t, in_w), BF16),
        scratch_shapes=[pltpu.VMEM((tm, d), BF16), pltpu.VMEM((2, tm, tn), F32)],
        compiler_params=_params(2, vmem),
        name="inproj",
    )(xs, mod_l, mod_l, g1, w_in, cos, sin_signed, qg, kg)


def _lane_repeat(x, n):
    return jnp.concatenate([x] * n, axis=1)


def _stack_heads(q_ref, g):
    return jnp.concatenate([q_ref[:, h * HEAD_DIM:(h + 1) * HEAD_DIM] for h in range(g)], axis=0)


def _unstack_store(o_ref, out, g, tq):
    for h in range(g):
        o_ref[:, h * HEAD_DIM:(h + 1) * HEAD_DIM] = out[h * tq:(h + 1) * tq, :].astype(o_ref.dtype)


def _gattn_kernel(q_ref, k_ref, v_ref, o_ref, vx_sc, kmax_sc, m_sc, acc_sc, *, tq, tk, g, n_ctx, n_lat):
    qi = pl.program_id(1)

    @pl.when(qi == 0)
    def _():
        vx_sc[:, 0:HEAD_DIM] = v_ref[...]
        vx_sc[:, HEAD_DIM:] = jnp.ones((vx_sc.shape[0], HEAD_DIM), BF16)
        kf = k_ref[...].astype(F32)
        k2 = jnp.max(jnp.sum(kf * kf, axis=-1, keepdims=True), axis=0, keepdims=True)
        kmax_sc[...] = jnp.broadcast_to(jnp.sqrt(k2), kmax_sc.shape)

    q = _stack_heads(q_ref, g)
    rows = q.shape[0]
    qf = q.astype(F32)
    bound = jnp.sqrt(jnp.sum(qf * qf, axis=-1, keepdims=True)) * kmax_sc[0:1, :]
    bounded = jnp.max(bound) <= SOFTMAX_SAFE_LOG2
    ctx_only = qi * tq < n_ctx
    ctx_chunk = [(0, n_ctx)]
    all_chunks = ctx_chunk + [(n_ctx + c * tk, tk) for c in range(n_lat // tk)]

    def finish():
        acc = acc_sc[...]
        _unstack_store(o_ref, acc[:, :HEAD_DIM] / acc[:, HEAD_DIM:], g, tq)

    def attend_bounded(chunks):
        for n, (lo, size) in enumerate(chunks):
            s = _dot_nt(q, k_ref[lo:lo + size, :])
            p = jnp.exp2(s - _lane_repeat(bound, size // LANES))
            pv = _dot(p.astype(BF16), vx_sc[lo:lo + size, :])
            acc_sc[...] = pv if n == 0 else acc_sc[...] + pv
        finish()

    def attend_online(chunks):
        for n, (lo, size) in enumerate(chunks):
            s = _dot_nt(q, k_ref[lo:lo + size, :])
            mx = jnp.max(s, axis=-1, keepdims=True)
            if n == 0:
                m_new = jnp.broadcast_to(mx, (rows, LANES))
            else:
                m_prev = m_sc[...]
                m_new = jnp.maximum(m_prev, mx)
            p = jnp.exp2(s - _lane_repeat(m_new, size // LANES))
            pv = _dot(p.astype(BF16), vx_sc[lo:lo + size, :])
            if n == 0:
                acc_sc[...] = pv
            else:
                acc_sc[...] = _lane_repeat(jnp.exp2(m_prev - m_new), 2) * acc_sc[...] + pv
            m_sc[...] = m_new
        finish()

    for use_bound, attend in ((True, attend_bounded), (False, attend_online)):
        path = bounded if use_bound else jnp.logical_not(bounded)
        pl.when(path & ctx_only)(functools.partial(attend, ctx_chunk))
        pl.when(path & jnp.logical_not(ctx_only))(functools.partial(attend, all_chunks))


def _gattn(p, n_ctx, mix):
    t = p.shape[0]
    hq = mix // HEAD_DIM
    g = hq // KV_HEADS
    n_lat = t - n_ctx
    tq = _pick(n_ctx, (256, 128))
    assert t % tq == 0
    tk = _pick(n_lat, (2048, 1024, 512, 256, 128))
    gw = g * HEAD_DIM
    off_q = mix // gw
    off_k = (mix + hq * HEAD_DIM) // HEAD_DIM
    off_v = off_k + KV_HEADS
    rows = g * tq
    vmem = (2 * tq * gw * 2 * 2 + 2 * 2 * t * HEAD_DIM * 2 + t * 2 * HEAD_DIM * 2 + rows * LANES * 4 * 4
            + rows * max(tk, n_ctx) * 4 * 3)
    return pl.pallas_call(
        functools.partial(_gattn_kernel, tq=tq, tk=tk, g=g, n_ctx=n_ctx, n_lat=n_lat),
        grid=(KV_HEADS, t // tq),
        in_specs=[pl.BlockSpec((tq, gw), lambda h, i: (i, off_q + h)),
                  pl.BlockSpec((t, HEAD_DIM), lambda h, i: (0, off_k + h)),
                  pl.BlockSpec((t, HEAD_DIM), lambda h, i: (0, off_v + h))],
        out_specs=pl.BlockSpec((tq, gw), lambda h, i: (i, h)),
        out_shape=jax.ShapeDtypeStruct((t, mix), BF16),
        scratch_shapes=[pltpu.VMEM((t, 2 * HEAD_DIM), BF16), pltpu.VMEM((8, LANES), F32),
                        pltpu.VMEM((rows, LANES), F32), pltpu.VMEM((rows, 2 * HEAD_DIM), F32)],
        compiler_params=_params(2, vmem),
        name="gattn",
    )(p, p, p)


def _wattn_kernel(*refs, tq, g, n_ctx, span, t):
    q_refs, k_refs, v_refs = refs[0:KV_HEADS], refs[KV_HEADS:2 * KV_HEADS], refs[2 * KV_HEADS:3 * KV_HEADS]
    sink_ref, o_ref, vx_sc = refs[3 * KV_HEADS:]
    qi = pl.program_id(0)

    @pl.when(qi == 0)
    def _():
        for h in range(KV_HEADS):
            vx_sc[h, :, 0:HEAD_DIM] = v_refs[h][...]
            vx_sc[h, :, HEAD_DIM:] = jnp.ones((t, HEAD_DIM), BF16)

    for h in range(KV_HEADS):
        _wattn_head(qi, q_refs[h], k_refs[h], vx_sc.at[h], sink_ref.at[h], o_ref.at[:, h * g * HEAD_DIM:(h + 1) * g * HEAD_DIM],
                    tq=tq, g=g, n_ctx=n_ctx, span=span, t=t)


def _wattn_head(qi, q_ref, k_ref, vx_sc, sink_ref, o_ref, *, tq, g, n_ctx, span, t):
    q = _stack_heads(q_ref, g)
    rows = q.shape[0]
    start = pl.multiple_of(jnp.clip(qi * tq - WINDOW, 0, t - span), LANES)

    far = t + 4 * WINDOW
    qrow = qi * tq + lax.broadcasted_iota(jnp.int32, (tq, 1), 0)
    qrow = jnp.concatenate([jnp.where(qrow >= n_ctx, qrow, -far)] * g, axis=0)
    krow = start + lax.broadcasted_iota(jnp.int32, (1, span), 1)
    krow = jnp.where(krow >= n_ctx, krow, far)
    keep = jnp.abs(krow - qrow) <= WINDOW

    s_c = _dot_nt(q, k_ref[0:n_ctx, :])
    s_w = jnp.where(keep, _dot_nt(q, k_ref[pl.ds(start, span), :]), NEG_INF)
    sink = sink_ref[...] * LOG2E
    mx = jnp.maximum(jnp.maximum(jnp.max(s_c, axis=-1, keepdims=True), jnp.max(s_w, axis=-1, keepdims=True)), sink)
    m = jnp.broadcast_to(mx, (rows, LANES))
    p_c = jnp.exp2(s_c - _lane_repeat(m, n_ctx // LANES))
    p_w = jnp.exp2(s_w - _lane_repeat(m, span // LANES))
    acc = _dot(p_c.astype(BF16), vx_sc[0:n_ctx, :]) + _dot(p_w.astype(BF16), vx_sc[pl.ds(start, span), :])
    denom = acc[:, HEAD_DIM:] + jnp.exp2(sink - m)
    _unstack_store(o_ref, acc[:, :HEAD_DIM] / denom, g, tq)


def _wattn(p, sink_l, n_ctx, mix):
    t = p.shape[0]
    hq = mix // HEAD_DIM
    g = hq // KV_HEADS
    tq = _pick(n_ctx, (256, 128))
    span = tq + 2 * WINDOW
    assert t % tq == 0 and t >= span
    gw = g * HEAD_DIM
    base = mix + hq * HEAD_DIM + 2 * KV_HEADS * HEAD_DIM
    off_q = base // gw
    off_k = (base + hq * HEAD_DIM) // HEAD_DIM
    off_v = off_k + KV_HEADS
    rows = g * tq
    sink_rows = jnp.repeat(sink_l.reshape(KV_HEADS, g), tq, axis=1).reshape(KV_HEADS, rows, 1)
    vmem = KV_HEADS * (2 * tq * gw * 2 * 2 + 2 * 2 * t * HEAD_DIM * 2 + 2 * rows * LANES * 4
                       + rows * (span + n_ctx) * 4 * 3 + t * 2 * HEAD_DIM * 2)
    heads = range(KV_HEADS)
    return pl.pallas_call(
        functools.partial(_wattn_kernel, tq=tq, g=g, n_ctx=n_ctx, span=span, t=t),
        grid=(t // tq,),
        in_specs=([pl.BlockSpec((tq, gw), lambda i, h=h: (i, off_q + h)) for h in heads]
                  + [pl.BlockSpec((t, HEAD_DIM), lambda i, h=h: (0, off_k + h)) for h in heads]
                  + [pl.BlockSpec((t, HEAD_DIM), lambda i, h=h: (0, off_v + h)) for h in heads]
                  + [pl.BlockSpec((KV_HEADS, rows, 1), lambda i: (0, 0, 0))]),
        out_specs=pl.BlockSpec((tq, mix), lambda i: (i, 0)),
        out_shape=jax.ShapeDtypeStruct((t, mix), BF16),
        scratch_shapes=[pltpu.VMEM((KV_HEADS, t, 2 * HEAD_DIM), BF16)],
        compiler_params=_params(1, vmem),
        name="wattn",
    )(*([p] * (3 * KV_HEADS)), sink_rows)


def _merge_kernel(u_ref, up_ref, un_ref, og_ref, ow_ref, g0_ref, g1_ref, g2_ref, pw_ref, ps_ref, wb_ref,
                  z_ref, ext_sc, pool_sc, *, tm, n_ctx, t, mix):
    i = pl.program_id(0)
    j = pl.program_id(1)
    gw = mix // len(POOL_WINDOWS)

    def pool(clipped):
        ext_sc[0:POOL_HALO, :] = up_ref[...].astype(F32)
        ext_sc[POOL_HALO:POOL_HALO + tm, :] = u_ref[...].astype(F32)
        ext_sc[POOL_HALO + tm:, :] = un_ref[...].astype(F32)
        r = i * tm + lax.broadcasted_iota(jnp.int32, (tm, 1), 0)
        r_ctx = r < n_ctx
        for gi, w in enumerate(POOL_WINDOWS):
            c0, c1 = gi * gw, (gi + 1) * gw
            tot = jnp.zeros((tm, gw), F32)
            cnt = jnp.zeros((tm, 1), F32)
            for off in range(-((w - 1) // 2), w // 2 + 1):
                part = ext_sc[POOL_HALO + off:POOL_HALO + off + tm, c0:c1]
                if clipped:
                    rr = r + off
                    ok = (rr >= 0) & (rr < t) & ((rr < n_ctx) == r_ctx)
                    part = jnp.where(ok, part, 0.0)
                    cnt = cnt + ok.astype(F32)
                tot = tot + part
            mean = tot / cnt if clipped else tot * (1.0 / w)
            dlt = mean - ext_sc[POOL_HALO:POOL_HALO + tm, c0:c1]
            y = _dot(dlt.astype(BF16), pw_ref[gi].astype(BF16)) * ps_ref[:, c0:c1]
            pool_sc[:, c0:c1] = y.astype(BF16)

    reach = max(POOL_WINDOWS) // 2
    interior = (i * tm - reach >= n_ctx) & ((i + 1) * tm + reach <= t)
    pl.when((j == 0) & interior)(functools.partial(pool, False))
    pl.when((j == 0) & jnp.logical_not(interior))(functools.partial(pool, True))

    def gate(g_ref):
        return _sigmoid_tanh(g_ref[...].astype(F32))

    z = gate(g0_ref) * _dot(pool_sc[...], wb_ref[0].astype(BF16))
    z = z + gate(g1_ref) * _dot(og_ref[...], wb_ref[1].astype(BF16))
    z = z + gate(g2_ref) * _dot(ow_ref[...], wb_ref[2].astype(BF16))
    z_ref[...] = z.astype(BF16)


def _merge(p, og, ow, pool_w, pool_scale, w_branch, layer, n_ctx, d):
    t = p.shape[0]
    mix = d // 2
    tm = _largest_tile(t, 1100, POOL_HALO)
    tn = _pick(d, (512, 256, 128))
    hb = tm // POOL_HALO
    n_hb = t // POOL_HALO
    gate0 = (p.shape[1] - 3 * d) // tn
    gs = pool_w.shape[-1]
    vmem = (2 * 3 * tm * mix * 2 + 2 * 3 * tm * tn * 2 + 2 * 3 * mix * tn * 4 + 3 * mix * tn * 2
            + (tm + 2 * POOL_HALO) * mix * 4 + tm * mix * 2 + 2 * tm * tn * 2 + 4 * tm * tn * 4
            + 2 * len(POOL_WINDOWS) * gs * gs * 4 + 6 * tm * gs * 4)
    return pl.pallas_call(
        functools.partial(_merge_kernel, tm=tm, n_ctx=n_ctx, t=t, mix=mix),
        grid=(t // tm, d // tn),
        in_specs=[pl.BlockSpec((tm, mix), lambda i, j: (i, 0)),
                  pl.BlockSpec((POOL_HALO, mix), lambda i, j: (jnp.maximum(i * hb - 1, 0), 0)),
                  pl.BlockSpec((POOL_HALO, mix), lambda i, j: (jnp.minimum((i + 1) * hb, n_hb - 1), 0)),
                  pl.BlockSpec((tm, mix), lambda i, j: (i, 0)),
                  pl.BlockSpec((tm, mix), lambda i, j: (i, 0)),
                  pl.BlockSpec((tm, tn), lambda i, j: (i, gate0 + j)),
                  pl.BlockSpec((tm, tn), lambda i, j: (i, gate0 + d // tn + j)),
                  pl.BlockSpec((tm, tn), lambda i, j: (i, gate0 + 2 * (d // tn) + j)),
                  pl.BlockSpec((None, len(POOL_WINDOWS), gs, gs), lambda i, j: (layer, 0, 0, 0)),
                  pl.BlockSpec((1, mix), lambda i, j: (0, 0)),
                  pl.BlockSpec((None, 3, mix, tn), lambda i, j: (layer, 0, 0, j))],
        out_specs=pl.BlockSpec((tm, tn), lambda i, j: (i, j)),
        out_shape=jax.ShapeDtypeStruct((t, d), BF16),
        scratch_shapes=[pltpu.VMEM((tm + 2 * POOL_HALO, mix), F32), pltpu.VMEM((tm, mix), BF16)],
        compiler_params=_params(2, vmem),
        name="merge",
    )(p, p, p, og, ow, p, p, p, pool_w, pool_scale, w_branch)


def _outproj_kernel(z_ref, w_ref, xs_ref, gate_ref, o_ref, *, tm, n_ctx):
    i = pl.program_id(0)
    gate = _row_select(i * tm, tm, n_ctx, gate_ref)
    o_ref[...] = xs_ref[...] + gate * _dot(z_ref[...], w_ref[...].astype(BF16))


def _outproj(z, w_out, layer, xs, mod_l, n_ctx):
    t, d = xs.shape
    tm = _largest_tile(t, 1536, BF16_ROWS)
    tn = _pick(d, (512, 256, 128))
    vmem = 2 * tm * d * 2 + 2 * d * tn * 4 + d * tn * 2 + 4 * tm * tn * 4 + 2 * tm * tn * 4
    return pl.pallas_call(
        functools.partial(_outproj_kernel, tm=tm, n_ctx=n_ctx),
        grid=(t // tm, d // tn),
        in_specs=[pl.BlockSpec((tm, d), lambda i, j: (i, 0)),
                  pl.BlockSpec((None, d, tn), lambda i, j: (layer, 0, j)),
                  pl.BlockSpec((tm, tn), lambda i, j: (i, j)),
                  pl.BlockSpec((8, tn), lambda i, j: (0, 2 * (d // tn) + j))],
        out_specs=pl.BlockSpec((tm, tn), lambda i, j: (i, j)),
        out_shape=jax.ShapeDtypeStruct((t, d), F32),
        compiler_params=_params(2, vmem),
        name="outproj",
    )(z, w_out, xs, mod_l)


def _router_kernel(xs_ref, shift_ref, scale_ref, g_ref, rwt_ref, rb_ref,
                   h_ref, ints_ref, ws_ref, cnt_ref, carry_sc, *, tm, n_ctx, n_exp):
    i = pl.program_id(0)

    @pl.when(i == 0)
    def _():
        carry_sc[...] = jnp.zeros(carry_sc.shape, F32)

    shift = _row_select(i * tm, tm, n_ctx, shift_ref)
    scale = _row_select(i * tm, tm, n_ctx, scale_ref)
    h = _norm_modulate(xs_ref[...], g_ref, shift, scale)
    h_ref[...] = h

    scores = _sigmoid(_dot_nt(rwt_ref[...].astype(BF16), h.astype(BF16)))
    sel = scores + rb_ref[...]
    per = n_exp // N_GROUPS
    sub = lax.broadcasted_iota(jnp.int32, (per, tm), 0)

    def top2(v):
        m1 = jnp.max(v, axis=0, keepdims=True)
        i1 = jnp.min(jnp.where(v == m1, sub, per), axis=0, keepdims=True)
        rest = jnp.where(sub == i1, -jnp.inf, v)
        m2 = jnp.max(rest, axis=0, keepdims=True)
        i2 = jnp.min(jnp.where(rest == m2, sub, per), axis=0, keepdims=True)
        return m1 + m2, i1, i2

    tops = [top2(sel[gi * per:(gi + 1) * per, :]) for gi in range(N_GROUPS)]
    best, l1, l2 = tops[0]
    grp = jnp.zeros((1, tm), jnp.int32)
    for gi in range(1, N_GROUPS):
        gs, a1, a2 = tops[gi]
        better = gs > best
        best = jnp.where(better, gs, best)
        grp = jnp.where(better, gi, grp)
        l1 = jnp.where(better, a1, l1)
        l2 = jnp.where(better, a2, l2)

    hot1 = [(grp == gi) & (sub == l1) for gi in range(N_GROUPS)]
    hot2 = [(grp == gi) & (sub == l2) for gi in range(N_GROUPS)]
    assign = jnp.concatenate([(a | b).astype(F32) for a, b in zip(hot1, hot2)], axis=0)

    before = (lax.broadcasted_iota(jnp.int32, (tm, tm), 0) < lax.broadcasted_iota(jnp.int32, (tm, tm), 1))
    pos = _dot(assign.astype(BF16), before.astype(F32).astype(BF16)) + carry_sc[...]

    def pick(hots, val):
        return sum(jnp.sum(jnp.where(hots[gi], val[gi * per:(gi + 1) * per, :], 0.0), axis=0, keepdims=True)
                   for gi in range(N_GROUPS))

    s1 = pick(hot1, scores)
    s2 = pick(hot2, scores)
    r1 = pick(hot1, pos)
    r2 = pick(hot2, pos)
    tot = s1 + s2
    ints_ref[0:1, :] = grp * per + l1
    ints_ref[1:2, :] = grp * per + l2
    ints_ref[2:3, :] = r1.astype(jnp.int32)
    ints_ref[3:4, :] = r2.astype(jnp.int32)
    ints_ref[4:8, :] = jnp.zeros((4, tm), jnp.int32)
    ws_ref[0:1, :] = s1 / tot
    ws_ref[1:2, :] = s2 / tot
    ws_ref[2:8, :] = jnp.zeros((6, tm), F32)
    carry_sc[...] = carry_sc[...] + jnp.sum(assign, axis=1, keepdims=True)
    cnt_ref[...] = jnp.broadcast_to(carry_sc[...], cnt_ref.shape)


def _router(xs, mod_l, g2, router_w, router_b, n_ctx):
    t, d = xs.shape
    n_exp = router_w.shape[1]
    tm = _pick(t, (768, 512, 256, 128))
    vmem = 2 * tm * d * 4 * 2 + 4 * tm * d * 4 + 3 * tm * tm * 4 + 2 * n_exp * d * 4
    return pl.pallas_call(
        functools.partial(_router_kernel, tm=tm, n_ctx=n_ctx, n_exp=n_exp),
        grid=(t // tm,),
        in_specs=[pl.BlockSpec((tm, d), lambda i: (i, 0)),
                  pl.BlockSpec((8, d), lambda i: (0, 3)),
                  pl.BlockSpec((8, d), lambda i: (0, 4)),
                  pl.BlockSpec((1, d), lambda i: (0, 0)),
                  pl.BlockSpec((n_exp, d), lambda i: (0, 0)),
                  pl.BlockSpec((n_exp, 1), lambda i: (0, 0))],
        out_specs=[pl.BlockSpec((tm, d), lambda i: (i, 0)),
                   pl.BlockSpec((8, tm), lambda i: (0, i)),
                   pl.BlockSpec((8, tm), lambda i: (0, i)),
                   pl.BlockSpec((n_exp, LANES), lambda i: (0, 0))],
        out_shape=[jax.ShapeDtypeStruct((t, d), F32),
                   jax.ShapeDtypeStruct((8, t), jnp.int32),
                   jax.ShapeDtypeStruct((8, t), F32),
                   jax.ShapeDtypeStruct((n_exp, LANES), F32)],
        scratch_shapes=[pltpu.VMEM((n_exp, 1), F32)],
        compiler_params=_params(1, vmem),
        name="router",
    )(xs, mod_l, mod_l, g2, router_w.T, router_b.reshape(n_exp, 1))


def _row_copy(src_ref, src_row, dst_ref, dst_row, sem):
    return pltpu.make_async_copy(src_ref.at[pl.ds(src_row, 1)], dst_ref.at[pl.ds(dst_row, 1)], sem)


def _dispatch_kernel(dst_ref, h_ref, xs_hbm, sem, *, tm):
    def issue(r, carry):
        for k in range(2):
            _row_copy(h_ref, r, xs_hbm, dst_ref[0, 0, k * tm + r], sem).start(priority=k)
        return carry

    lax.fori_loop(0, tm, issue, 0, unroll=DMA_UNROLL)
    for k in range(2):
        pltpu.make_async_copy(h_ref, xs_hbm.at[pl.ds(0, tm)], sem).wait()


def _dest_blocks(ints, pad_start, tm):
    t = ints.shape[1]
    experts = jnp.arange(pad_start.shape[0], dtype=jnp.int32)
    start = jnp.sum(jnp.where(ints[0:2, :, None] == experts, pad_start, 0), axis=-1)
    dest = start + ints[2:4]
    return dest.reshape(2, t // tm, tm).transpose(1, 0, 2).reshape(t // tm, 1, 2 * tm)


def _dispatch(h2, dest_blocks, n_rows, tm):
    t, d = h2.shape
    return pl.pallas_call(
        functools.partial(_dispatch_kernel, tm=tm),
        grid=(t // tm,),
        in_specs=[pl.BlockSpec((1, 1, 2 * tm), lambda i: (i, 0, 0), memory_space=pltpu.SMEM),
                  pl.BlockSpec((tm, d), lambda i: (i, 0))],
        out_specs=pl.BlockSpec(memory_space=pl.ANY),
        out_shape=jax.ShapeDtypeStruct((n_rows, d), F32),
        scratch_shapes=[pltpu.SemaphoreType.DMA(())],
        compiler_params=_params(1, 2 * tm * d * 4),
        name="dispatch",
    )(dest_blocks, h2)


def _expert_weights(b, plan_refs, w_hbm, stage, cast, sems, layer):
    blk_seg_ref, blk_first_ref, seg_e_ref, cnt_ref = plan_refs
    n_seg = cnt_ref[1]

    def copies(k):
        e = seg_e_ref[k]
        return [pltpu.make_async_copy(w.at[layer, e], st, sm) for w, st, sm in zip(w_hbm, stage, sems)]

    @pl.when(b == 0)
    def _():
        for cp in copies(0):
            cp.start()

    @pl.when(blk_first_ref[b] == 1)
    def _():
        k = blk_seg_ref[b]
        for cp in copies(k):
            cp.wait()
        for st, dst in zip(stage, cast):
            dst[...] = st[...].astype(BF16)

        @pl.when(k + 1 < n_seg)
        def _():
            for cp in copies(k + 1):
                cp.start()


def _ffn_kernel(blk_valid_ref, blk_seg_ref, blk_first_ref, seg_e_ref, cnt_ref, x_ref, wg_hbm, wu_hbm, wd_hbm, y_ref,
                wg_st, wu_st, wd_st, wg_sc, wu_sc, wd_sc, sem_g, sem_u, sem_d, *, layer):
    b = pl.program_id(0)

    @pl.when(b < cnt_ref[0])
    def _():
        _expert_weights(b, (blk_seg_ref, blk_first_ref, seg_e_ref, cnt_ref), (wg_hbm, wu_hbm, wd_hbm),
                        (wg_st, wu_st, wd_st), (wg_sc, wu_sc, wd_sc), (sem_g, sem_u, sem_d), layer)
        rows = lax.broadcasted_iota(jnp.int32, (x_ref.shape[0], 1), 0)
        x = jnp.where(rows < blk_valid_ref[b], x_ref[...], 0.0).astype(BF16)
        gte = _dot(x, wg_sc[...])
        up = _dot(x, wu_sc[...])
        hidden = (gte * _sigmoid(gte) * up).astype(BF16)
        y_ref[...] = _dot(hidden, wd_sc[...])


def _ffn(xs_sorted, plan, w_gate, w_up, w_down, layer):
    n_rows, d = xs_sorted.shape
    ff = w_gate.shape[-1]
    blk = MOE_BLOCK
    n_blk = n_rows // blk
    prefetch = (plan["blk_valid"], plan["blk_seg"], plan["blk_first"], plan["seg_e"], plan["cnt"])
    row_map = lambda b, v, s, f, e, c: (jnp.minimum(b, c[0] - 1), 0)
    hbm = pl.BlockSpec(memory_space=pl.ANY)
    dma = pltpu.SemaphoreType.DMA(())
    grid_spec = pltpu.PrefetchScalarGridSpec(
        num_scalar_prefetch=len(prefetch),
        grid=(n_blk,),
        in_specs=[pl.BlockSpec((blk, d), row_map), hbm, hbm, hbm],
        out_specs=pl.BlockSpec((blk, d), row_map),
        scratch_shapes=[pltpu.VMEM((d, ff), F32), pltpu.VMEM((d, ff), F32), pltpu.VMEM((ff, d), F32),
                        pltpu.VMEM((d, ff), BF16), pltpu.VMEM((d, ff), BF16), pltpu.VMEM((ff, d), BF16),
                        dma, dma, dma],
    )
    vmem = 3 * d * ff * (4 + 2) + 4 * blk * d * 4 + blk * d * 2 + 3 * blk * ff * 4 + blk * d * 4
    return pl.pallas_call(
        functools.partial(_ffn_kernel, layer=layer),
        grid_spec=grid_spec,
        out_shape=jax.ShapeDtypeStruct((n_rows, d), F32),
        compiler_params=pltpu.CompilerParams(dimension_semantics=("arbitrary",), vmem_limit_bytes=min(vmem + (6 << 20), VMEM_CAP)),
        name="ffn",
    )(*prefetch, xs_sorted, w_gate, w_up, w_down)


def _combine_kernel(src_ref, nxt_ref, y_hbm, xs_ref, ws_ref, gate_ref, fg_ref, o_ref, ybuf, sem, *, tm, n_ctx, final):
    i = pl.program_id(0)
    slot = i % 2

    def gather(idx_ref, into):
        def issue(r, carry):
            for k in range(2):
                _row_copy(y_hbm, idx_ref[0, 0, k * tm + r], ybuf.at[into, k], r, sem.at[into]).start(priority=k)
            return carry

        lax.fori_loop(0, tm, issue, 0, unroll=DMA_UNROLL)

    @pl.when(i == 0)
    def _():
        gather(src_ref, 0)

    @pl.when(i + 1 < pl.num_programs(0))
    def _():
        gather(nxt_ref, 1 - slot)

    for k in range(2):
        pltpu.make_async_copy(y_hbm.at[pl.ds(0, tm)], ybuf.at[slot, k], sem.at[slot]).wait()

    wcol = jnp.transpose(ws_ref[...])
    y = wcol[:, 0:1] * ybuf[slot, 0] + wcol[:, 1:2] * ybuf[slot, 1]
    x = xs_ref[...] + _row_select(i * tm, tm, n_ctx, gate_ref) * y
    if final:
        ms = jnp.mean(x * x, axis=-1, keepdims=True)
        x = x * lax.rsqrt(ms + RMS_EPS) * fg_ref[...]
    o_ref[...] = x


def _combine(y_sorted, dest_blocks, xs, ws, mod_l, final_g, n_ctx, tm, final):
    t, d = xs.shape
    if final:
        skip = n_ctx // tm
        out_rows = t - n_ctx
        out_map = lambda i: (jnp.maximum(i - skip, 0), 0)
    else:
        out_rows = t
        out_map = lambda i: (i, 0)
    return pl.pallas_call(
        functools.partial(_combine_kernel, tm=tm, n_ctx=n_ctx, final=final),
        grid=(t // tm,),
        in_specs=[pl.BlockSpec((1, 1, 2 * tm), lambda i: (i, 0, 0), memory_space=pltpu.SMEM),
                  pl.BlockSpec((1, 1, 2 * tm), lambda i: (jnp.minimum(i + 1, t // tm - 1), 0, 0), memory_space=pltpu.SMEM),
                  pl.BlockSpec(memory_space=pl.ANY),
                  pl.BlockSpec((tm, d), lambda i: (i, 0)),
                  pl.BlockSpec((8, tm), lambda i: (0, i)),
                  pl.BlockSpec((8, d), lambda i: (0, 5)),
                  pl.BlockSpec((1, d), lambda i: (0, 0))],
        out_specs=pl.BlockSpec((tm, d), out_map),
        out_shape=jax.ShapeDtypeStruct((out_rows, d), F32),
        scratch_shapes=[pltpu.VMEM((2, 2, tm, d), F32), pltpu.SemaphoreType.DMA((2,))],
        compiler_params=_params(1, 4 * tm * d * 4 + 4 * tm * d * 4 + 4 * tm * d * 4),
        name="combine",
    )(dest_blocks, dest_blocks, y_sorted, xs, ws, mod_l, final_g)


def _rope_tables(n_lat, n_ctx):
    rows = n_lat // GRID_W
    row_ids = jnp.repeat(jnp.arange(rows, dtype=F32), GRID_W)
    col_ids = jnp.tile(jnp.arange(GRID_W, dtype=F32), rows)
    n_freq = HEAD_DIM // 4
    inv = ROPE_THETA ** (-jnp.arange(n_freq, dtype=F32) / n_freq)
    ar = row_ids[:, None] * inv
    ac = col_ids[:, None] * inv
    ang = jnp.concatenate([ar, ar, ac, ac], axis=-1)
    sign = jnp.where((jnp.arange(HEAD_DIM) % (HEAD_DIM // 2)) < HEAD_DIM // 4, -1.0, 1.0).astype(F32)
    cos = jnp.concatenate([jnp.ones((n_ctx, HEAD_DIM), F32), jnp.cos(ang)], axis=0)
    sin = jnp.concatenate([jnp.zeros((n_ctx, HEAD_DIM), F32), jnp.sin(ang) * sign], axis=0)
    return cos, sin


def _moe_plan(counts, n_blk):
    blk = MOE_BLOCK
    n_exp = counts.shape[0]
    padded = (counts + blk - 1) // blk * blk
    pad_end = jnp.cumsum(padded)
    pad_start = pad_end - padded
    n_used = jnp.maximum(pad_end[-1] // blk, 1)
    b = jnp.minimum(jnp.arange(n_blk, dtype=jnp.int32), n_used - 1)
    blk_e = jnp.sum((pad_end[None, :] <= (b * blk)[:, None]).astype(jnp.int32), axis=1)
    blk_e = jnp.minimum(blk_e, n_exp - 1)
    blk_valid = jnp.clip(counts[blk_e] - (b * blk - pad_start[blk_e]), 0, blk).astype(jnp.int32)
    live = counts > 0
    seg_of_e = jnp.cumsum(live.astype(jnp.int32)) - 1
    n_seg = jnp.maximum(seg_of_e[-1] + 1, 1)
    ks = jnp.arange(n_exp, dtype=jnp.int32)
    seg_e = jnp.sum(jnp.where(live[None, :] & (seg_of_e[None, :] == ks[:, None]), ks[None, :], 0), axis=1)
    blk_first = jnp.concatenate([jnp.ones((1,), jnp.int32), (blk_e[1:] != blk_e[:-1]).astype(jnp.int32)])
    return {
        "pad_start": pad_start.astype(jnp.int32),
        "blk_valid": blk_valid,
        "blk_seg": seg_of_e[blk_e].astype(jnp.int32),
        "blk_first": blk_first,
        "seg_e": seg_e.astype(jnp.int32),
        "cnt": jnp.stack([n_used, n_seg]).astype(jnp.int32),
    }


def kernel(x, c, ctx, c_ctx, w_ada, b_ada, norm1_g, norm2_g, w_in, pool_w, pool_scale, q_norm_g, k_norm_g,
           sink, w_branch, w_out, router_w, router_b, w_gate, w_up, w_down, final_g):
    assert x.shape[0] == 1, "single-sequence kernel"
    n_lat, d = x.shape[1], x.shape[2]
    n_ctx = ctx.shape[1]
    t = n_ctx + n_lat
    mix = d // 2
    depth = w_in.shape[0]
    n_exp = router_w.shape[1]
    tok = _pick(n_ctx, (256, 128))
    assert (2 * t) % MOE_BLOCK == 0
    n_blk = 2 * t // MOE_BLOCK + n_exp
    cos, sin_signed = _rope_tables(n_lat, n_ctx)
    mod = _ada(c, c_ctx, w_ada, b_ada)
    xs = jnp.concatenate([ctx[0], x[0]], axis=0)
    for l in range(depth):
        last = l == depth - 1
        p = _inproj(xs, mod[l], norm1_g[l][None], w_in, l, cos, sin_signed,
                    q_norm_g[l][None], k_norm_g[l][None], n_ctx)
        og = _gattn(p, n_ctx, mix)
        ow = _wattn(p, sink[l], n_ctx, mix)
        z = _merge(p, og, ow, pool_w, pool_scale[l][None], w_branch, l, n_ctx, d)
        xs = _outproj(z, w_out, l, xs, mod[l], n_ctx)
        h2, ints, ws, cnt = _router(xs, mod[l], norm2_g[l][None], router_w, router_b, n_ctx)
        plan = _moe_plan(cnt[:, 0].astype(jnp.int32), n_blk)
        dest_blocks = _dest_blocks(ints, plan["pad_start"], tok)
        xs_sorted = _dispatch(h2, dest_blocks, n_blk * MOE_BLOCK, tok)
        y_sorted = _ffn(xs_sorted, plan, w_gate, w_up, w_down, l)
        xs = _combine(y_sorted, dest_blocks, xs, ws, mod[l], final_g[None], n_ctx, tok, last)
    return xs[None]
```

```python
import functools

import jax
import jax.numpy as jnp
from jax import lax
from jax.experimental import pallas as pl
from jax.experimental.pallas import tpu as pltpu

HEAD_DIM = 128
LANES = 128
BF16_ROWS = 16
KV_HEADS = 2
GRID_W = 64
WINDOW = 128
POOL_WINDOWS = (2, 4, 8, 16)
POOL_HALO = 16
N_GROUPS = 4
ROPE_THETA = 10000.0
RMS_EPS = 1e-6
NEG_INF = -1e30
ATTN_SCALE = HEAD_DIM ** -0.5
LOG2E = 1.4426950408889634
SOFTMAX_SAFE_LOG2 = 50.0
MOE_BLOCK = 256
DMA_UNROLL = 8
VMEM_CAP = 60 * 1024 * 1024

BF16 = jnp.bfloat16
F32 = jnp.float32


def _pick(n, prefs):
    for p in prefs:
        if n % p == 0:
            return p
    raise ValueError(f"no tile in {prefs} divides {n}")


def _largest_tile(n, cap, mult):
    return max(k for k in range(mult, min(n, cap) + 1, mult) if n % k == 0)


def _params(n_axes, vmem_bytes):
    limit = int(min(max(vmem_bytes * 5 // 4 + (4 << 20), 16 << 20), VMEM_CAP))
    return pltpu.CompilerParams(dimension_semantics=("arbitrary",) * n_axes, vmem_limit_bytes=limit)


def _sigmoid(x):
    return 1.0 / (1.0 + jnp.exp(-x))


def _sigmoid_tanh(x):
    return 0.5 * jnp.tanh(0.5 * x) + 0.5


def _dot(a, b):
    return jnp.dot(a, b, preferred_element_type=F32)


def _dot_nt(a, b):
    return lax.dot_general(a, b, (((1,), (1,)), ((), ())), preferred_element_type=F32)


def _row_select(row0, n_rows, n_ctx, mod_ref):
    rows = row0 + lax.broadcasted_iota(jnp.int32, (n_rows, 1), 0)
    return jnp.where(rows < n_ctx, mod_ref[1:2, :], mod_ref[0:1, :])


def _norm_modulate(x, g_ref, shift, scale):
    ms = jnp.mean(x * x, axis=-1, keepdims=True)
    y = x * lax.rsqrt(ms + RMS_EPS) * g_ref[...]
    return y * (1.0 + scale) + shift


def _ada_kernel(cb_ref, w_ref, b_ref, o_ref, *, tn):
    s = cb_ref[...]
    s = s * _sigmoid(s)
    outs = []
    for r in range(2):
        cols = [jnp.sum(w_ref[:, c * LANES:(c + 1) * LANES] * s[r], axis=0, keepdims=True)
                for c in range(tn // LANES)]
        outs.append(jnp.concatenate(cols, axis=1) + b_ref[...])
    o_ref[...] = jnp.concatenate(outs + [jnp.zeros((6, tn), F32)], axis=0)


def _ada(c, c_ctx, w_ada, b_ada):
    depth, d, w6 = w_ada.shape
    tn = _pick(w6, (1024, 512, 256, 128))
    cb = jnp.broadcast_to(jnp.stack([c[0], c_ctx])[:, :, None], (2, d, LANES))
    vmem = 2 * d * tn * 4 + 2 * d * LANES * 4 * 2 + d * LANES * 4 * 4
    return pl.pallas_call(
        functools.partial(_ada_kernel, tn=tn),
        grid=(depth, w6 // tn),
        in_specs=[pl.BlockSpec((2, d, LANES), lambda l, j: (0, 0, 0)),
                  pl.BlockSpec((None, d, tn), lambda l, j: (l, 0, j)),
                  pl.BlockSpec((None, 1, tn), lambda l, j: (l, 0, j))],
        out_specs=pl.BlockSpec((None, 8, tn), lambda l, j: (l, 0, j)),
        out_shape=jax.ShapeDtypeStruct((depth, 8, w6), F32),
        compiler_params=_params(2, vmem),
        name="ada",
    )(cb, w_ada, b_ada.reshape(depth, 1, w6))


def _rope(x, cos, sin_signed, first_half):
    rot = jnp.where(first_half, pltpu.roll(x, 3 * HEAD_DIM // 4, axis=1), pltpu.roll(x, HEAD_DIM // 4, axis=1))
    return x * cos + rot * sin_signed


def _head_rms(x, g_ref):
    ms = jnp.mean(x * x, axis=-1, keepdims=True)
    return x * lax.rsqrt(ms + RMS_EPS) * g_ref[...]


def _inproj_kernel(x_ref, shift_ref, scale_ref, g_ref, w_ref, cos_ref, sin_ref, qg_ref, kg_ref,
                   o_ref, h_sc, *, tm, tn, n_ctx, signatures):
    i = pl.program_id(0)
    j = pl.program_id(1)

    def normed():
        x = x_ref[...]
        return x * lax.rsqrt(jnp.mean(x * x, axis=-1, keepdims=True) + RMS_EPS)

    has_ctx_rows = i * tm < n_ctx

    @pl.when((j == 0) & has_ctx_rows)
    def _():
        gain = g_ref[...] * (1.0 + scale_ref[0:2, :])
        rows = i * tm + lax.broadcasted_iota(jnp.int32, (tm, 1), 0)
        is_ctx = rows < n_ctx
        h = normed() * jnp.where(is_ctx, gain[1:2], gain[0:1]) + jnp.where(is_ctx, shift_ref[1:2, :], shift_ref[0:1, :])
        h_sc[...] = h.astype(BF16)

    @pl.when((j == 0) & jnp.logical_not(has_ctx_rows))
    def _():
        h = normed() * (g_ref[...] * (1.0 + scale_ref[0:1, :])) + shift_ref[0:1, :]
        h_sc[...] = h.astype(BF16)

    acc = _dot(h_sc[...], w_ref[...].astype(BF16))

    lane = lax.broadcasted_iota(jnp.int32, (1, HEAD_DIM), 1)
    first_half = (lane % (HEAD_DIM // 2)) < (HEAD_DIM // 4)

    for sig, js in signatures:
        cond = functools.reduce(jnp.logical_or, [j == jj for jj in js])

        @pl.when(cond)
        def _(sig=sig):
            if all(t == "plain" for t in sig):
                o_ref[...] = acc.astype(BF16)
                return
            for bi, typ in enumerate(sig):
                a = acc[:, bi * LANES:(bi + 1) * LANES]
                if typ != "plain":
                    if typ in ("qnr", "knr"):
                        a = _head_rms(a, qg_ref if typ == "qnr" else kg_ref)
                    a = _rope(a, cos_ref[...], sin_ref[...], first_half)
                    if typ in ("qnr", "qr"):
                        a = a * (ATTN_SCALE * LOG2E)
                o_ref[:, bi * LANES:(bi + 1) * LANES] = a.astype(BF16)


def _col_types(mix, d):
    hq = mix // HEAD_DIM
    types = (["plain"] * hq + ["qnr"] * hq + ["knr"] * KV_HEADS + ["plain"] * KV_HEADS
             + ["qr"] * hq + ["kr"] * KV_HEADS + ["plain"] * KV_HEADS + ["plain"] * (3 * d // LANES))
    return types


def _inproj(xs, mod_l, g1, w_in, layer, cos, sin_signed, qg, kg, n_ctx):
    t, d = xs.shape
    in_w = w_in.shape[-1]
    mix = d // 2
    tm = _pick(t, (768, 512, 256, 128))
    tn = _pick(in_w, (1024, 512, 256))
    types = _col_types(mix, d)
    assert len(types) * LANES == in_w
    per = tn // LANES
    sigs = {}
    for jj in range(in_w // tn):
        sigs.setdefault(tuple(types[jj * per:(jj + 1) * per]), []).append(jj)
    signatures = tuple((s, tuple(js)) for s, js in sigs.items())
    vmem = (2 * tm * d * 4 + 2 * d * tn * 4 + tm * d * 2 + 2 * tm * tn * 2 + tm * tn * 4 * 2 + d * tn * 2
            + 4 * tm * LANES * 4)
    return pl.pallas_call(
        functools.partial(_inproj_kernel, tm=tm, tn=tn, n_ctx=n_ctx, signatures=signatures),
        grid=(t // tm, in_w // tn),
        in_specs=[pl.BlockSpec((tm, d), lambda i, j: (i, 0)),
                  pl.BlockSpec((8, d), lambda i, j: (0, 0)),
                  pl.BlockSpec((8, d), lambda i, j: (0, 1)),
                  pl.BlockSpec((1, d), lambda i, j: (0, 0)),
                  pl.BlockSpec((None, d, tn), lambda i, j: (layer, 0, j)),
                  pl.BlockSpec((tm, HEAD_DIM), lambda i, j: (i, 0)),
                  pl.BlockSpec((tm, HEAD_DIM), lambda i, j: (i, 0)),
                  pl.BlockSpec((1, HEAD_DIM), lambda i, j: (0, 0)),
                  pl.BlockSpec((1, HEAD_DIM), lambda i, j: (0, 0))],
        out_specs=pl.BlockSpec((tm, tn), lambda i, j: (i, j)),
        out_shape=jax.ShapeDtypeStruct((t, in_w), BF16),
        scratch_shapes=[pltpu.VMEM((tm, d), BF16)],
        compiler_params=_params(2, vmem),
        name="inproj",
    )(xs, mod_l, mod_l, g1, w_in, cos, sin_signed, qg, kg)


def _lane_repeat(x, n):
    return jnp.concatenate([x] * n, axis=1)


def _stack_heads(q_ref, g):
    return jnp.concatenate([q_ref[:, h * HEAD_DIM:(h + 1) * HEAD_DIM] for h in range(g)], axis=0)


def _unstack_store(o_ref, out, g, tq):
    for h in range(g):
        o_ref[:, h * HEAD_DIM:(h + 1) * HEAD_DIM] = out[h * tq:(h + 1) * tq, :].astype(o_ref.dtype)


def _gattn_kernel(q_ref, k_ref, v_ref, o_ref, vx_sc, kmax_sc, m_sc, acc_sc, *, tq, tk, g, n_ctx, n_lat):
    qi = pl.program_id(1)

    @pl.when(qi == 0)
    def _():
        vx_sc[:, 0:HEAD_DIM] = v_ref[...]
        vx_sc[:, HEAD_DIM:] = jnp.ones((vx_sc.shape[0], HEAD_DIM), BF16)
        kf = k_ref[...].astype(F32)
        k2 = jnp.max(jnp.sum(kf * kf, axis=-1, keepdims=True), axis=0, keepdims=True)
        kmax_sc[...] = jnp.broadcast_to(jnp.sqrt(k2), kmax_sc.shape)

    q = _stack_heads(q_ref, g)
    rows = q.shape[0]
    qf = q.astype(F32)
    bound = jnp.sqrt(jnp.sum(qf * qf, axis=-1, keepdims=True)) * kmax_sc[0:1, :]
    bounded = jnp.max(bound) <= SOFTMAX_SAFE_LOG2
    ctx_only = qi * tq < n_ctx
    ctx_chunk = [(0, n_ctx)]
    all_chunks = ctx_chunk + [(n_ctx + c * tk, tk) for c in range(n_lat // tk)]

    def finish():
        acc = acc_sc[...]
        _unstack_store(o_ref, acc[:, :HEAD_DIM] / acc[:, HEAD_DIM:], g, tq)

    def attend_bounded(chunks):
        for n, (lo, size) in enumerate(chunks):
            s = _dot_nt(q, k_ref[lo:lo + size, :])
            p = jnp.exp2(s - _lane_repeat(bound, size // LANES))
            pv = _dot(p.astype(BF16), vx_sc[lo:lo + size, :])
            acc_sc[...] = pv if n == 0 else acc_sc[...] + pv
        finish()

    def attend_online(chunks):
        for n, (lo, size) in enumerate(chunks):
            s = _dot_nt(q, k_ref[lo:lo + size, :])
            mx = jnp.max(s, axis=-1, keepdims=True)
            if n == 0:
                m_new = jnp.broadcast_to(mx, (rows, LANES))
            else:
                m_prev = m_sc[...]
                m_new = jnp.maximum(m_prev, mx)
            p = jnp.exp2(s - _lane_repeat(m_new, size // LANES))
            pv = _dot(p.astype(BF16), vx_sc[lo:lo + size, :])
            if n == 0:
                acc_sc[...] = pv
            else:
                acc_sc[...] = _lane_repeat(jnp.exp2(m_prev - m_new), 2) * acc_sc[...] + pv
            m_sc[...] = m_new
        finish()

    for use_bound, attend in ((True, attend_bounded), (False, attend_online)):
        path = bounded if use_bound else jnp.logical_not(bounded)
        pl.when(path & ctx_only)(functools.partial(attend, ctx_chunk))
        pl.when(path & jnp.logical_not(ctx_only))(functools.partial(attend, all_chunks))


def _gattn(p, n_ctx, mix):
    t = p.shape[0]
    hq = mix // HEAD_DIM
    g = hq // KV_HEADS
    n_lat = t - n_ctx
    tq = _pick(n_ctx, (256, 128))
    assert t % tq == 0
    tk = _pick(n_lat, (2048, 1024, 512, 256, 128))
    gw = g * HEAD_DIM
    off_q = mix // gw
    off_k = (mix + hq * HEAD_DIM) // HEAD_DIM
    off_v = off_k + KV_HEADS
    rows = g * tq
    vmem = (2 * tq * gw * 2 * 2 + 2 * 2 * t * HEAD_DIM * 2 + t * 2 * HEAD_DIM * 2 + rows * LANES * 4 * 4
            + rows * max(tk, n_ctx) * 4 * 3)
    return pl.pallas_call(
        functools.partial(_gattn_kernel, tq=tq, tk=tk, g=g, n_ctx=n_ctx, n_lat=n_lat),
        grid=(KV_HEADS, t // tq),
        in_specs=[pl.BlockSpec((tq, gw), lambda h, i: (i, off_q + h)),
                  pl.BlockSpec((t, HEAD_DIM), lambda h, i: (0, off_k + h)),
                  pl.BlockSpec((t, HEAD_DIM), lambda h, i: (0, off_v + h))],
        out_specs=pl.BlockSpec((tq, gw), lambda h, i: (i, h)),
        out_shape=jax.ShapeDtypeStruct((t, mix), BF16),
        scratch_shapes=[pltpu.VMEM((t, 2 * HEAD_DIM), BF16), pltpu.VMEM((8, LANES), F32),
                        pltpu.VMEM((rows, LANES), F32), pltpu.VMEM((rows, 2 * HEAD_DIM), F32)],
        compiler_params=_params(2, vmem),
        name="gattn",
    )(p, p, p)


def _wattn_kernel(*refs, tq, n_sub, g, n_ctx, span, t):
    q_refs, k_refs, v_refs = refs[0:KV_HEADS], refs[KV_HEADS:2 * KV_HEADS], refs[2 * KV_HEADS:3 * KV_HEADS]
    sink_ref, o_ref, vx_sc = refs[3 * KV_HEADS:]
    qi = pl.program_id(0)

    @pl.when(qi == 0)
    def _():
        for h in range(KV_HEADS):
            vx_sc[h, :, 0:HEAD_DIM] = v_refs[h][...]
            vx_sc[h, :, HEAD_DIM:] = jnp.ones((t, HEAD_DIM), BF16)

    for s in range(n_sub):
        for h in range(KV_HEADS):
            _wattn_head(qi * n_sub + s, q_refs[h].at[s * tq:(s + 1) * tq], k_refs[h], vx_sc.at[h], sink_ref.at[h],
                        o_ref.at[s * tq:(s + 1) * tq, h * g * HEAD_DIM:(h + 1) * g * HEAD_DIM],
                        tq=tq, g=g, n_ctx=n_ctx, span=span, t=t)


def _wattn_head(qi, q_ref, k_ref, vx_sc, sink_ref, o_ref, *, tq, g, n_ctx, span, t):
    q = _stack_heads(q_ref, g)
    rows = q.shape[0]
    start = pl.multiple_of(jnp.clip(qi * tq - WINDOW, 0, t - span), LANES)

    far = t + 4 * WINDOW
    qrow = qi * tq + lax.broadcasted_iota(jnp.int32, (tq, 1), 0)
    qrow = jnp.concatenate([jnp.where(qrow >= n_ctx, qrow, -far)] * g, axis=0)
    krow = start + lax.broadcasted_iota(jnp.int32, (1, span), 1)
    krow = jnp.where(krow >= n_ctx, krow, far)
    keep = jnp.abs(krow - qrow) <= WINDOW

    s_c = _dot_nt(q, k_ref[0:n_ctx, :])
    s_w = jnp.where(keep, _dot_nt(q, k_ref[pl.ds(start, span), :]), NEG_INF)
    sink = sink_ref[...] * LOG2E
    mx = jnp.maximum(jnp.maximum(jnp.max(s_c, axis=-1, keepdims=True), jnp.max(s_w, axis=-1, keepdims=True)), sink)
    m = jnp.broadcast_to(mx, (rows, LANES))
    p_c = jnp.exp2(s_c - _lane_repeat(m, n_ctx // LANES))
    p_w = jnp.exp2(s_w - _lane_repeat(m, span // LANES))
    acc = _dot(p_c.astype(BF16), vx_sc[0:n_ctx, :]) + _dot(p_w.astype(BF16), vx_sc[pl.ds(start, span), :])
    denom = acc[:, HEAD_DIM:] + jnp.exp2(sink - m)
    _unstack_store(o_ref, acc[:, :HEAD_DIM] / denom, g, tq)


def _wattn(p, sink_l, n_ctx, mix):
    t = p.shape[0]
    hq = mix // HEAD_DIM
    g = hq // KV_HEADS
    tq = _pick(n_ctx, (256, 128))
    span = tq + 2 * WINDOW
    assert t % tq == 0 and t >= span
    gw = g * HEAD_DIM
    base = mix + hq * HEAD_DIM + 2 * KV_HEADS * HEAD_DIM
    off_q = base // gw
    off_k = (base + hq * HEAD_DIM) // HEAD_DIM
    off_v = off_k + KV_HEADS
    rows = g * tq
    sink_rows = jnp.repeat(sink_l.reshape(KV_HEADS, g), tq, axis=1).reshape(KV_HEADS, rows, 1)
    n_sub = 3 if (t // tq) % 3 == 0 else 1
    vmem = KV_HEADS * (2 * n_sub * tq * gw * 2 * 2 + 2 * 2 * t * HEAD_DIM * 2 + 2 * rows * LANES * 4
                       + n_sub * rows * (span + n_ctx) * 4 * 3 + t * 2 * HEAD_DIM * 2)
    heads = range(KV_HEADS)
    return pl.pallas_call(
        functools.partial(_wattn_kernel, tq=tq, n_sub=n_sub, g=g, n_ctx=n_ctx, span=span, t=t),
        grid=(t // (tq * n_sub),),
        in_specs=([pl.BlockSpec((tq * n_sub, gw), lambda i, h=h: (i, off_q + h)) for h in heads]
                  + [pl.BlockSpec((t, HEAD_DIM), lambda i, h=h: (0, off_k + h)) for h in heads]
                  + [pl.BlockSpec((t, HEAD_DIM), lambda i, h=h: (0, off_v + h)) for h in heads]
                  + [pl.BlockSpec((KV_HEADS, rows, 1), lambda i: (0, 0, 0))]),
        out_specs=pl.BlockSpec((tq * n_sub, mix), lambda i: (i, 0)),
        out_shape=jax.ShapeDtypeStruct((t, mix), BF16),
        scratch_shapes=[pltpu.VMEM((KV_HEADS, t, 2 * HEAD_DIM), BF16)],
        compiler_params=_params(1, vmem),
        name="wattn",
    )(*([p] * (3 * KV_HEADS)), sink_rows)


def _merge_kernel(u_ref, up_ref, un_ref, og_ref, ow_ref, g0_ref, g1_ref, g2_ref, pw_ref, ps_ref, wb_ref,
                  z_ref, ext_sc, pool_sc, *, tm, n_ctx, t, mix):
    i = pl.program_id(0)
    j = pl.program_id(1)
    gw = mix // len(POOL_WINDOWS)

    def pool(clipped):
        ext_sc[0:POOL_HALO, :] = up_ref[...].astype(F32)
        ext_sc[POOL_HALO:POOL_HALO + tm, :] = u_ref[...].astype(F32)
        ext_sc[POOL_HALO + tm:, :] = un_ref[...].astype(F32)
        r = i * tm + lax.broadcasted_iota(jnp.int32, (tm, 1), 0)
        r_ctx = r < n_ctx
        for gi, w in enumerate(POOL_WINDOWS):
            c0, c1 = gi * gw, (gi + 1) * gw
            tot = jnp.zeros((tm, gw), F32)
            cnt = jnp.zeros((tm, 1), F32)
            for off in range(-((w - 1) // 2), w // 2 + 1):
                part = ext_sc[POOL_HALO + off:POOL_HALO + off + tm, c0:c1]
                if clipped:
                    rr = r + off
                    ok = (rr >= 0) & (rr < t) & ((rr < n_ctx) == r_ctx)
                    part = jnp.where(ok, part, 0.0)
                    cnt = cnt + ok.astype(F32)
                tot = tot + part
            mean = tot / cnt if clipped else tot * (1.0 / w)
            dlt = mean - ext_sc[POOL_HALO:POOL_HALO + tm, c0:c1]
            y = _dot(dlt.astype(BF16), pw_ref[gi].astype(BF16)) * ps_ref[:, c0:c1]
            pool_sc[:, c0:c1] = y.astype(BF16)

    reach = max(POOL_WINDOWS) // 2
    interior = (i * tm - reach >= n_ctx) & ((i + 1) * tm + reach <= t)
    pl.when((j == 0) & interior)(functools.partial(pool, False))
    pl.when((j == 0) & jnp.logical_not(interior))(functools.partial(pool, True))

    def gate(g_ref):
        return _sigmoid_tanh(g_ref[...].astype(F32))

    z = gate(g0_ref) * _dot(pool_sc[...], wb_ref[0].astype(BF16))
    z = z + gate(g1_ref) * _dot(og_ref[...], wb_ref[1].astype(BF16))
    z = z + gate(g2_ref) * _dot(ow_ref[...], wb_ref[2].astype(BF16))
    z_ref[...] = z.astype(BF16)


def _merge(p, og, ow, pool_w, pool_scale, w_branch, layer, n_ctx, d):
    t = p.shape[0]
    mix = d // 2
    tm = _largest_tile(t, 1100, POOL_HALO)
    tn = _pick(d, (512, 256, 128))
    hb = tm // POOL_HALO
    n_hb = t // POOL_HALO
    gate0 = (p.shape[1] - 3 * d) // tn
    gs = pool_w.shape[-1]
    vmem = (2 * 3 * tm * mix * 2 + 2 * 3 * tm * tn * 2 + 2 * 3 * mix * tn * 4 + 3 * mix * tn * 2
            + (tm + 2 * POOL_HALO) * mix * 4 + tm * mix * 2 + 2 * tm * tn * 2 + 4 * tm * tn * 4
            + 2 * len(POOL_WINDOWS) * gs * gs * 4 + 6 * tm * gs * 4)
    return pl.pallas_call(
        functools.partial(_merge_kernel, tm=tm, n_ctx=n_ctx, t=t, mix=mix),
        grid=(t // tm, d // tn),
        in_specs=[pl.BlockSpec((tm, mix), lambda i, j: (i, 0)),
                  pl.BlockSpec((POOL_HALO, mix), lambda i, j: (jnp.maximum(i * hb - 1, 0), 0)),
                  pl.BlockSpec((POOL_HALO, mix), lambda i, j: (jnp.minimum((i + 1) * hb, n_hb - 1), 0)),
                  pl.BlockSpec((tm, mix), lambda i, j: (i, 0)),
                  pl.BlockSpec((tm, mix), lambda i, j: (i, 0)),
                  pl.BlockSpec((tm, tn), lambda i, j: (i, gate0 + j)),
                  pl.BlockSpec((tm, tn), lambda i, j: (i, gate0 + d // tn + j)),
                  pl.BlockSpec((tm, tn), lambda i, j: (i, gate0 + 2 * (d // tn) + j)),
                  pl.BlockSpec((None, len(POOL_WINDOWS), gs, gs), lambda i, j: (layer, 0, 0, 0)),
                  pl.BlockSpec((1, mix), lambda i, j: (0, 0)),
                  pl.BlockSpec((None, 3, mix, tn), lambda i, j: (layer, 0, 0, j))],
        out_specs=pl.BlockSpec((tm, tn), lambda i, j: (i, j)),
        out_shape=jax.ShapeDtypeStruct((t, d), BF16),
        scratch_shapes=[pltpu.VMEM((tm + 2 * POOL_HALO, mix), F32), pltpu.VMEM((tm, mix), BF16)],
        compiler_params=_params(2, vmem),
        name="merge",
    )(p, p, p, og, ow, p, p, p, pool_w, pool_scale, w_branch)


def _outproj_kernel(z_ref, w_ref, xs_ref, gate_ref, o_ref, *, tm, n_ctx):
    i = pl.program_id(0)
    gate = _row_select(i * tm, tm, n_ctx, gate_ref)
    o_ref[...] = xs_ref[...] + gate * _dot(z_ref[...], w_ref[...].astype(BF16))


def _outproj(z, w_out, layer, xs, mod_l, n_ctx):
    t, d = xs.shape
    tm = _largest_tile(t, 1536, BF16_ROWS)
    tn = _pick(d, (512, 256, 128))
    vmem = 2 * tm * d * 2 + 2 * d * tn * 4 + d * tn * 2 + 4 * tm * tn * 4 + 2 * tm * tn * 4
    return pl.pallas_call(
        functools.partial(_outproj_kernel, tm=tm, n_ctx=n_ctx),
        grid=(t // tm, d // tn),
        in_specs=[pl.BlockSpec((tm, d), lambda i, j: (i, 0)),
                  pl.BlockSpec((None, d, tn), lambda i, j: (layer, 0, j)),
                  pl.BlockSpec((tm, tn), lambda i, j: (i, j)),
                  pl.BlockSpec((8, tn), lambda i, j: (0, 2 * (d // tn) + j))],
        out_specs=pl.BlockSpec((tm, tn), lambda i, j: (i, j)),
        out_shape=jax.ShapeDtypeStruct((t, d), F32),
        compiler_params=_params(2, vmem),
        name="outproj",
    )(z, w_out, xs, mod_l)


def _router_kernel(xs_ref, shift_ref, scale_ref, g_ref, rwt_ref, rb_ref,
                   h_ref, ints_ref, ws_ref, cnt_ref, carry_sc, *, tm, n_ctx, n_exp):
    i = pl.program_id(0)

    @pl.when(i == 0)
    def _():
        carry_sc[...] = jnp.zeros(carry_sc.shape, F32)

    shift = _row_select(i * tm, tm, n_ctx, shift_ref)
    scale = _row_select(i * tm, tm, n_ctx, scale_ref)
    h = _norm_modulate(xs_ref[...], g_ref, shift, scale)
    h_ref[...] = h

    scores = _sigmoid(_dot_nt(rwt_ref[...].astype(BF16), h.astype(BF16)))
    sel = scores + rb_ref[...]
    per = n_exp // N_GROUPS
    sub = lax.broadcasted_iota(jnp.int32, (per, tm), 0)

    def top2(v):
        m1 = jnp.max(v, axis=0, keepdims=True)
        i1 = jnp.min(jnp.where(v == m1, sub, per), axis=0, keepdims=True)
        rest = jnp.where(sub == i1, -jnp.inf, v)
        m2 = jnp.max(rest, axis=0, keepdims=True)
        i2 = jnp.min(jnp.where(rest == m2, sub, per), axis=0, keepdims=True)
        return m1 + m2, i1, i2

    tops = [top2(sel[gi * per:(gi + 1) * per, :]) for gi in range(N_GROUPS)]
    best, l1, l2 = tops[0]
    grp = jnp.zeros((1, tm), jnp.int32)
    for gi in range(1, N_GROUPS):
        gs, a1, a2 = tops[gi]
        better = gs > best
        best = jnp.where(better, gs, best)
        grp = jnp.where(better, gi, grp)
        l1 = jnp.where(better, a1, l1)
        l2 = jnp.where(better, a2, l2)

    hot1 = [(grp == gi) & (sub == l1) for gi in range(N_GROUPS)]
    hot2 = [(grp == gi) & (sub == l2) for gi in range(N_GROUPS)]
    assign = jnp.concatenate([(a | b).astype(F32) for a, b in zip(hot1, hot2)], axis=0)

    before = (lax.broadcasted_iota(jnp.int32, (tm, tm), 0) < lax.broadcasted_iota(jnp.int32, (tm, tm), 1))
    pos = _dot(assign.astype(BF16), before.astype(F32).astype(BF16)) + carry_sc[...]

    def pick(hots, val):
        return sum(jnp.sum(jnp.where(hots[gi], val[gi * per:(gi + 1) * per, :], 0.0), axis=0, keepdims=True)
                   for gi in range(N_GROUPS))

    s1 = pick(hot1, scores)
    s2 = pick(hot2, scores)
    r1 = pick(hot1, pos)
    r2 = pick(hot2, pos)
    tot = s1 + s2
    ints_ref[0:1, :] = grp * per + l1
    ints_ref[1:2, :] = grp * per + l2
    ints_ref[2:3, :] = r1.astype(jnp.int32)
    ints_ref[3:4, :] = r2.astype(jnp.int32)
    ints_ref[4:8, :] = jnp.zeros((4, tm), jnp.int32)
    ws_ref[0:1, :] = s1 / tot
    ws_ref[1:2, :] = s2 / tot
    ws_ref[2:8, :] = jnp.zeros((6, tm), F32)
    carry_sc[...] = carry_sc[...] + jnp.sum(assign, axis=1, keepdims=True)
    cnt_ref[...] = jnp.broadcast_to(carry_sc[...], cnt_ref.shape)


def _router(xs, mod_l, g2, router_w, router_b, n_ctx):
    t, d = xs.shape
    n_exp = router_w.shape[1]
    tm = _pick(t, (768, 512, 256, 128))
    vmem = 2 * tm * d * 4 * 2 + 4 * tm * d * 4 + 3 * tm * tm * 4 + 2 * n_exp * d * 4
    return pl.pallas_call(
        functools.partial(_router_kernel, tm=tm, n_ctx=n_ctx, n_exp=n_exp),
        grid=(t // tm,),
        in_specs=[pl.BlockSpec((tm, d), lambda i: (i, 0)),
                  pl.BlockSpec((8, d), lambda i: (0, 3)),
                  pl.BlockSpec((8, d), lambda i: (0, 4)),
                  pl.BlockSpec((1, d), lambda i: (0, 0)),
                  pl.BlockSpec((n_exp, d), lambda i: (0, 0)),
                  pl.BlockSpec((n_exp, 1), lambda i: (0, 0))],
        out_specs=[pl.BlockSpec((tm, d), lambda i: (i, 0)),
                   pl.BlockSpec((8, tm), lambda i: (0, i)),
                   pl.BlockSpec((8, tm), lambda i: (0, i)),
                   pl.BlockSpec((n_exp, LANES), lambda i: (0, 0))],
        out_shape=[jax.ShapeDtypeStruct((t, d), F32),
                   jax.ShapeDtypeStruct((8, t), jnp.int32),
                   jax.ShapeDtypeStruct((8, t), F32),
                   jax.ShapeDtypeStruct((n_exp, LANES), F32)],
        scratch_shapes=[pltpu.VMEM((n_exp, 1), F32)],
        compiler_params=_params(1, vmem),
        name="router",
    )(xs, mod_l, mod_l, g2, router_w.T, router_b.reshape(n_exp, 1))


def _row_copy(src_ref, src_row, dst_ref, dst_row, sem):
    return pltpu.make_async_copy(src_ref.at[pl.ds(src_row, 1)], dst_ref.at[pl.ds(dst_row, 1)], sem)


def _dispatch_kernel(dst_ref, h_ref, xs_hbm, sem, *, tm):
    def issue(r, carry):
        for k in range(2):
            _row_copy(h_ref, r, xs_hbm, dst_ref[0, 0, k * tm + r], sem).start(priority=k)
        return carry

    lax.fori_loop(0, tm, issue, 0, unroll=DMA_UNROLL)
    for k in range(2):
        pltpu.make_async_copy(h_ref, xs_hbm.at[pl.ds(0, tm)], sem).wait()


def _dest_blocks(ints, pad_start, tm):
    t = ints.shape[1]
    experts = jnp.arange(pad_start.shape[0], dtype=jnp.int32)
    start = jnp.sum(jnp.where(ints[0:2, :, None] == experts, pad_start, 0), axis=-1)
    dest = start + ints[2:4]
    return dest.reshape(2, t // tm, tm).transpose(1, 0, 2).reshape(t // tm, 1, 2 * tm)


def _dispatch(h2, dest_blocks, n_rows, tm):
    t, d = h2.shape
    return pl.pallas_call(
        functools.partial(_dispatch_kernel, tm=tm),
        grid=(t // tm,),
        in_specs=[pl.BlockSpec((1, 1, 2 * tm), lambda i: (i, 0, 0), memory_space=pltpu.SMEM),
                  pl.BlockSpec((tm, d), lambda i: (i, 0))],
        out_specs=pl.BlockSpec(memory_space=pl.ANY),
        out_shape=jax.ShapeDtypeStruct((n_rows, d), F32),
        scratch_shapes=[pltpu.SemaphoreType.DMA(())],
        compiler_params=_params(1, 2 * tm * d * 4),
        name="dispatch",
    )(dest_blocks, h2)


def _ffn_kernel(blk_valid_ref, blk_seg_ref, blk_first_ref, seg_e_ref, cnt_ref, x_ref, wg_hbm, wu_hbm, wd_hbm, y_ref,
                wg_st, wu_st, wd_st, wg_sc, wu_sc, wd_sc, sem_g, sem_u, sem_d, *, layer):
    b = pl.program_id(0)
    n_seg = cnt_ref[1]

    def copies(k):
        e = seg_e_ref[k]
        return [pltpu.make_async_copy(w.at[layer, e], st, sm)
                for w, st, sm in ((wg_hbm, wg_st, sem_g), (wu_hbm, wu_st, sem_u), (wd_hbm, wd_st, sem_d))]

    def x_block():
        rows = lax.broadcasted_iota(jnp.int32, (x_ref.shape[0], 1), 0)
        return jnp.where(rows < blk_valid_ref[b], x_ref[...], 0.0).astype(BF16)

    def finish(gte, up):
        hidden = (gte * _sigmoid(gte) * up).astype(BF16)
        y_ref[...] = _dot(hidden, wd_sc[...])

    @pl.when(b < cnt_ref[0])
    def _():
        @pl.when(b == 0)
        def _():
            for cp in copies(0):
                cp.start()

        first = blk_first_ref[b] == 1

        @pl.when(first)
        def _():
            k = blk_seg_ref[b]
            copy_g, copy_u, copy_d = copies(k)
            x = x_block()
            copy_g.wait()
            wg_sc[...] = wg_st[...].astype(BF16)
            copy_u.wait()
            gte = _dot(x, wg_sc[...])
            wu_sc[...] = wu_st[...].astype(BF16)
            copy_d.wait()
            up = _dot(x, wu_sc[...])
            wd_sc[...] = wd_st[...].astype(BF16)

            @pl.when(k + 1 < n_seg)
            def _():
                for cp in copies(k + 1):
                    cp.start()

            finish(gte, up)

        @pl.when(jnp.logical_not(first))
        def _():
            x = x_block()
            finish(_dot(x, wg_sc[...]), _dot(x, wu_sc[...]))


def _ffn(xs_sorted, plan, w_gate, w_up, w_down, layer):
    n_rows, d = xs_sorted.shape
    ff = w_gate.shape[-1]
    blk = MOE_BLOCK
    n_blk = n_rows // blk
    prefetch = (plan["blk_valid"], plan["blk_seg"], plan["blk_first"], plan["seg_e"], plan["cnt"])
    row_map = lambda b, v, s, f, e, c: (jnp.minimum(b, c[0] - 1), 0)
    hbm = pl.BlockSpec(memory_space=pl.ANY)
    dma = pltpu.SemaphoreType.DMA(())
    grid_spec = pltpu.PrefetchScalarGridSpec(
        num_scalar_prefetch=len(prefetch),
        grid=(n_blk,),
        in_specs=[pl.BlockSpec((blk, d), row_map), hbm, hbm, hbm],
        out_specs=pl.BlockSpec((blk, d), row_map),
        scratch_shapes=[pltpu.VMEM((d, ff), F32), pltpu.VMEM((d, ff), F32), pltpu.VMEM((ff, d), F32),
                        pltpu.VMEM((d, ff), BF16), pltpu.VMEM((d, ff), BF16), pltpu.VMEM((ff, d), BF16),
                        dma, dma, dma],
    )
    vmem = 3 * d * ff * (4 + 2) + 4 * blk * d * 4 + blk * d * 2 + 3 * blk * ff * 4 + blk * d * 4
    return pl.pallas_call(
        functools.partial(_ffn_kernel, layer=layer),
        grid_spec=grid_spec,
        out_shape=jax.ShapeDtypeStruct((n_rows, d), F32),
        compiler_params=pltpu.CompilerParams(dimension_semantics=("arbitrary",), vmem_limit_bytes=min(vmem + (6 << 20), VMEM_CAP)),
        name="ffn",
    )(*prefetch, xs_sorted, w_gate, w_up, w_down)


def _combine_kernel(src_ref, nxt_ref, y_hbm, xs_ref, ws_ref, gate_ref, fg_ref, o_ref, ybuf, sem, *, tm, n_ctx, final):
    i = pl.program_id(0)
    slot = i % 2

    def gather(idx_ref, into):
        def issue(r, carry):
            for k in range(2):
                _row_copy(y_hbm, idx_ref[0, 0, k * tm + r], ybuf.at[into, k], r, sem.at[into]).start(priority=k)
            return carry

        lax.fori_loop(0, tm, issue, 0, unroll=DMA_UNROLL)

    @pl.when(i == 0)
    def _():
        gather(src_ref, 0)

    @pl.when(i + 1 < pl.num_programs(0))
    def _():
        gather(nxt_ref, 1 - slot)

    for k in range(2):
        pltpu.make_async_copy(y_hbm.at[pl.ds(0, tm)], ybuf.at[slot, k], sem.at[slot]).wait()

    wcol = jnp.transpose(ws_ref[...])
    y = wcol[:, 0:1] * ybuf[slot, 0] + wcol[:, 1:2] * ybuf[slot, 1]
    x = xs_ref[...] + _row_select(i * tm, tm, n_ctx, gate_ref) * y
    if final:
        ms = jnp.mean(x * x, axis=-1, keepdims=True)
        x = x * lax.rsqrt(ms + RMS_EPS) * fg_ref[...]
    o_ref[...] = x


def _combine(y_sorted, dest_blocks, xs, ws, mod_l, final_g, n_ctx, tm, final):
    t, d = xs.shape
    if final:
        skip = n_ctx // tm
        out_rows = t - n_ctx
        out_map = lambda i: (jnp.maximum(i - skip, 0), 0)
    else:
        out_rows = t
        out_map = lambda i: (i, 0)
    return pl.pallas_call(
        functools.partial(_combine_kernel, tm=tm, n_ctx=n_ctx, final=final),
        grid=(t // tm,),
        in_specs=[pl.BlockSpec((1, 1, 2 * tm), lambda i: (i, 0, 0), memory_space=pltpu.SMEM),
                  pl.BlockSpec((1, 1, 2 * tm), lambda i: (jnp.minimum(i + 1, t // tm - 1), 0, 0), memory_space=pltpu.SMEM),
                  pl.BlockSpec(memory_space=pl.ANY),
                  pl.BlockSpec((tm, d), lambda i: (i, 0)),
                  pl.BlockSpec((8, tm), lambda i: (0, i)),
                  pl.BlockSpec((8, d), lambda i: (0, 5)),
                  pl.BlockSpec((1, d), lambda i: (0, 0))],
        out_specs=pl.BlockSpec((tm, d), out_map),
        out_shape=jax.ShapeDtypeStruct((out_rows, d), F32),
        scratch_shapes=[pltpu.VMEM((2, 2, tm, d), F32), pltpu.SemaphoreType.DMA((2,))],
        compiler_params=_params(1, 4 * tm * d * 4 + 4 * tm * d * 4 + 4 * tm * d * 4),
        name="combine",
    )(dest_blocks, dest_blocks, y_sorted, xs, ws, mod_l, final_g)


def _rope_tables(n_lat, n_ctx):
    rows = n_lat // GRID_W
    row_ids = jnp.repeat(jnp.arange(rows, dtype=F32), GRID_W)
    col_ids = jnp.tile(jnp.arange(GRID_W, dtype=F32), rows)
    n_freq = HEAD_DIM // 4
    inv = ROPE_THETA ** (-jnp.arange(n_freq, dtype=F32) / n_freq)
    ar = row_ids[:, None] * inv
    ac = col_ids[:, None] * inv
    ang = jnp.concatenate([ar, ar, ac, ac], axis=-1)
    sign = jnp.where((jnp.arange(HEAD_DIM) % (HEAD_DIM // 2)) < HEAD_DIM // 4, -1.0, 1.0).astype(F32)
    cos = jnp.concatenate([jnp.ones((n_ctx, HEAD_DIM), F32), jnp.cos(ang)], axis=0)
    sin = jnp.concatenate([jnp.zeros((n_ctx, HEAD_DIM), F32), jnp.sin(ang) * sign], axis=0)
    return cos, sin


def _moe_plan(counts, n_blk):
    blk = MOE_BLOCK
    n_exp = counts.shape[0]
    padded = (counts + blk - 1) // blk * blk
    pad_end = jnp.cumsum(padded)
    pad_start = pad_end - padded
    n_used = jnp.maximum(pad_end[-1] // blk, 1)
    b = jnp.minimum(jnp.arange(n_blk, dtype=jnp.int32), n_used - 1)
    blk_e = jnp.sum((pad_end[None, :] <= (b * blk)[:, None]).astype(jnp.int32), axis=1)
    blk_e = jnp.minimum(blk_e, n_exp - 1)
    blk_valid = jnp.clip(counts[blk_e] - (b * blk - pad_start[blk_e]), 0, blk).astype(jnp.int32)
    live = counts > 0
    seg_of_e = jnp.cumsum(live.astype(jnp.int32)) - 1
    n_seg = jnp.maximum(seg_of_e[-1] + 1, 1)
    ks = jnp.arange(n_exp, dtype=jnp.int32)
    seg_e = jnp.sum(jnp.where(live[None, :] & (seg_of_e[None, :] == ks[:, None]), ks[None, :], 0), axis=1)
    blk_first = jnp.concatenate([jnp.ones((1,), jnp.int32), (blk_e[1:] != blk_e[:-1]).astype(jnp.int32)])
    return {
        "pad_start": pad_start.astype(jnp.int32),
        "blk_valid": blk_valid,
        "blk_seg": seg_of_e[blk_e].astype(jnp.int32),
        "blk_first": blk_first,
        "seg_e": seg_e.astype(jnp.int32),
        "cnt": jnp.stack([n_used, n_seg]).astype(jnp.int32),
    }


def kernel(x, c, ctx, c_ctx, w_ada, b_ada, norm1_g, norm2_g, w_in, pool_w, pool_scale, q_norm_g, k_norm_g,
           sink, w_branch, w_out, router_w, router_b, w_gate, w_up, w_down, final_g):
    assert x.shape[0] == 1, "single-sequence kernel"
    n_lat, d = x.shape[1], x.shape[2]
    n_ctx = ctx.shape[1]
    t = n_ctx + n_lat
    mix = d // 2
    depth = w_in.shape[0]
    n_exp = router_w.shape[1]
    tok = _pick(n_ctx, (256, 128))
    assert (2 * t) % MOE_BLOCK == 0
    n_blk = 2 * t // MOE_BLOCK + n_exp
    cos, sin_signed = _rope_tables(n_lat, n_ctx)
    mod = _ada(c, c_ctx, w_ada, b_ada)
    xs = jnp.concatenate([ctx[0], x[0]], axis=0)
    for l in range(depth):
        last = l == depth - 1
        p = _inproj(xs, mod[l], norm1_g[l][None], w_in, l, cos, sin_signed,
                    q_norm_g[l][None], k_norm_g[l][None], n_ctx)
        og = _gattn(p, n_ctx, mix)
        ow = _wattn(p, sink[l], n_ctx, mix)
        z = _merge(p, og, ow, pool_w, pool_scale[l][None], w_branch, l, n_ctx, d)
        xs = _outproj(z, w_out, l, xs, mod[l], n_ctx)
        h2, ints, ws, cnt = _router(xs, mod[l], norm2_g[l][None], router_w, router_b, n_ctx)
        plan = _moe_plan(cnt[:, 0].astype(jnp.int32), n_blk)
        dest_blocks = _dest_blocks(ints, plan["pad_start"], tok)
        xs_sorted = _dispatch(h2, dest_blocks, n_blk * MOE_BLOCK, tok)
        y_sorted = _ffn(xs_sorted, plan, w_gate, w_up, w_down, l)
        xs = _combine(y_sorted, dest_blocks, xs, ws, mod[l], final_g[None], n_ctx, tok, last)
    return xs[None]
```

```python
import functools

import jax
import jax.numpy as jnp
from jax import lax
from jax.experimental import pallas as pl
from jax.experimental.pallas import tpu as pltpu

HEAD_DIM = 128
LANES = 128
BF16_ROWS = 16
KV_HEADS = 2
GRID_W = 64
WINDOW = 128
POOL_WINDOWS = (2, 4, 8, 16)
POOL_HALO = 16
N_GROUPS = 4
ROPE_THETA = 10000.0
RMS_EPS = 1e-6
NEG_INF = -1e30
ATTN_SCALE = HEAD_DIM ** -0.5
LOG2E = 1.4426950408889634
SOFTMAX_SAFE_LOG2 = 50.0
MOE_BLOCK = 256
DMA_UNROLL = 8
VMEM_CAP = 60 * 1024 * 1024

BF16 = jnp.bfloat16
F32 = jnp.float32


def _pick(n, prefs):
    for p in prefs:
        if n % p == 0:
            return p
    raise ValueError(f"no tile in {prefs} divides {n}")


def _largest_tile(n, cap, mult):
    return max(k for k in range(mult, min(n, cap) + 1, mult) if n % k == 0)


def _params(n_axes, vmem_bytes):
    limit = int(min(max(vmem_bytes * 5 // 4 + (4 << 20), 16 << 20), VMEM_CAP))
    return pltpu.CompilerParams(dimension_semantics=("arbitrary",) * n_axes, vmem_limit_bytes=limit)


def _sigmoid(x):
    return 1.0 / (1.0 + jnp.exp(-x))


def _sigmoid_tanh(x):
    return 0.5 * jnp.tanh(0.5 * x) + 0.5


def _dot(a, b):
    return jnp.dot(a, b, preferred_element_type=F32)


def _dot_nt(a, b):
    return lax.dot_general(a, b, (((1,), (1,)), ((), ())), preferred_element_type=F32)


def _row_select(row0, n_rows, n_ctx, mod_ref):
    rows = row0 + lax.broadcasted_iota(jnp.int32, (n_rows, 1), 0)
    return jnp.where(rows < n_ctx, mod_ref[1:2, :], mod_ref[0:1, :])


def _norm_modulate(x, g_ref, shift, scale):
    ms = jnp.mean(x * x, axis=-1, keepdims=True)
    y = x * lax.rsqrt(ms + RMS_EPS) * g_ref[...]
    return y * (1.0 + scale) + shift


def _ada_kernel(cb_ref, w_ref, b_ref, o_ref, *, tn):
    s = cb_ref[...]
    s = s * _sigmoid(s)
    outs = []
    for r in range(2):
        cols = [jnp.sum(w_ref[:, c * LANES:(c + 1) * LANES] * s[r], axis=0, keepdims=True)
                for c in range(tn // LANES)]
        outs.append(jnp.concatenate(cols, axis=1) + b_ref[...])
    o_ref[...] = jnp.concatenate(outs + [jnp.zeros((6, tn), F32)], axis=0)


def _ada(c, c_ctx, w_ada, b_ada):
    depth, d, w6 = w_ada.shape
    tn = _pick(w6, (1024, 512, 256, 128))
    cb = jnp.broadcast_to(jnp.stack([c[0], c_ctx])[:, :, None], (2, d, LANES))
    vmem = 2 * d * tn * 4 + 2 * d * LANES * 4 * 2 + d * LANES * 4 * 4
    return pl.pallas_call(
        functools.partial(_ada_kernel, tn=tn),
        grid=(depth, w6 // tn),
        in_specs=[pl.BlockSpec((2, d, LANES), lambda l, j: (0, 0, 0)),
                  pl.BlockSpec((None, d, tn), lambda l, j: (l, 0, j)),
                  pl.BlockSpec((None, 1, tn), lambda l, j: (l, 0, j))],
        out_specs=pl.BlockSpec((None, 8, tn), lambda l, j: (l, 0, j)),
        out_shape=jax.ShapeDtypeStruct((depth, 8, w6), F32),
        compiler_params=_params(2, vmem),
        name="ada",
    )(cb, w_ada, b_ada.reshape(depth, 1, w6))


def _rope(x, cos, sin_signed, first_half):
    rot = jnp.where(first_half, pltpu.roll(x, 3 * HEAD_DIM // 4, axis=1), pltpu.roll(x, HEAD_DIM // 4, axis=1))
    return x * cos + rot * sin_signed


def _head_rms(x, g_ref):
    ms = jnp.mean(x * x, axis=-1, keepdims=True)
    return x * lax.rsqrt(ms + RMS_EPS) * g_ref[...]


def _inproj_kernel(x_ref, shift_ref, scale_ref, g_ref, w_ref, cos_ref, sin_ref, qg_ref, kg_ref,
                   o_ref, h_sc, *, tm, tn, n_ctx, signatures):
    i = pl.program_id(0)
    j = pl.program_id(1)

    def normed():
        x = x_ref[...]
        return x * lax.rsqrt(jnp.mean(x * x, axis=-1, keepdims=True) + RMS_EPS)

    has_ctx_rows = i * tm < n_ctx

    @pl.when((j == 0) & has_ctx_rows)
    def _():
        gain = g_ref[...] * (1.0 + scale_ref[0:2, :])
        rows = i * tm + lax.broadcasted_iota(jnp.int32, (tm, 1), 0)
        is_ctx = rows < n_ctx
        h = normed() * jnp.where(is_ctx, gain[1:2], gain[0:1]) + jnp.where(is_ctx, shift_ref[1:2, :], shift_ref[0:1, :])
        h_sc[...] = h.astype(BF16)

    @pl.when((j == 0) & jnp.logical_not(has_ctx_rows))
    def _():
        h = normed() * (g_ref[...] * (1.0 + scale_ref[0:1, :])) + shift_ref[0:1, :]
        h_sc[...] = h.astype(BF16)

    acc = _dot(h_sc[...], w_ref[...].astype(BF16))

    lane = lax.broadcasted_iota(jnp.int32, (1, HEAD_DIM), 1)
    first_half = (lane % (HEAD_DIM // 2)) < (HEAD_DIM // 4)

    for sig, js in signatures:
        cond = functools.reduce(jnp.logical_or, [j == jj for jj in js])

        @pl.when(cond)
        def _(sig=sig):
            if all(t == "plain" for t in sig):
                o_ref[...] = acc.astype(BF16)
                return
            for bi, typ in enumerate(sig):
                a = acc[:, bi * LANES:(bi + 1) * LANES]
                if typ != "plain":
                    if typ in ("qnr", "knr"):
                        a = _head_rms(a, qg_ref if typ == "qnr" else kg_ref)
                    a = _rope(a, cos_ref[...], sin_ref[...], first_half)
                    if typ in ("qnr", "qr"):
                        a = a * (ATTN_SCALE * LOG2E)
                o_ref[:, bi * LANES:(bi + 1) * LANES] = a.astype(BF16)


def _col_types(mix, d):
    hq = mix // HEAD_DIM
    types = (["plain"] * hq + ["qnr"] * hq + ["knr"] * KV_HEADS + ["plain"] * KV_HEADS
             + ["qr"] * hq + ["kr"] * KV_HEADS + ["plain"] * KV_HEADS + ["plain"] * (3 * d // LANES))
    return types


def _inproj(xs, mod_l, g1, w_in, layer, cos, sin_signed, qg, kg, n_ctx):
    t, d = xs.shape
    in_w = w_in.shape[-1]
    mix = d // 2
    tm = _pick(t, (768, 512, 256, 128))
    tn = _pick(in_w, (1024, 512, 256))
    types = _col_types(mix, d)
    assert len(types) * LANES == in_w
    per = tn // LANES
    sigs = {}
    for jj in range(in_w // tn):
        sigs.setdefault(tuple(types[jj * per:(jj + 1) * per]), []).append(jj)
    signatures = tuple((s, tuple(js)) for s, js in sigs.items())
    vmem = (2 * tm * d * 4 + 2 * d * tn * 4 + tm * d * 2 + 2 * tm * tn * 2 + tm * tn * 4 * 2 + d * tn * 2
            + 4 * tm * LANES * 4)
    return pl.pallas_call(
        functools.partial(_inproj_kernel, tm=tm, tn=tn, n_ctx=n_ctx, signatures=signatures),
        grid=(t // tm, in_w // tn),
        in_specs=[pl.BlockSpec((tm, d), lambda i, j: (i, 0)),
                  pl.BlockSpec((8, d), lambda i, j: (0, 0)),
                  pl.BlockSpec((8, d), lambda i, j: (0, 1)),
                  pl.BlockSpec((1, d), lambda i, j: (0, 0)),
                  pl.BlockSpec((None, d, tn), lambda i, j: (layer, 0, j)),
                  pl.BlockSpec((tm, HEAD_DIM), lambda i, j: (i, 0)),
                  pl.BlockSpec((tm, HEAD_DIM), lambda i, j: (i, 0)),
                  pl.BlockSpec((1, HEAD_DIM), lambda i, j: (0, 0)),
                  pl.BlockSpec((1, HEAD_DIM), lambda i, j: (0, 0))],
        out_specs=pl.BlockSpec((tm, tn), lambda i, j: (i, j)),
        out_shape=jax.ShapeDtypeStruct((t, in_w), BF16),
        scratch_shapes=[pltpu.VMEM((tm, d), BF16)],
        compiler_params=_params(2, vmem),
        name="inproj",
    )(xs, mod_l, mod_l, g1, w_in, cos, sin_signed, qg, kg)


def _lane_repeat(x, n):
    return jnp.concatenate([x] * n, axis=1)


def _stack_heads(q_ref, g):
    return jnp.concatenate([q_ref[:, h * HEAD_DIM:(h + 1) * HEAD_DIM] for h in range(g)], axis=0)


def _unstack_store(o_ref, out, g, tq):
    for h in range(g):
        o_ref[:, h * HEAD_DIM:(h + 1) * HEAD_DIM] = out[h * tq:(h + 1) * tq, :].astype(o_ref.dtype)


def _gattn_kernel(qmax_ref, q_ref, k_ref, v_ref, o_ref, vx_sc, kmax_sc, flag_sc, m_sc, acc_sc, *, tq, tk, g, n_ctx, n_lat):
    qi = pl.program_id(1)
    rows = g * tq

    @pl.when(qi == 0)
    def _():
        vx_sc[:, 0:HEAD_DIM] = v_ref[...]
        vx_sc[:, HEAD_DIM:] = jnp.ones((vx_sc.shape[0], HEAD_DIM), BF16)
        kf = k_ref[...].astype(F32)
        k2 = jnp.max(jnp.sum(kf * kf, axis=-1, keepdims=True), axis=0, keepdims=True)
        kmax = jnp.broadcast_to(jnp.sqrt(k2), kmax_sc.shape)
        kmax_sc[...] = kmax
        flag_sc[0] = (qmax_ref[0] * jnp.max(kmax) <= SOFTMAX_SAFE_LOG2).astype(jnp.int32)

    bounded = flag_sc[0] == 1
    ctx_only = qi * tq < n_ctx
    ctx_chunk = [(0, n_ctx)]
    all_chunks = ctx_chunk + [(n_ctx + c * tk, tk) for c in range(n_lat // tk)]

    def finish():
        acc = acc_sc[...]
        _unstack_store(o_ref, acc[:, :HEAD_DIM] / acc[:, HEAD_DIM:], g, tq)

    def attend_bounded(chunks):
        q = _stack_heads(q_ref, g)
        qf = q.astype(F32)
        bound = jnp.sqrt(jnp.sum(qf * qf, axis=-1, keepdims=True)) * kmax_sc[0:1, :]
        for n, (lo, size) in enumerate(chunks):
            s = _dot_nt(q, k_ref[lo:lo + size, :])
            p = jnp.exp2(s - _lane_repeat(bound, size // LANES))
            pv = _dot(p.astype(BF16), vx_sc[lo:lo + size, :])
            acc_sc[...] = pv if n == 0 else acc_sc[...] + pv
        finish()

    def attend_online(chunks):
        q = _stack_heads(q_ref, g)
        for n, (lo, size) in enumerate(chunks):
            s = _dot_nt(q, k_ref[lo:lo + size, :])
            mx = jnp.max(s, axis=-1, keepdims=True)
            if n == 0:
                m_new = jnp.broadcast_to(mx, (rows, LANES))
            else:
                m_prev = m_sc[...]
                m_new = jnp.maximum(m_prev, mx)
            p = jnp.exp2(s - _lane_repeat(m_new, size // LANES))
            pv = _dot(p.astype(BF16), vx_sc[lo:lo + size, :])
            if n == 0:
                acc_sc[...] = pv
            else:
                acc_sc[...] = _lane_repeat(jnp.exp2(m_prev - m_new), 2) * acc_sc[...] + pv
            m_sc[...] = m_new
        finish()

    for use_bound, attend in ((True, attend_bounded), (False, attend_online)):
        path = bounded if use_bound else jnp.logical_not(bounded)
        pl.when(path & ctx_only)(functools.partial(attend, ctx_chunk))
        pl.when(path & jnp.logical_not(ctx_only))(functools.partial(attend, all_chunks))


def _gattn(p, q_gain, n_ctx, mix):
    t = p.shape[0]
    qmax = (1.02 * HEAD_DIM ** 0.5 * ATTN_SCALE * LOG2E) * jnp.max(jnp.abs(q_gain)).reshape(1)
    hq = mix // HEAD_DIM
    g = hq // KV_HEADS
    n_lat = t - n_ctx
    tq = _pick(n_ctx, (256, 128))
    assert t % tq == 0
    tk = _pick(n_lat, (2048, 1024, 512, 256, 128))
    gw = g * HEAD_DIM
    off_q = mix // gw
    off_k = (mix + hq * HEAD_DIM) // HEAD_DIM
    off_v = off_k + KV_HEADS
    rows = g * tq
    vmem = (2 * tq * gw * 2 * 2 + 2 * 2 * t * HEAD_DIM * 2 + t * 2 * HEAD_DIM * 2 + rows * LANES * 4 * 4
            + rows * max(tk, n_ctx) * 4 * 3)
    return pl.pallas_call(
        functools.partial(_gattn_kernel, tq=tq, tk=tk, g=g, n_ctx=n_ctx, n_lat=n_lat),
        grid=(KV_HEADS, t // tq),
        in_specs=[pl.BlockSpec(memory_space=pltpu.SMEM),
                  pl.BlockSpec((tq, gw), lambda h, i: (i, off_q + h)),
                  pl.BlockSpec((t, HEAD_DIM), lambda h, i: (0, off_k + h)),
                  pl.BlockSpec((t, HEAD_DIM), lambda h, i: (0, off_v + h))],
        out_specs=pl.BlockSpec((tq, gw), lambda h, i: (i, h)),
        out_shape=jax.ShapeDtypeStruct((t, mix), BF16),
        scratch_shapes=[pltpu.VMEM((t, 2 * HEAD_DIM), BF16), pltpu.VMEM((8, LANES), F32), pltpu.SMEM((1,), jnp.int32),
                        pltpu.VMEM((rows, LANES), F32), pltpu.VMEM((rows, 2 * HEAD_DIM), F32)],
        compiler_params=_params(2, vmem),
        name="gattn",
    )(qmax, p, p, p)


def _wattn_kernel(*refs, tq, n_sub, g, n_ctx, span, t):
    q_refs, k_refs, v_refs = refs[0:KV_HEADS], refs[KV_HEADS:2 * KV_HEADS], refs[2 * KV_HEADS:3 * KV_HEADS]
    sink_ref, o_ref, vx_sc = refs[3 * KV_HEADS:]
    qi = pl.program_id(0)

    @pl.when(qi == 0)
    def _():
        for h in range(KV_HEADS):
            vx_sc[h, :, 0:HEAD_DIM] = v_refs[h][...]
            vx_sc[h, :, HEAD_DIM:] = jnp.ones((t, HEAD_DIM), BF16)

    for s in range(n_sub):
        for h in range(KV_HEADS):
            _wattn_head(qi * n_sub + s, q_refs[h].at[s * tq:(s + 1) * tq], k_refs[h], vx_sc.at[h], sink_ref.at[h],
                        o_ref.at[s * tq:(s + 1) * tq, h * g * HEAD_DIM:(h + 1) * g * HEAD_DIM],
                        tq=tq, g=g, n_ctx=n_ctx, span=span, t=t)


def _wattn_head(qi, q_ref, k_ref, vx_sc, sink_ref, o_ref, *, tq, g, n_ctx, span, t):
    q = _stack_heads(q_ref, g)
    rows = q.shape[0]
    start = pl.multiple_of(jnp.clip(qi * tq - WINDOW, 0, t - span), LANES)

    far = t + 4 * WINDOW
    qrow = qi * tq + lax.broadcasted_iota(jnp.int32, (tq, 1), 0)
    qrow = jnp.concatenate([jnp.where(qrow >= n_ctx, qrow, -far)] * g, axis=0)
    krow = start + lax.broadcasted_iota(jnp.int32, (1, span), 1)
    krow = jnp.where(krow >= n_ctx, krow, far)
    keep = jnp.abs(krow - qrow) <= WINDOW

    s_c = _dot_nt(q, k_ref[0:n_ctx, :])
    s_w = jnp.where(keep, _dot_nt(q, k_ref[pl.ds(start, span), :]), NEG_INF)
    sink = sink_ref[...] * LOG2E
    mx = jnp.maximum(jnp.maximum(jnp.max(s_c, axis=-1, keepdims=True), jnp.max(s_w, axis=-1, keepdims=True)), sink)
    m = jnp.broadcast_to(mx, (rows, LANES))
    p_c = jnp.exp2(s_c - _lane_repeat(m, n_ctx // LANES))
    p_w = jnp.exp2(s_w - _lane_repeat(m, span // LANES))
    acc = _dot(p_c.astype(BF16), vx_sc[0:n_ctx, :]) + _dot(p_w.astype(BF16), vx_sc[pl.ds(start, span), :])
    denom = acc[:, HEAD_DIM:] + jnp.exp2(sink - m)
    _unstack_store(o_ref, acc[:, :HEAD_DIM] / denom, g, tq)


def _wattn(p, sink_l, n_ctx, mix):
    t = p.shape[0]
    hq = mix // HEAD_DIM
    g = hq // KV_HEADS
    tq = _pick(n_ctx, (256, 128))
    span = tq + 2 * WINDOW
    assert t % tq == 0 and t >= span
    gw = g * HEAD_DIM
    base = mix + hq * HEAD_DIM + 2 * KV_HEADS * HEAD_DIM
    off_q = base // gw
    off_k = (base + hq * HEAD_DIM) // HEAD_DIM
    off_v = off_k + KV_HEADS
    rows = g * tq
    sink_rows = jnp.repeat(sink_l.reshape(KV_HEADS, g), tq, axis=1).reshape(KV_HEADS, rows, 1)
    n_sub = 3 if (t // tq) % 3 == 0 else 1
    vmem = KV_HEADS * (2 * n_sub * tq * gw * 2 * 2 + 2 * 2 * t * HEAD_DIM * 2 + 2 * rows * LANES * 4
                       + n_sub * rows * (span + n_ctx) * 4 * 3 + t * 2 * HEAD_DIM * 2)
    heads = range(KV_HEADS)
    return pl.pallas_call(
        functools.partial(_wattn_kernel, tq=tq, n_sub=n_sub, g=g, n_ctx=n_ctx, span=span, t=t),
        grid=(t // (tq * n_sub),),
        in_specs=([pl.BlockSpec((tq * n_sub, gw), lambda i, h=h: (i, off_q + h)) for h in heads]
                  + [pl.BlockSpec((t, HEAD_DIM), lambda i, h=h: (0, off_k + h)) for h in heads]
                  + [pl.BlockSpec((t, HEAD_DIM), lambda i, h=h: (0, off_v + h)) for h in heads]
                  + [pl.BlockSpec((KV_HEADS, rows, 1), lambda i: (0, 0, 0))]),
        out_specs=pl.BlockSpec((tq * n_sub, mix), lambda i: (i, 0)),
        out_shape=jax.ShapeDtypeStruct((t, mix), BF16),
        scratch_shapes=[pltpu.VMEM((KV_HEADS, t, 2 * HEAD_DIM), BF16)],
        compiler_params=_params(1, vmem),
        name="wattn",
    )(*([p] * (3 * KV_HEADS)), sink_rows)


def _merge_kernel(u_ref, up_ref, un_ref, og_ref, ow_ref, g0_ref, g1_ref, g2_ref, pw_ref, ps_ref, wb_ref,
                  z_ref, ext_sc, pool_sc, *, tm, n_ctx, t, mix):
    i = pl.program_id(0)
    j = pl.program_id(1)
    gw = mix // len(POOL_WINDOWS)

    def pool(clipped):
        ext_sc[0:POOL_HALO, :] = up_ref[...].astype(F32)
        ext_sc[POOL_HALO:POOL_HALO + tm, :] = u_ref[...].astype(F32)
        ext_sc[POOL_HALO + tm:, :] = un_ref[...].astype(F32)
        r = i * tm + lax.broadcasted_iota(jnp.int32, (tm, 1), 0)
        r_ctx = r < n_ctx
        for gi, w in enumerate(POOL_WINDOWS):
            c0, c1 = gi * gw, (gi + 1) * gw
            tot = jnp.zeros((tm, gw), F32)
            cnt = jnp.zeros((tm, 1), F32)
            for off in range(-((w - 1) // 2), w // 2 + 1):
                part = ext_sc[POOL_HALO + off:POOL_HALO + off + tm, c0:c1]
                if clipped:
                    rr = r + off
                    ok = (rr >= 0) & (rr < t) & ((rr < n_ctx) == r_ctx)
                    part = jnp.where(ok, part, 0.0)
                    cnt = cnt + ok.astype(F32)
                tot = tot + part
            mean = tot / cnt if clipped else tot * (1.0 / w)
            dlt = mean - ext_sc[POOL_HALO:POOL_HALO + tm, c0:c1]
            y = _dot(dlt.astype(BF16), pw_ref[gi].astype(BF16)) * ps_ref[:, c0:c1]
            pool_sc[:, c0:c1] = y.astype(BF16)

    reach = max(POOL_WINDOWS) // 2
    interior = (i * tm - reach >= n_ctx) & ((i + 1) * tm + reach <= t)
    pl.when((j == 0) & interior)(functools.partial(pool, False))
    pl.when((j == 0) & jnp.logical_not(interior))(functools.partial(pool, True))

    def gate(g_ref):
        return _sigmoid_tanh(g_ref[...].astype(F32))

    z = gate(g0_ref) * _dot(pool_sc[...], wb_ref[0].astype(BF16))
    z = z + gate(g1_ref) * _dot(og_ref[...], wb_ref[1].astype(BF16))
    z = z + gate(g2_ref) * _dot(ow_ref[...], wb_ref[2].astype(BF16))
    z_ref[...] = z.astype(BF16)


def _merge(p, og, ow, pool_w, pool_scale, w_branch, layer, n_ctx, d):
    t = p.shape[0]
    mix = d // 2
    tm = _largest_tile(t, 1100, POOL_HALO)
    tn = _pick(d, (512, 256, 128))
    hb = tm // POOL_HALO
    n_hb = t // POOL_HALO
    gate0 = (p.shape[1] - 3 * d) // tn
    gs = pool_w.shape[-1]
    vmem = (2 * 3 * tm * mix * 2 + 2 * 3 * tm * tn * 2 + 2 * 3 * mix * tn * 4 + 3 * mix * tn * 2
            + (tm + 2 * POOL_HALO) * mix * 4 + tm * mix * 2 + 2 * tm * tn * 2 + 4 * tm * tn * 4
            + 2 * len(POOL_WINDOWS) * gs * gs * 4 + 6 * tm * gs * 4)
    return pl.pallas_call(
        functools.partial(_merge_kernel, tm=tm, n_ctx=n_ctx, t=t, mix=mix),
        grid=(t // tm, d // tn),
        in_specs=[pl.BlockSpec((tm, mix), lambda i, j: (i, 0)),
                  pl.BlockSpec((POOL_HALO, mix), lambda i, j: (jnp.maximum(i * hb - 1, 0), 0)),
                  pl.BlockSpec((POOL_HALO, mix), lambda i, j: (jnp.minimum((i + 1) * hb, n_hb - 1), 0)),
                  pl.BlockSpec((tm, mix), lambda i, j: (i, 0)),
                  pl.BlockSpec((tm, mix), lambda i, j: (i, 0)),
                  pl.BlockSpec((tm, tn), lambda i, j: (i, gate0 + j)),
                  pl.BlockSpec((tm, tn), lambda i, j: (i, gate0 + d // tn + j)),
                  pl.BlockSpec((tm, tn), lambda i, j: (i, gate0 + 2 * (d // tn) + j)),
                  pl.BlockSpec((None, len(POOL_WINDOWS), gs, gs), lambda i, j: (layer, 0, 0, 0)),
                  pl.BlockSpec((1, mix), lambda i, j: (0, 0)),
                  pl.BlockSpec((None, 3, mix, tn), lambda i, j: (layer, 0, 0, j))],
        out_specs=pl.BlockSpec((tm, tn), lambda i, j: (i, j)),
        out_shape=jax.ShapeDtypeStruct((t, d), BF16),
        scratch_shapes=[pltpu.VMEM((tm + 2 * POOL_HALO, mix), F32), pltpu.VMEM((tm, mix), BF16)],
        compiler_params=_params(2, vmem),
        name="merge",
    )(p, p, p, og, ow, p, p, p, pool_w, pool_scale, w_branch)


def _outproj_kernel(z_ref, w_ref, xs_ref, gate_ref, o_ref, *, tm, n_ctx):
    i = pl.program_id(0)
    gate = _row_select(i * tm, tm, n_ctx, gate_ref)
    o_ref[...] = xs_ref[...] + gate * _dot(z_ref[...], w_ref[...].astype(BF16))


def _outproj(z, w_out, layer, xs, mod_l, n_ctx):
    t, d = xs.shape
    tm = _largest_tile(t, 1536, BF16_ROWS)
    tn = _pick(d, (512, 256, 128))
    vmem = 2 * tm * d * 2 + 2 * d * tn * 4 + d * tn * 2 + 4 * tm * tn * 4 + 2 * tm * tn * 4
    return pl.pallas_call(
        functools.partial(_outproj_kernel, tm=tm, n_ctx=n_ctx),
        grid=(t // tm, d // tn),
        in_specs=[pl.BlockSpec((tm, d), lambda i, j: (i, 0)),
                  pl.BlockSpec((None, d, tn), lambda i, j: (layer, 0, j)),
                  pl.BlockSpec((tm, tn), lambda i, j: (i, j)),
                  pl.BlockSpec((8, tn), lambda i, j: (0, 2 * (d // tn) + j))],
        out_specs=pl.BlockSpec((tm, tn), lambda i, j: (i, j)),
        out_shape=jax.ShapeDtypeStruct((t, d), F32),
        compiler_params=_params(2, vmem),
        name="outproj",
    )(z, w_out, xs, mod_l)


def _router_kernel(xs_ref, shift_ref, scale_ref, g_ref, rwt_ref, rb_ref,
                   h_ref, ints_ref, ws_ref, cnt_ref, carry_sc, *, tm, n_ctx, n_exp):
    i = pl.program_id(0)

    @pl.when(i == 0)
    def _():
        carry_sc[...] = jnp.zeros(carry_sc.shape, F32)

    shift = _row_select(i * tm, tm, n_ctx, shift_ref)
    scale = _row_select(i * tm, tm, n_ctx, scale_ref)
    h = _norm_modulate(xs_ref[...], g_ref, shift, scale)
    h_ref[...] = h

    scores = _sigmoid(_dot_nt(rwt_ref[...].astype(BF16), h.astype(BF16)))
    sel = scores + rb_ref[...]
    per = n_exp // N_GROUPS
    sub = lax.broadcasted_iota(jnp.int32, (per, tm), 0)

    def top2(v):
        m1 = jnp.max(v, axis=0, keepdims=True)
        i1 = jnp.min(jnp.where(v == m1, sub, per), axis=0, keepdims=True)
        rest = jnp.where(sub == i1, -jnp.inf, v)
        m2 = jnp.max(rest, axis=0, keepdims=True)
        i2 = jnp.min(jnp.where(rest == m2, sub, per), axis=0, keepdims=True)
        return m1 + m2, i1, i2

    tops = [top2(sel[gi * per:(gi + 1) * per, :]) for gi in range(N_GROUPS)]
    best, l1, l2 = tops[0]
    grp = jnp.zeros((1, tm), jnp.int32)
    for gi in range(1, N_GROUPS):
        gs, a1, a2 = tops[gi]
        better = gs > best
        best = jnp.where(better, gs, best)
        grp = jnp.where(better, gi, grp)
        l1 = jnp.where(better, a1, l1)
        l2 = jnp.where(better, a2, l2)

    hot1 = [(grp == gi) & (sub == l1) for gi in range(N_GROUPS)]
    hot2 = [(grp == gi) & (sub == l2) for gi in range(N_GROUPS)]
    assign = jnp.concatenate([(a | b).astype(F32) for a, b in zip(hot1, hot2)], axis=0)

    before = (lax.broadcasted_iota(jnp.int32, (tm, tm), 0) < lax.broadcasted_iota(jnp.int32, (tm, tm), 1))
    pos = _dot(assign.astype(BF16), before.astype(F32).astype(BF16)) + carry_sc[...]

    def pick(hots, val):
        return sum(jnp.sum(jnp.where(hots[gi], val[gi * per:(gi + 1) * per, :], 0.0), axis=0, keepdims=True)
                   for gi in range(N_GROUPS))

    s1 = pick(hot1, scores)
    s2 = pick(hot2, scores)
    r1 = pick(hot1, pos)
    r2 = pick(hot2, pos)
    tot = s1 + s2
    ints_ref[0:1, :] = grp * per + l1
    ints_ref[1:2, :] = grp * per + l2
    ints_ref[2:3, :] = r1.astype(jnp.int32)
    ints_ref[3:4, :] = r2.astype(jnp.int32)
    ints_ref[4:8, :] = jnp.zeros((4, tm), jnp.int32)
    ws_ref[0:1, :] = s1 / tot
    ws_ref[1:2, :] = s2 / tot
    ws_ref[2:8, :] = jnp.zeros((6, tm), F32)
    carry_sc[...] = carry_sc[...] + jnp.sum(assign, axis=1, keepdims=True)
    cnt_ref[...] = jnp.broadcast_to(carry_sc[...], cnt_ref.shape)


def _router(xs, mod_l, g2, router_w, router_b, n_ctx):
    t, d = xs.shape
    n_exp = router_w.shape[1]
    tm = _pick(t, (768, 512, 256, 128))
    vmem = 2 * tm * d * 4 * 2 + 4 * tm * d * 4 + 3 * tm * tm * 4 + 2 * n_exp * d * 4
    return pl.pallas_call(
        functools.partial(_router_kernel, tm=tm, n_ctx=n_ctx, n_exp=n_exp),
        grid=(t // tm,),
        in_specs=[pl.BlockSpec((tm, d), lambda i: (i, 0)),
                  pl.BlockSpec((8, d), lambda i: (0, 3)),
                  pl.BlockSpec((8, d), lambda i: (0, 4)),
                  pl.BlockSpec((1, d), lambda i: (0, 0)),
                  pl.BlockSpec((n_exp, d), lambda i: (0, 0)),
                  pl.BlockSpec((n_exp, 1), lambda i: (0, 0))],
        out_specs=[pl.BlockSpec((tm, d), lambda i: (i, 0)),
                   pl.BlockSpec((8, tm), lambda i: (0, i)),
                   pl.BlockSpec((8, tm), lambda i: (0, i)),
                   pl.BlockSpec((n_exp, LANES), lambda i: (0, 0))],
        out_shape=[jax.ShapeDtypeStruct((t, d), F32),
                   jax.ShapeDtypeStruct((8, t), jnp.int32),
                   jax.ShapeDtypeStruct((8, t), F32),
                   jax.ShapeDtypeStruct((n_exp, LANES), F32)],
        scratch_shapes=[pltpu.VMEM((n_exp, 1), F32)],
        compiler_params=_params(1, vmem),
        name="router",
    )(xs, mod_l, mod_l, g2, router_w.T, router_b.reshape(n_exp, 1))


def _row_copy(src_ref, src_row, dst_ref, dst_row, sem):
    return pltpu.make_async_copy(src_ref.at[pl.ds(src_row, 1)], dst_ref.at[pl.ds(dst_row, 1)], sem)


def _dispatch_kernel(dst_ref, h_ref, xs_hbm, sem, *, tm):
    def issue(r, carry):
        for k in range(2):
            _row_copy(h_ref, r, xs_hbm, dst_ref[0, 0, k * tm + r], sem).start(priority=k)
        return carry

    lax.fori_loop(0, tm, issue, 0, unroll=DMA_UNROLL)
    for k in range(2):
        pltpu.make_async_copy(h_ref, xs_hbm.at[pl.ds(0, tm)], sem).wait()


def _dest_blocks(ints, pad_start, tm):
    t = ints.shape[1]
    experts = jnp.arange(pad_start.shape[0], dtype=jnp.int32)
    start = jnp.sum(jnp.where(ints[0:2, :, None] == experts, pad_start, 0), axis=-1)
    dest = start + ints[2:4]
    return dest.reshape(2, t // tm, tm).transpose(1, 0, 2).reshape(t // tm, 1, 2 * tm)


def _dispatch(h2, dest_blocks, n_rows, tm):
    t, d = h2.shape
    return pl.pallas_call(
        functools.partial(_dispatch_kernel, tm=tm),
        grid=(t // tm,),
        in_specs=[pl.BlockSpec((1, 1, 2 * tm), lambda i: (i, 0, 0), memory_space=pltpu.SMEM),
                  pl.BlockSpec((tm, d), lambda i: (i, 0))],
        out_specs=pl.BlockSpec(memory_space=pl.ANY),
        out_shape=jax.ShapeDtypeStruct((n_rows, d), F32),
        scratch_shapes=[pltpu.SemaphoreType.DMA(())],
        compiler_params=_params(1, 2 * tm * d * 4),
        name="dispatch",
    )(dest_blocks, h2)


def _ffn_kernel(blk_valid_ref, blk_seg_ref, blk_first_ref, seg_e_ref, cnt_ref, x_ref, wg_hbm, wu_hbm, wd_hbm, y_ref,
                wg_st, wu_st, wd_st, wg_sc, wu_sc, wd_sc, sem_g, sem_u, sem_d, *, layer):
    b = pl.program_id(0)
    n_seg = cnt_ref[1]

    def copies(k):
        e = seg_e_ref[k]
        return [pltpu.make_async_copy(w.at[layer, e], st, sm)
                for w, st, sm in ((wg_hbm, wg_st, sem_g), (wu_hbm, wu_st, sem_u), (wd_hbm, wd_st, sem_d))]

    def x_block():
        rows = lax.broadcasted_iota(jnp.int32, (x_ref.shape[0], 1), 0)
        return jnp.where(rows < blk_valid_ref[b], x_ref[...], 0.0).astype(BF16)

    def finish(gte, up):
        hidden = (gte * _sigmoid(gte) * up).astype(BF16)
        y_ref[...] = _dot(hidden, wd_sc[...])

    @pl.when(b < cnt_ref[0])
    def _():
        @pl.when(b == 0)
        def _():
            for cp in copies(0):
                cp.start()

        @pl.when(blk_first_ref[b] == 1)
        def _():
            k = blk_seg_ref[b]
            for cp, nxt, st, dst in zip(copies(k), copies(k + 1), (wg_st, wu_st, wd_st), (wg_sc, wu_sc, wd_sc)):
                cp.wait()
                dst[...] = st[...].astype(BF16)
                pl.when(k + 1 < n_seg)(nxt.start)

        x = x_block()
        finish(_dot(x, wg_sc[...]), _dot(x, wu_sc[...]))


def _ffn(xs_sorted, plan, w_gate, w_up, w_down, layer):
    n_rows, d = xs_sorted.shape
    ff = w_gate.shape[-1]
    blk = MOE_BLOCK
    n_blk = n_rows // blk
    prefetch = (plan["blk_valid"], plan["blk_seg"], plan["blk_first"], plan["seg_e"], plan["cnt"])
    row_map = lambda b, v, s, f, e, c: (jnp.minimum(b, c[0] - 1), 0)
    hbm = pl.BlockSpec(memory_space=pl.ANY)
    dma = pltpu.SemaphoreType.DMA(())
    grid_spec = pltpu.PrefetchScalarGridSpec(
        num_scalar_prefetch=len(prefetch),
        grid=(n_blk,),
        in_specs=[pl.BlockSpec((blk, d), row_map), hbm, hbm, hbm],
        out_specs=pl.BlockSpec((blk, d), row_map),
        scratch_shapes=[pltpu.VMEM((d, ff), F32), pltpu.VMEM((d, ff), F32), pltpu.VMEM((ff, d), F32),
                        pltpu.VMEM((d, ff), BF16), pltpu.VMEM((d, ff), BF16), pltpu.VMEM((ff, d), BF16),
                        dma, dma, dma],
    )
    vmem = 3 * d * ff * (4 + 2) + 4 * blk * d * 4 + blk * d * 2 + 3 * blk * ff * 4 + blk * d * 4
    return pl.pallas_call(
        functools.partial(_ffn_kernel, layer=layer),
        grid_spec=grid_spec,
        out_shape=jax.ShapeDtypeStruct((n_rows, d), F32),
        compiler_params=pltpu.CompilerParams(dimension_semantics=("arbitrary",), vmem_limit_bytes=min(vmem + (6 << 20), VMEM_CAP)),
        name="ffn",
    )(*prefetch, xs_sorted, w_gate, w_up, w_down)


def _combine_kernel(src_ref, nxt_ref, y_hbm, xs_ref, ws_ref, gate_ref, fg_ref, o_ref, ybuf, sem, *, tm, n_ctx, final):
    i = pl.program_id(0)
    slot = i % 2

    def gather(idx_ref, into):
        def issue(r, carry):
            for k in range(2):
                _row_copy(y_hbm, idx_ref[0, 0, k * tm + r], ybuf.at[into, k], r, sem.at[into]).start(priority=k)
            return carry

        lax.fori_loop(0, tm, issue, 0, unroll=DMA_UNROLL)

    @pl.when(i == 0)
    def _():
        gather(src_ref, 0)

    @pl.when(i + 1 < pl.num_programs(0))
    def _():
        gather(nxt_ref, 1 - slot)

    for k in range(2):
        pltpu.make_async_copy(y_hbm.at[pl.ds(0, tm)], ybuf.at[slot, k], sem.at[slot]).wait()

    wcol = jnp.transpose(ws_ref[...])
    y = wcol[:, 0:1] * ybuf[slot, 0] + wcol[:, 1:2] * ybuf[slot, 1]
    x = xs_ref[...] + _row_select(i * tm, tm, n_ctx, gate_ref) * y
    if final:
        ms = jnp.mean(x * x, axis=-1, keepdims=True)
        x = x * lax.rsqrt(ms + RMS_EPS) * fg_ref[...]
    o_ref[...] = x


def _combine(y_sorted, dest_blocks, xs, ws, mod_l, final_g, n_ctx, tm, final):
    t, d = xs.shape
    if final:
        skip = n_ctx // tm
        out_rows = t - n_ctx
        out_map = lambda i: (jnp.maximum(i - skip, 0), 0)
    else:
        out_rows = t
        out_map = lambda i: (i, 0)
    return pl.pallas_call(
        functools.partial(_combine_kernel, tm=tm, n_ctx=n_ctx, final=final),
        grid=(t // tm,),
        in_specs=[pl.BlockSpec((1, 1, 2 * tm), lambda i: (i, 0, 0), memory_space=pltpu.SMEM),
                  pl.BlockSpec((1, 1, 2 * tm), lambda i: (jnp.minimum(i + 1, t // tm - 1), 0, 0), memory_space=pltpu.SMEM),
                  pl.BlockSpec(memory_space=pl.ANY),
                  pl.BlockSpec((tm, d), lambda i: (i, 0)),
                  pl.BlockSpec((8, tm), lambda i: (0, i)),
                  pl.BlockSpec((8, d), lambda i: (0, 5)),
                  pl.BlockSpec((1, d), lambda i: (0, 0))],
        out_specs=pl.BlockSpec((tm, d), out_map),
        out_shape=jax.ShapeDtypeStruct((out_rows, d), F32),
        scratch_shapes=[pltpu.VMEM((2, 2, tm, d), F32), pltpu.SemaphoreType.DMA((2,))],
        compiler_params=_params(1, 4 * tm * d * 4 + 4 * tm * d * 4 + 4 * tm * d * 4),
        name="combine",
    )(dest_blocks, dest_blocks, y_sorted, xs, ws, mod_l, final_g)


def _rope_tables(n_lat, n_ctx):
    rows = n_lat // GRID_W
    row_ids = jnp.repeat(jnp.arange(rows, dtype=F32), GRID_W)
    col_ids = jnp.tile(jnp.arange(GRID_W, dtype=F32), rows)
    n_freq = HEAD_DIM // 4
    inv = ROPE_THETA ** (-jnp.arange(n_freq, dtype=F32) / n_freq)
    ar = row_ids[:, None] * inv
    ac = col_ids[:, None] * inv
    ang = jnp.concatenate([ar, ar, ac, ac], axis=-1)
    sign = jnp.where((jnp.arange(HEAD_DIM) % (HEAD_DIM // 2)) < HEAD_DIM // 4, -1.0, 1.0).astype(F32)
    cos = jnp.concatenate([jnp.ones((n_ctx, HEAD_DIM), F32), jnp.cos(ang)], axis=0)
    sin = jnp.concatenate([jnp.zeros((n_ctx, HEAD_DIM), F32), jnp.sin(ang) * sign], axis=0)
    return cos, sin


def _moe_plan(counts, n_blk):
    blk = MOE_BLOCK
    n_exp = counts.shape[0]
    padded = (counts + blk - 1) // blk * blk
    pad_end = jnp.cumsum(padded)
    pad_start = pad_end - padded
    n_used = jnp.maximum(pad_end[-1] // blk, 1)
    b = jnp.minimum(jnp.arange(n_blk, dtype=jnp.int32), n_used - 1)
    blk_e = jnp.sum((pad_end[None, :] <= (b * blk)[:, None]).astype(jnp.int32), axis=1)
    blk_e = jnp.minimum(blk_e, n_exp - 1)
    blk_valid = jnp.clip(counts[blk_e] - (b * blk - pad_start[blk_e]), 0, blk).astype(jnp.int32)
    live = counts > 0
    seg_of_e = jnp.cumsum(live.astype(jnp.int32)) - 1
    n_seg = jnp.maximum(seg_of_e[-1] + 1, 1)
    ks = jnp.arange(n_exp, dtype=jnp.int32)
    seg_e = jnp.sum(jnp.where(live[None, :] & (seg_of_e[None, :] == ks[:, None]), ks[None, :], 0), axis=1)
    blk_first = jnp.concatenate([jnp.ones((1,), jnp.int32), (blk_e[1:] != blk_e[:-1]).astype(jnp.int32)])
    return {
        "pad_start": pad_start.astype(jnp.int32),
        "blk_valid": blk_valid,
        "blk_seg": seg_of_e[blk_e].astype(jnp.int32),
        "blk_first": blk_first,
        "seg_e": jnp.concatenate([seg_e, seg_e[-1:]]).astype(jnp.int32),
        "cnt": jnp.stack([n_used, n_seg]).astype(jnp.int32),
    }


def kernel(x, c, ctx, c_ctx, w_ada, b_ada, norm1_g, norm2_g, w_in, pool_w, pool_scale, q_norm_g, k_norm_g,
           sink, w_branch, w_out, router_w, router_b, w_gate, w_up, w_down, final_g):
    assert x.shape[0] == 1, "single-sequence kernel"
    n_lat, d = x.shape[1], x.shape[2]
    n_ctx = ctx.shape[1]
    t = n_ctx + n_lat
    mix = d // 2
    depth = w_in.shape[0]
    n_exp = router_w.shape[1]
    tok = _pick(n_ctx, (256, 128))
    assert (2 * t) % MOE_BLOCK == 0
    n_blk = 2 * t // MOE_BLOCK + n_exp
    cos, sin_signed = _rope_tables(n_lat, n_ctx)
    mod = _ada(c, c_ctx, w_ada, b_ada)
    xs = jnp.concatenate([ctx[0], x[0]], axis=0)
    for l in range(depth):
        last = l == depth - 1
        p = _inproj(xs, mod[l], norm1_g[l][None], w_in, l, cos, sin_signed,
                    q_norm_g[l][None], k_norm_g[l][None], n_ctx)
        og = _gattn(p, q_norm_g[l], n_ctx, mix)
        ow = _wattn(p, sink[l], n_ctx, mix)
        z = _merge(p, og, ow, pool_w, pool_scale[l][None], w_branch, l, n_ctx, d)
        xs = _outproj(z, w_out, l, xs, mod[l], n_ctx)
        h2, ints, ws, cnt = _router(xs, mod[l], norm2_g[l][None], router_w, router_b, n_ctx)
        plan = _moe_plan(cnt[:, 0].astype(jnp.int32), n_blk)
        dest_blocks = _dest_blocks(ints, plan["pad_start"], tok)
        xs_sorted = _dispatch(h2, dest_blocks, n_blk * MOE_BLOCK, tok)
        y_sorted = _ffn(xs_sorted, plan, w_gate, w_up, w_down, l)
        xs = _combine(y_sorted, dest_blocks, xs, ws, mod[l], final_g[None], n_ctx, tok, last)
    return xs[None]
```

```python
import functools

import jax
import jax.numpy as jnp
from jax import lax
from jax.experimental import pallas as pl
from jax.experimental.pallas import tpu as pltpu

HEAD_DIM = 128
LANES = 128
BF16_ROWS = 16
KV_HEADS = 2
GRID_W = 64
WINDOW = 128
POOL_WINDOWS = (2, 4, 8, 16)
POOL_HALO = 16
N_GROUPS = 4
ROPE_THETA = 10000.0
RMS_EPS = 1e-6
NEG_INF = -1e30
ATTN_SCALE = HEAD_DIM ** -0.5
LOG2E = 1.4426950408889634
SOFTMAX_SAFE_LOG2 = 50.0
MOE_BLOCK = 256
DMA_UNROLL = 8
VMEM_CAP = 60 * 1024 * 1024

BF16 = jnp.bfloat16
F32 = jnp.float32


def _pick(n, prefs):
    for p in prefs:
        if n % p == 0:
            return p
    raise ValueError(f"no tile in {prefs} divides {n}")


def _largest_tile(n, cap, mult):
    return max(k for k in range(mult, min(n, cap) + 1, mult) if n % k == 0)


def _params(n_axes, vmem_bytes):
    limit = int(min(max(vmem_bytes * 5 // 4 + (4 << 20), 16 << 20), VMEM_CAP))
    return pltpu.CompilerParams(dimension_semantics=("arbitrary",) * n_axes, vmem_limit_bytes=limit)


def _sigmoid(x):
    return 1.0 / (1.0 + jnp.exp(-x))


def _sigmoid_tanh(x):
    return 0.5 * jnp.tanh(0.5 * x) + 0.5


def _dot(a, b):
    return jnp.dot(a, b, preferred_element_type=F32)


def _dot_nt(a, b):
    return lax.dot_general(a, b, (((1,), (1,)), ((), ())), preferred_element_type=F32)


def _row_select(row0, n_rows, n_ctx, mod_ref):
    rows = row0 + lax.broadcasted_iota(jnp.int32, (n_rows, 1), 0)
    return jnp.where(rows < n_ctx, mod_ref[1:2, :], mod_ref[0:1, :])


def _norm_modulate(x, g_ref, shift, scale):
    ms = jnp.mean(x * x, axis=-1, keepdims=True)
    y = x * lax.rsqrt(ms + RMS_EPS) * g_ref[...]
    return y * (1.0 + scale) + shift


def _ada_kernel(cb_ref, w_ref, b_ref, o_ref, *, tn):
    s = cb_ref[...]
    s = s * _sigmoid(s)
    outs = []
    for r in range(2):
        cols = [jnp.sum(w_ref[:, c * LANES:(c + 1) * LANES] * s[r], axis=0, keepdims=True)
                for c in range(tn // LANES)]
        outs.append(jnp.concatenate(cols, axis=1) + b_ref[...])
    o_ref[...] = jnp.concatenate(outs + [jnp.zeros((6, tn), F32)], axis=0)


def _ada(c, c_ctx, w_ada, b_ada):
    depth, d, w6 = w_ada.shape
    tn = _pick(w6, (1024, 512, 256, 128))
    cb = jnp.broadcast_to(jnp.stack([c[0], c_ctx])[:, :, None], (2, d, LANES))
    vmem = 2 * d * tn * 4 + 2 * d * LANES * 4 * 2 + d * LANES * 4 * 4
    return pl.pallas_call(
        functools.partial(_ada_kernel, tn=tn),
        grid=(depth, w6 // tn),
        in_specs=[pl.BlockSpec((2, d, LANES), lambda l, j: (0, 0, 0)),
                  pl.BlockSpec((None, d, tn), lambda l, j: (l, 0, j)),
                  pl.BlockSpec((None, 1, tn), lambda l, j: (l, 0, j))],
        out_specs=pl.BlockSpec((None, 8, tn), lambda l, j: (l, 0, j)),
        out_shape=jax.ShapeDtypeStruct((depth, 8, w6), F32),
        compiler_params=_params(2, vmem),
        name="ada",
    )(cb, w_ada, b_ada.reshape(depth, 1, w6))


def _rope(x, cos, sin_signed, first_half):
    rot = jnp.where(first_half, pltpu.roll(x, 3 * HEAD_DIM // 4, axis=1), pltpu.roll(x, HEAD_DIM // 4, axis=1))
    return x * cos + rot * sin_signed


def _head_rms(x, g_ref):
    ms = jnp.mean(x * x, axis=-1, keepdims=True)
    return x * lax.rsqrt(ms + RMS_EPS) * g_ref[...]


def _inproj_kernel(x_ref, shift_ref, scale_ref, g_ref, w_ref, cos_ref, sin_ref, qg_ref, kg_ref,
                   o_ref, h_sc, *, tm, tn, n_ctx, signatures):
    i = pl.program_id(0)
    j = pl.program_id(1)

    def normed():
        x = x_ref[...]
        return x * lax.rsqrt(jnp.mean(x * x, axis=-1, keepdims=True) + RMS_EPS)

    has_ctx_rows = i * tm < n_ctx

    @pl.when((j == 0) & has_ctx_rows)
    def _():
        gain = g_ref[...] * (1.0 + scale_ref[0:2, :])
        rows = i * tm + lax.broadcasted_iota(jnp.int32, (tm, 1), 0)
        is_ctx = rows < n_ctx
        h = normed() * jnp.where(is_ctx, gain[1:2], gain[0:1]) + jnp.where(is_ctx, shift_ref[1:2, :], shift_ref[0:1, :])
        h_sc[...] = h.astype(BF16)

    @pl.when((j == 0) & jnp.logical_not(has_ctx_rows))
    def _():
        h = normed() * (g_ref[...] * (1.0 + scale_ref[0:1, :])) + shift_ref[0:1, :]
        h_sc[...] = h.astype(BF16)

    acc = _dot(h_sc[...], w_ref[...].astype(BF16))

    lane = lax.broadcasted_iota(jnp.int32, (1, HEAD_DIM), 1)
    first_half = (lane % (HEAD_DIM // 2)) < (HEAD_DIM // 4)

    for sig, js in signatures:
        cond = functools.reduce(jnp.logical_or, [j == jj for jj in js])

        @pl.when(cond)
        def _(sig=sig):
            if all(t == "plain" for t in sig):
                o_ref[...] = acc.astype(BF16)
                return
            for bi, typ in enumerate(sig):
                a = acc[:, bi * LANES:(bi + 1) * LANES]
                if typ != "plain":
                    if typ in ("qnr", "knr"):
                        a = _head_rms(a, qg_ref if typ == "qnr" else kg_ref)
                    a = _rope(a, cos_ref[...], sin_ref[...], first_half)
                    if typ in ("qnr", "qr"):
                        a = a * (ATTN_SCALE * LOG2E)
                o_ref[:, bi * LANES:(bi + 1) * LANES] = a.astype(BF16)


def _col_types(mix, d):
    hq = mix // HEAD_DIM
    types = (["plain"] * hq + ["qnr"] * hq + ["knr"] * KV_HEADS + ["plain"] * KV_HEADS
             + ["qr"] * hq + ["kr"] * KV_HEADS + ["plain"] * KV_HEADS + ["plain"] * (3 * d // LANES))
    return types


def _inproj(xs, mod_l, g1, w_in, layer, cos, sin_signed, qg, kg, n_ctx):
    t, d = xs.shape
    in_w = w_in.shape[-1]
    mix = d // 2
    tm = _pick(t, (768, 512, 256, 128))
    tn = _pick(in_w, (1024, 512, 256))
    types = _col_types(mix, d)
    assert len(types) * LANES == in_w
    per = tn // LANES
    sigs = {}
    for jj in range(in_w // tn):
        sigs.setdefault(tuple(types[jj * per:(jj + 1) * per]), []).append(jj)
    signatures = tuple((s, tuple(js)) for s, js in sigs.items())
    vmem = (2 * tm * d * 4 + 2 * d * tn * 4 + tm * d * 2 + 2 * tm * tn * 2 + tm * tn * 4 * 2 + d * tn * 2
            + 4 * tm * LANES * 4)
    return pl.pallas_call(
        functools.partial(_inproj_kernel, tm=tm, tn=tn, n_ctx=n_ctx, signatures=signatures),
        grid=(t // tm, in_w // tn),
        in_specs=[pl.BlockSpec((tm, d), lambda i, j: (i, 0)),
                  pl.BlockSpec((8, d), lambda i, j: (0, 0)),
                  pl.BlockSpec((8, d), lambda i, j: (0, 1)),
                  pl.BlockSpec((1, d), lambda i, j: (0, 0)),
                  pl.BlockSpec((None, d, tn), lambda i, j: (layer, 0, j)),
                  pl.BlockSpec((tm, HEAD_DIM), lambda i, j: (i, 0)),
                  pl.BlockSpec((tm, HEAD_DIM), lambda i, j: (i, 0)),
                  pl.BlockSpec((1, HEAD_DIM), lambda i, j: (0, 0)),
                  pl.BlockSpec((1, HEAD_DIM), lambda i, j: (0, 0))],
        out_specs=pl.BlockSpec((tm, tn), lambda i, j: (i, j)),
        out_shape=jax.ShapeDtypeStruct((t, in_w), BF16),
        scratch_shapes=[pltpu.VMEM((tm, d), BF16)],
        compiler_params=_params(2, vmem),
        name="inproj",
    )(xs, mod_l, mod_l, g1, w_in, cos, sin_signed, qg, kg)


def _lane_repeat(x, n):
    return jnp.concatenate([x] * n, axis=1)


def _stack_heads(q_ref, g):
    return jnp.concatenate([q_ref[:, h * HEAD_DIM:(h + 1) * HEAD_DIM] for h in range(g)], axis=0)


def _unstack_store(o_ref, out, g, tq):
    for h in range(g):
        o_ref[:, h * HEAD_DIM:(h + 1) * HEAD_DIM] = out[h * tq:(h + 1) * tq, :].astype(o_ref.dtype)


def _gattn_kernel(qmax_ref, q_ref, k_ref, v_ref, o_ref, vx_sc, kmax_sc, flag_sc, m_sc, acc_sc, *, tq, tk, g, n_ctx, n_lat):
    qi = pl.program_id(1)
    rows = g * tq

    @pl.when(qi == 0)
    def _():
        vx_sc[:, 0:HEAD_DIM] = v_ref[...]
        vx_sc[:, HEAD_DIM:] = jnp.ones((vx_sc.shape[0], HEAD_DIM), BF16)
        kf = k_ref[...].astype(F32)
        k2 = jnp.max(jnp.sum(kf * kf, axis=-1, keepdims=True), axis=0, keepdims=True)
        kmax = jnp.broadcast_to(jnp.sqrt(k2), kmax_sc.shape)
        kmax_sc[...] = kmax
        flag_sc[0] = (qmax_ref[0] * jnp.max(kmax) <= SOFTMAX_SAFE_LOG2).astype(jnp.int32)

    bounded = flag_sc[0] == 1
    ctx_only = qi * tq < n_ctx
    ctx_chunk = [(0, n_ctx)]
    all_chunks = ctx_chunk + [(n_ctx + c * tk, tk) for c in range(n_lat // tk)]

    def finish():
        acc = acc_sc[...]
        _unstack_store(o_ref, acc[:, :HEAD_DIM] / acc[:, HEAD_DIM:], g, tq)

    def attend_bounded(chunks):
        q = _stack_heads(q_ref, g)
        qf = q.astype(F32)
        bound = jnp.sqrt(jnp.sum(qf * qf, axis=-1, keepdims=True)) * kmax_sc[0:1, :]
        for n, (lo, size) in enumerate(chunks):
            s = _dot_nt(q, k_ref[lo:lo + size, :])
            p = jnp.exp2(s - _lane_repeat(bound, size // LANES))
            pv = _dot(p.astype(BF16), vx_sc[lo:lo + size, :])
            acc_sc[...] = pv if n == 0 else acc_sc[...] + pv
        finish()

    def attend_online(chunks):
        q = _stack_heads(q_ref, g)
        for n, (lo, size) in enumerate(chunks):
            s = _dot_nt(q, k_ref[lo:lo + size, :])
            mx = jnp.max(s, axis=-1, keepdims=True)
            if n == 0:
                m_new = jnp.broadcast_to(mx, (rows, LANES))
            else:
                m_prev = m_sc[...]
                m_new = jnp.maximum(m_prev, mx)
            p = jnp.exp2(s - _lane_repeat(m_new, size // LANES))
            pv = _dot(p.astype(BF16), vx_sc[lo:lo + size, :])
            if n == 0:
                acc_sc[...] = pv
            else:
                acc_sc[...] = _lane_repeat(jnp.exp2(m_prev - m_new), 2) * acc_sc[...] + pv
            m_sc[...] = m_new
        finish()

    for use_bound, attend in ((True, attend_bounded), (False, attend_online)):
        path = bounded if use_bound else jnp.logical_not(bounded)
        pl.when(path & ctx_only)(functools.partial(attend, ctx_chunk))
        pl.when(path & jnp.logical_not(ctx_only))(functools.partial(attend, all_chunks))


def _gattn(p, q_gain, n_ctx, mix):
    t = p.shape[0]
    qmax = (1.02 * HEAD_DIM ** 0.5 * ATTN_SCALE * LOG2E) * jnp.max(jnp.abs(q_gain)).reshape(1)
    hq = mix // HEAD_DIM
    g = hq // KV_HEADS
    n_lat = t - n_ctx
    tq = _pick(n_ctx, (256, 128))
    assert t % tq == 0
    tk = _pick(n_lat, (2048, 1024, 512, 256, 128))
    gw = g * HEAD_DIM
    off_q = mix // gw
    off_k = (mix + hq * HEAD_DIM) // HEAD_DIM
    off_v = off_k + KV_HEADS
    rows = g * tq
    vmem = (2 * tq * gw * 2 * 2 + 2 * 2 * t * HEAD_DIM * 2 + t * 2 * HEAD_DIM * 2 + rows * LANES * 4 * 4
            + rows * max(tk, n_ctx) * 4 * 3)
    return pl.pallas_call(
        functools.partial(_gattn_kernel, tq=tq, tk=tk, g=g, n_ctx=n_ctx, n_lat=n_lat),
        grid=(KV_HEADS, t // tq),
        in_specs=[pl.BlockSpec(memory_space=pltpu.SMEM),
                  pl.BlockSpec((tq, gw), lambda h, i: (i, off_q + h)),
                  pl.BlockSpec((t, HEAD_DIM), lambda h, i: (0, off_k + h)),
                  pl.BlockSpec((t, HEAD_DIM), lambda h, i: (0, off_v + h))],
        out_specs=pl.BlockSpec((tq, gw), lambda h, i: (i, h)),
        out_shape=jax.ShapeDtypeStruct((t, mix), BF16),
        scratch_shapes=[pltpu.VMEM((t, 2 * HEAD_DIM), BF16), pltpu.VMEM((8, LANES), F32), pltpu.SMEM((1,), jnp.int32),
                        pltpu.VMEM((rows, LANES), F32), pltpu.VMEM((rows, 2 * HEAD_DIM), F32)],
        compiler_params=_params(2, vmem),
        name="gattn",
    )(qmax, p, p, p)


def _wattn_kernel(*refs, tq, n_sub, g, n_ctx, span, t):
    q_refs, k_refs, v_refs = refs[0:KV_HEADS], refs[KV_HEADS:2 * KV_HEADS], refs[2 * KV_HEADS:3 * KV_HEADS]
    sink_ref, o_ref, vx_sc = refs[3 * KV_HEADS:]
    qi = pl.program_id(0)

    @pl.when(qi == 0)
    def _():
        for h in range(KV_HEADS):
            vx_sc[h, :, 0:HEAD_DIM] = v_refs[h][...]
            vx_sc[h, :, HEAD_DIM:] = jnp.ones((t, HEAD_DIM), BF16)

    for s in range(n_sub):
        for h in range(KV_HEADS):
            _wattn_head(qi * n_sub + s, q_refs[h].at[s * tq:(s + 1) * tq], k_refs[h], vx_sc.at[h], sink_ref.at[h],
                        o_ref.at[s * tq:(s + 1) * tq, h * g * HEAD_DIM:(h + 1) * g * HEAD_DIM],
                        tq=tq, g=g, n_ctx=n_ctx, span=span, t=t)


def _wattn_head(qi, q_ref, k_ref, vx_sc, sink_ref, o_ref, *, tq, g, n_ctx, span, t):
    q = _stack_heads(q_ref, g)
    rows = q.shape[0]
    start = pl.multiple_of(jnp.clip(qi * tq - WINDOW, 0, t - span), LANES)

    far = t + 4 * WINDOW
    qrow = qi * tq + lax.broadcasted_iota(jnp.int32, (tq, 1), 0)
    qrow = jnp.concatenate([jnp.where(qrow >= n_ctx, qrow, -far)] * g, axis=0)
    krow = start + lax.broadcasted_iota(jnp.int32, (1, span), 1)
    krow = jnp.where(krow >= n_ctx, krow, far)
    keep = jnp.abs(krow - qrow) <= WINDOW

    s_c = _dot_nt(q, k_ref[0:n_ctx, :])
    s_w = jnp.where(keep, _dot_nt(q, k_ref[pl.ds(start, span), :]), NEG_INF)
    sink = sink_ref[...] * LOG2E
    mx = jnp.maximum(jnp.maximum(jnp.max(s_c, axis=-1, keepdims=True), jnp.max(s_w, axis=-1, keepdims=True)), sink)
    m = jnp.broadcast_to(mx, (rows, LANES))
    p_c = jnp.exp2(s_c - _lane_repeat(m, n_ctx // LANES))
    p_w = jnp.exp2(s_w - _lane_repeat(m, span // LANES))
    acc = _dot(p_c.astype(BF16), vx_sc[0:n_ctx, :]) + _dot(p_w.astype(BF16), vx_sc[pl.ds(start, span), :])
    denom = acc[:, HEAD_DIM:] + jnp.exp2(sink - m)
    _unstack_store(o_ref, acc[:, :HEAD_DIM] / denom, g, tq)


def _wattn(p, sink_l, n_ctx, mix):
    t = p.shape[0]
    hq = mix // HEAD_DIM
    g = hq // KV_HEADS
    tq = _pick(n_ctx, (256, 128))
    span = tq + 2 * WINDOW
    assert t % tq == 0 and t >= span
    gw = g * HEAD_DIM
    base = mix + hq * HEAD_DIM + 2 * KV_HEADS * HEAD_DIM
    off_q = base // gw
    off_k = (base + hq * HEAD_DIM) // HEAD_DIM
    off_v = off_k + KV_HEADS
    rows = g * tq
    sink_rows = jnp.repeat(sink_l.reshape(KV_HEADS, g), tq, axis=1).reshape(KV_HEADS, rows, 1)
    n_sub = 3 if (t // tq) % 3 == 0 else 1
    vmem = KV_HEADS * (2 * n_sub * tq * gw * 2 * 2 + 2 * 2 * t * HEAD_DIM * 2 + 2 * rows * LANES * 4
                       + n_sub * rows * (span + n_ctx) * 4 * 3 + t * 2 * HEAD_DIM * 2)
    heads = range(KV_HEADS)
    return pl.pallas_call(
        functools.partial(_wattn_kernel, tq=tq, n_sub=n_sub, g=g, n_ctx=n_ctx, span=span, t=t),
        grid=(t // (tq * n_sub),),
        in_specs=([pl.BlockSpec((tq * n_sub, gw), lambda i, h=h: (i, off_q + h)) for h in heads]
                  + [pl.BlockSpec((t, HEAD_DIM), lambda i, h=h: (0, off_k + h)) for h in heads]
                  + [pl.BlockSpec((t, HEAD_DIM), lambda i, h=h: (0, off_v + h)) for h in heads]
                  + [pl.BlockSpec((KV_HEADS, rows, 1), lambda i: (0, 0, 0))]),
        out_specs=pl.BlockSpec((tq * n_sub, mix), lambda i: (i, 0)),
        out_shape=jax.ShapeDtypeStruct((t, mix), BF16),
        scratch_shapes=[pltpu.VMEM((KV_HEADS, t, 2 * HEAD_DIM), BF16)],
        compiler_params=_params(1, vmem),
        name="wattn",
    )(*([p] * (3 * KV_HEADS)), sink_rows)


def _merge_kernel(u_ref, up_ref, un_ref, og_ref, ow_ref, g0_ref, g1_ref, g2_ref, pw_ref, ps_ref, wb_ref,
                  z_ref, ext_sc, pool_sc, *, tm, n_ctx, t, mix):
    i = pl.program_id(0)
    j = pl.program_id(1)
    gw = mix // len(POOL_WINDOWS)

    def pool(clipped):
        ext_sc[0:POOL_HALO, :] = up_ref[...].astype(F32)
        ext_sc[POOL_HALO:POOL_HALO + tm, :] = u_ref[...].astype(F32)
        ext_sc[POOL_HALO + tm:, :] = un_ref[...].astype(F32)
        r = i * tm + lax.broadcasted_iota(jnp.int32, (tm, 1), 0)
        r_ctx = r < n_ctx
        for gi, w in enumerate(POOL_WINDOWS):
            c0, c1 = gi * gw, (gi + 1) * gw
            tot = jnp.zeros((tm, gw), F32)
            cnt = jnp.zeros((tm, 1), F32)
            for off in range(-((w - 1) // 2), w // 2 + 1):
                part = ext_sc[POOL_HALO + off:POOL_HALO + off + tm, c0:c1]
                if clipped:
                    rr = r + off
                    ok = (rr >= 0) & (rr < t) & ((rr < n_ctx) == r_ctx)
                    part = jnp.where(ok, part, 0.0)
                    cnt = cnt + ok.astype(F32)
                tot = tot + part
            mean = tot / cnt if clipped else tot * (1.0 / w)
            dlt = mean - ext_sc[POOL_HALO:POOL_HALO + tm, c0:c1]
            y = _dot(dlt.astype(BF16), pw_ref[gi].astype(BF16)) * ps_ref[:, c0:c1]
            pool_sc[:, c0:c1] = y.astype(BF16)

    reach = max(POOL_WINDOWS) // 2
    interior = (i * tm - reach >= n_ctx) & ((i + 1) * tm + reach <= t)
    pl.when((j == 0) & interior)(functools.partial(pool, False))
    pl.when((j == 0) & jnp.logical_not(interior))(functools.partial(pool, True))

    def gate(g_ref):
        return _sigmoid_tanh(g_ref[...].astype(F32))

    z = gate(g0_ref) * _dot(pool_sc[...], wb_ref[0].astype(BF16))
    z = z + gate(g1_ref) * _dot(og_ref[...], wb_ref[1].astype(BF16))
    z = z + gate(g2_ref) * _dot(ow_ref[...], wb_ref[2].astype(BF16))
    z_ref[...] = z.astype(BF16)


def _merge(p, og, ow, pool_w, pool_scale, w_branch, layer, n_ctx, d):
    t = p.shape[0]
    mix = d // 2
    tm = _largest_tile(t, 1100, POOL_HALO)
    tn = _pick(d, (512, 256, 128))
    hb = tm // POOL_HALO
    n_hb = t // POOL_HALO
    gate0 = (p.shape[1] - 3 * d) // tn
    gs = pool_w.shape[-1]
    vmem = (2 * 3 * tm * mix * 2 + 2 * 3 * tm * tn * 2 + 2 * 3 * mix * tn * 4 + 3 * mix * tn * 2
            + (tm + 2 * POOL_HALO) * mix * 4 + tm * mix * 2 + 2 * tm * tn * 2 + 4 * tm * tn * 4
            + 2 * len(POOL_WINDOWS) * gs * gs * 4 + 6 * tm * gs * 4)
    return pl.pallas_call(
        functools.partial(_merge_kernel, tm=tm, n_ctx=n_ctx, t=t, mix=mix),
        grid=(t // tm, d // tn),
        in_specs=[pl.BlockSpec((tm, mix), lambda i, j: (i, 0)),
                  pl.BlockSpec((POOL_HALO, mix), lambda i, j: (jnp.maximum(i * hb - 1, 0), 0)),
                  pl.BlockSpec((POOL_HALO, mix), lambda i, j: (jnp.minimum((i + 1) * hb, n_hb - 1), 0)),
                  pl.BlockSpec((tm, mix), lambda i, j: (i, 0)),
                  pl.BlockSpec((tm, mix), lambda i, j: (i, 0)),
                  pl.BlockSpec((tm, tn), lambda i, j: (i, gate0 + j)),
                  pl.BlockSpec((tm, tn), lambda i, j: (i, gate0 + d // tn + j)),
                  pl.BlockSpec((tm, tn), lambda i, j: (i, gate0 + 2 * (d // tn) + j)),
                  pl.BlockSpec((None, len(POOL_WINDOWS), gs, gs), lambda i, j: (layer, 0, 0, 0)),
                  pl.BlockSpec((1, mix), lambda i, j: (0, 0)),
                  pl.BlockSpec((None, 3, mix, tn), lambda i, j: (layer, 0, 0, j))],
        out_specs=pl.BlockSpec((tm, tn), lambda i, j: (i, j)),
        out_shape=jax.ShapeDtypeStruct((t, d), BF16),
        scratch_shapes=[pltpu.VMEM((tm + 2 * POOL_HALO, mix), F32), pltpu.VMEM((tm, mix), BF16)],
        compiler_params=_params(2, vmem),
        name="merge",
    )(p, p, p, og, ow, p, p, p, pool_w, pool_scale, w_branch)


W_CHUNK = 256


def _route(h, rwt_ref, rb_ref, ints_ref, ws_ref, cnt_ref, carry_sc, *, tm, n_exp):
    scores = _sigmoid(_dot_nt(rwt_ref[...].astype(BF16), h.astype(BF16)))
    sel = scores + rb_ref[...]
    per = n_exp // N_GROUPS
    sub = lax.broadcasted_iota(jnp.int32, (per, tm), 0)

    def top2(v):
        m1 = jnp.max(v, axis=0, keepdims=True)
        i1 = jnp.min(jnp.where(v == m1, sub, per), axis=0, keepdims=True)
        rest = jnp.where(sub == i1, -jnp.inf, v)
        m2 = jnp.max(rest, axis=0, keepdims=True)
        i2 = jnp.min(jnp.where(rest == m2, sub, per), axis=0, keepdims=True)
        return m1 + m2, i1, i2

    tops = [top2(sel[gi * per:(gi + 1) * per, :]) for gi in range(N_GROUPS)]
    best, l1, l2 = tops[0]
    grp = jnp.zeros((1, tm), jnp.int32)
    for gi in range(1, N_GROUPS):
        gs, a1, a2 = tops[gi]
        better = gs > best
        best = jnp.where(better, gs, best)
        grp = jnp.where(better, gi, grp)
        l1 = jnp.where(better, a1, l1)
        l2 = jnp.where(better, a2, l2)

    hot1 = [(grp == gi) & (sub == l1) for gi in range(N_GROUPS)]
    hot2 = [(grp == gi) & (sub == l2) for gi in range(N_GROUPS)]
    assign = jnp.concatenate([(a | b).astype(F32) for a, b in zip(hot1, hot2)], axis=0)

    before = (lax.broadcasted_iota(jnp.int32, (tm, tm), 0) < lax.broadcasted_iota(jnp.int32, (tm, tm), 1))
    pos = _dot(assign.astype(BF16), before.astype(F32).astype(BF16)) + carry_sc[...]

    def pick(hots, val):
        return sum(jnp.sum(jnp.where(hots[gi], val[gi * per:(gi + 1) * per, :], 0.0), axis=0, keepdims=True)
                   for gi in range(N_GROUPS))

    s1 = pick(hot1, scores)
    s2 = pick(hot2, scores)
    r1 = pick(hot1, pos)
    r2 = pick(hot2, pos)
    tot = s1 + s2
    ints_ref[0:1, :] = grp * per + l1
    ints_ref[1:2, :] = grp * per + l2
    ints_ref[2:3, :] = r1.astype(jnp.int32)
    ints_ref[3:4, :] = r2.astype(jnp.int32)
    ints_ref[4:8, :] = jnp.zeros((4, tm), jnp.int32)
    ws_ref[0:1, :] = s1 / tot
    ws_ref[1:2, :] = s2 / tot
    ws_ref[2:8, :] = jnp.zeros((6, tm), F32)
    carry_sc[...] = carry_sc[...] + jnp.sum(assign, axis=1, keepdims=True)
    cnt_ref[...] = jnp.broadcast_to(carry_sc[...], cnt_ref.shape)


def _outproj_router_kernel(z_ref, w_hbm, xs_ref, gate_ref, shift_ref, scale_ref, g_ref, rwt_ref, rb_ref,
                           o_ref, h_ref, ints_ref, ws_ref, cnt_ref, w_sc, stage, sem, carry_sc,
                           *, tm, n_ctx, n_exp, layer):
    i = pl.program_id(0)
    d = w_sc.shape[0]

    @pl.when(i == 0)
    def _():
        carry_sc[...] = jnp.zeros(carry_sc.shape, F32)
        n_chunks = d // W_CHUNK

        def chunk(c):
            return pltpu.make_async_copy(w_hbm.at[layer, :, c * W_CHUNK:(c + 1) * W_CHUNK], stage.at[c % 2], sem.at[c % 2])

        chunk(0).start()
        for c in range(n_chunks):
            if c + 1 < n_chunks:
                chunk(c + 1).start()
            chunk(c).wait()
            w_sc[:, c * W_CHUNK:(c + 1) * W_CHUNK] = stage[c % 2].astype(BF16)

    gate = _row_select(i * tm, tm, n_ctx, gate_ref)
    x = xs_ref[...] + gate * _dot(z_ref[...], w_sc[...])
    o_ref[...] = x
    shift = _row_select(i * tm, tm, n_ctx, shift_ref)
    scale = _row_select(i * tm, tm, n_ctx, scale_ref)
    h = _norm_modulate(x, g_ref, shift, scale)
    h_ref[...] = h
    _route(h, rwt_ref, rb_ref, ints_ref, ws_ref, cnt_ref, carry_sc, tm=tm, n_exp=n_exp)


def _outproj_router(z, w_out, layer, xs, mod_l, g2, router_w, router_b, n_ctx):
    t, d = xs.shape
    n_exp = router_w.shape[1]
    tm = _largest_tile(t, 400, LANES)
    vmem = (d * d * 2 + 2 * d * W_CHUNK * 4 + 2 * tm * d * 2 + 6 * tm * d * 4 + 6 * tm * d * 4 + 3 * tm * tm * 4
            + 2 * n_exp * d * 4)
    row = lambda i: (i, 0)
    return pl.pallas_call(
        functools.partial(_outproj_router_kernel, tm=tm, n_ctx=n_ctx, n_exp=n_exp, layer=layer),
        grid=(t // tm,),
        in_specs=[pl.BlockSpec((tm, d), row),
                  pl.BlockSpec(memory_space=pl.ANY),
                  pl.BlockSpec((tm, d), row),
                  pl.BlockSpec((8, d), lambda i: (0, 2)),
                  pl.BlockSpec((8, d), lambda i: (0, 3)),
                  pl.BlockSpec((8, d), lambda i: (0, 4)),
                  pl.BlockSpec((1, d), lambda i: (0, 0)),
                  pl.BlockSpec((n_exp, d), lambda i: (0, 0)),
                  pl.BlockSpec((n_exp, 1), lambda i: (0, 0))],
        out_specs=[pl.BlockSpec((tm, d), row),
                   pl.BlockSpec((tm, d), row),
                   pl.BlockSpec((8, tm), lambda i: (0, i)),
                   pl.BlockSpec((8, tm), lambda i: (0, i)),
                   pl.BlockSpec((n_exp, LANES), lambda i: (0, 0))],
        out_shape=[jax.ShapeDtypeStruct((t, d), F32),
                   jax.ShapeDtypeStruct((t, d), F32),
                   jax.ShapeDtypeStruct((8, t), jnp.int32),
                   jax.ShapeDtypeStruct((8, t), F32),
                   jax.ShapeDtypeStruct((n_exp, LANES), F32)],
        scratch_shapes=[pltpu.VMEM((d, d), BF16), pltpu.VMEM((2, d, W_CHUNK), F32), pltpu.SemaphoreType.DMA((2,)),
                        pltpu.VMEM((n_exp, 1), F32)],
        compiler_params=_params(1, vmem),
        name="outproj_router",
    )(z, w_out, xs, mod_l, mod_l, mod_l, g2, router_w.T, router_b.reshape(n_exp, 1))


def _row_copy(src_ref, src_row, dst_ref, dst_row, sem):
    return pltpu.make_async_copy(src_ref.at[pl.ds(src_row, 1)], dst_ref.at[pl.ds(dst_row, 1)], sem)


def _dispatch_kernel(dst_ref, h_ref, xs_hbm, sem, *, tm):
    def issue(r, carry):
        for k in range(2):
            _row_copy(h_ref, r, xs_hbm, dst_ref[0, 0, k * tm + r], sem).start(priority=k)
        return carry

    lax.fori_loop(0, tm, issue, 0, unroll=DMA_UNROLL)
    for k in range(2):
        pltpu.make_async_copy(h_ref, xs_hbm.at[pl.ds(0, tm)], sem).wait()


def _dest_blocks(ints, pad_start, tm):
    t = ints.shape[1]
    experts = jnp.arange(pad_start.shape[0], dtype=jnp.int32)
    start = jnp.sum(jnp.where(ints[0:2, :, None] == experts, pad_start, 0), axis=-1)
    dest = start + ints[2:4]
    return dest.reshape(2, t // tm, tm).transpose(1, 0, 2).reshape(t // tm, 1, 2 * tm)


def _dispatch(h2, dest_blocks, n_rows, tm):
    t, d = h2.shape
    return pl.pallas_call(
        functools.partial(_dispatch_kernel, tm=tm),
        grid=(t // tm,),
        in_specs=[pl.BlockSpec((1, 1, 2 * tm), lambda i: (i, 0, 0), memory_space=pltpu.SMEM),
                  pl.BlockSpec((tm, d), lambda i: (i, 0))],
        out_specs=pl.BlockSpec(memory_space=pl.ANY),
        out_shape=jax.ShapeDtypeStruct((n_rows, d), F32),
        scratch_shapes=[pltpu.SemaphoreType.DMA(())],
        compiler_params=_params(1, 2 * tm * d * 4),
        name="dispatch",
    )(dest_blocks, h2)


def _ffn_kernel(blk_valid_ref, blk_seg_ref, blk_first_ref, seg_e_ref, cnt_ref, x_ref, wg_hbm, wu_hbm, wd_hbm, y_ref,
                wg_st, wu_st, wd_st, wg_sc, wu_sc, wd_sc, sem_g, sem_u, sem_d, *, layer):
    b = pl.program_id(0)
    n_seg = cnt_ref[1]

    def copies(k):
        e = seg_e_ref[k]
        return [pltpu.make_async_copy(w.at[layer, e], st, sm)
                for w, st, sm in ((wg_hbm, wg_st, sem_g), (wu_hbm, wu_st, sem_u), (wd_hbm, wd_st, sem_d))]

    def x_block():
        rows = lax.broadcasted_iota(jnp.int32, (x_ref.shape[0], 1), 0)
        return jnp.where(rows < blk_valid_ref[b], x_ref[...], 0.0).astype(BF16)

    def finish(gte, up):
        hidden = (gte * _sigmoid(gte) * up).astype(BF16)
        y_ref[...] = _dot(hidden, wd_sc[...])

    @pl.when(b < cnt_ref[0])
    def _():
        @pl.when(b == 0)
        def _():
            for cp in copies(0):
                cp.start()

        @pl.when(blk_first_ref[b] == 1)
        def _():
            k = blk_seg_ref[b]
            for cp, nxt, st, dst in zip(copies(k), copies(k + 1), (wg_st, wu_st, wd_st), (wg_sc, wu_sc, wd_sc)):
                cp.wait()
                dst[...] = st[...].astype(BF16)
                pl.when(k + 1 < n_seg)(nxt.start)

        x = x_block()
        finish(_dot(x, wg_sc[...]), _dot(x, wu_sc[...]))


def _ffn(xs_sorted, plan, w_gate, w_up, w_down, layer):
    n_rows, d = xs_sorted.shape
    ff = w_gate.shape[-1]
    blk = MOE_BLOCK
    n_blk = n_rows // blk
    prefetch = (plan["blk_valid"], plan["blk_seg"], plan["blk_first"], plan["seg_e"], plan["cnt"])
    row_map = lambda b, v, s, f, e, c: (jnp.minimum(b, c[0] - 1), 0)
    hbm = pl.BlockSpec(memory_space=pl.ANY)
    dma = pltpu.SemaphoreType.DMA(())
    grid_spec = pltpu.PrefetchScalarGridSpec(
        num_scalar_prefetch=len(prefetch),
        grid=(n_blk,),
        in_specs=[pl.BlockSpec((blk, d), row_map), hbm, hbm, hbm],
        out_specs=pl.BlockSpec((blk, d), row_map),
        scratch_shapes=[pltpu.VMEM((d, ff), F32), pltpu.VMEM((d, ff), F32), pltpu.VMEM((ff, d), F32),
                        pltpu.VMEM((d, ff), BF16), pltpu.VMEM((d, ff), BF16), pltpu.VMEM((ff, d), BF16),
                        dma, dma, dma],
    )
    vmem = 3 * d * ff * (4 + 2) + 4 * blk * d * 4 + blk * d * 2 + 3 * blk * ff * 4 + blk * d * 4
    return pl.pallas_call(
        functools.partial(_ffn_kernel, layer=layer),
        grid_spec=grid_spec,
        out_shape=jax.ShapeDtypeStruct((n_rows, d), F32),
        compiler_params=pltpu.CompilerParams(dimension_semantics=("arbitrary",), vmem_limit_bytes=min(vmem + (6 << 20), VMEM_CAP)),
        name="ffn",
    )(*prefetch, xs_sorted, w_gate, w_up, w_down)


def _combine_kernel(src_ref, nxt_ref, y_hbm, xs_ref, ws_ref, gate_ref, fg_ref, o_ref, ybuf, sem, *, tm, n_ctx, final):
    i = pl.program_id(0)
    slot = i % 2

    def gather(idx_ref, into):
        def issue(r, carry):
            for k in range(2):
                _row_copy(y_hbm, idx_ref[0, 0, k * tm + r], ybuf.at[into, k], r, sem.at[into]).start(priority=k)
            return carry

        lax.fori_loop(0, tm, issue, 0, unroll=DMA_UNROLL)

    @pl.when(i == 0)
    def _():
        gather(src_ref, 0)

    @pl.when(i + 1 < pl.num_programs(0))
    def _():
        gather(nxt_ref, 1 - slot)

    for k in range(2):
        pltpu.make_async_copy(y_hbm.at[pl.ds(0, tm)], ybuf.at[slot, k], sem.at[slot]).wait()

    wcol = jnp.transpose(ws_ref[...])
    y = wcol[:, 0:1] * ybuf[slot, 0] + wcol[:, 1:2] * ybuf[slot, 1]
    x = xs_ref[...] + _row_select(i * tm, tm, n_ctx, gate_ref) * y
    if final:
        ms = jnp.mean(x * x, axis=-1, keepdims=True)
        x = x * lax.rsqrt(ms + RMS_EPS) * fg_ref[...]
    o_ref[...] = x


def _combine(y_sorted, dest_blocks, xs, ws, mod_l, final_g, n_ctx, tm, final):
    t, d = xs.shape
    if final:
        skip = n_ctx // tm
        out_rows = t - n_ctx
        out_map = lambda i: (jnp.maximum(i - skip, 0), 0)
    else:
        out_rows = t
        out_map = lambda i: (i, 0)
    return pl.pallas_call(
        functools.partial(_combine_kernel, tm=tm, n_ctx=n_ctx, final=final),
        grid=(t // tm,),
        in_specs=[pl.BlockSpec((1, 1, 2 * tm), lambda i: (i, 0, 0), memory_space=pltpu.SMEM),
                  pl.BlockSpec((1, 1, 2 * tm), lambda i: (jnp.minimum(i + 1, t // tm - 1), 0, 0), memory_space=pltpu.SMEM),
                  pl.BlockSpec(memory_space=pl.ANY),
                  pl.BlockSpec((tm, d), lambda i: (i, 0)),
                  pl.BlockSpec((8, tm), lambda i: (0, i)),
                  pl.BlockSpec((8, d), lambda i: (0, 5)),
                  pl.BlockSpec((1, d), lambda i: (0, 0))],
        out_specs=pl.BlockSpec((tm, d), out_map),
        out_shape=jax.ShapeDtypeStruct((out_rows, d), F32),
        scratch_shapes=[pltpu.VMEM((2, 2, tm, d), F32), pltpu.SemaphoreType.DMA((2,))],
        compiler_params=_params(1, 4 * tm * d * 4 + 4 * tm * d * 4 + 4 * tm * d * 4),
        name="combine",
    )(dest_blocks, dest_blocks, y_sorted, xs, ws, mod_l, final_g)


def _rope_tables(n_lat, n_ctx):
    rows = n_lat // GRID_W
    row_ids = jnp.repeat(jnp.arange(rows, dtype=F32), GRID_W)
    col_ids = jnp.tile(jnp.arange(GRID_W, dtype=F32), rows)
    n_freq = HEAD_DIM // 4
    inv = ROPE_THETA ** (-jnp.arange(n_freq, dtype=F32) / n_freq)
    ar = row_ids[:, None] * inv
    ac = col_ids[:, None] * inv
    ang = jnp.concatenate([ar, ar, ac, ac], axis=-1)
    sign = jnp.where((jnp.arange(HEAD_DIM) % (HEAD_DIM // 2)) < HEAD_DIM // 4, -1.0, 1.0).astype(F32)
    cos = jnp.concatenate([jnp.ones((n_ctx, HEAD_DIM), F32), jnp.cos(ang)], axis=0)
    sin = jnp.concatenate([jnp.zeros((n_ctx, HEAD_DIM), F32), jnp.sin(ang) * sign], axis=0)
    return cos, sin


def _moe_plan(counts, n_blk):
    blk = MOE_BLOCK
    n_exp = counts.shape[0]
    padded = (counts + blk - 1) // blk * blk
    pad_end = jnp.cumsum(padded)
    pad_start = pad_end - padded
    n_used = jnp.maximum(pad_end[-1] // blk, 1)
    b = jnp.minimum(jnp.arange(n_blk, dtype=jnp.int32), n_used - 1)
    blk_e = jnp.sum((pad_end[None, :] <= (b * blk)[:, None]).astype(jnp.int32), axis=1)
    blk_e = jnp.minimum(blk_e, n_exp - 1)
    blk_valid = jnp.clip(counts[blk_e] - (b * blk - pad_start[blk_e]), 0, blk).astype(jnp.int32)
    live = counts > 0
    seg_of_e = jnp.cumsum(live.astype(jnp.int32)) - 1
    n_seg = jnp.maximum(seg_of_e[-1] + 1, 1)
    ks = jnp.arange(n_exp, dtype=jnp.int32)
    seg_e = jnp.sum(jnp.where(live[None, :] & (seg_of_e[None, :] == ks[:, None]), ks[None, :], 0), axis=1)
    blk_first = jnp.concatenate([jnp.ones((1,), jnp.int32), (blk_e[1:] != blk_e[:-1]).astype(jnp.int32)])
    return {
        "pad_start": pad_start.astype(jnp.int32),
        "blk_valid": blk_valid,
        "blk_seg": seg_of_e[blk_e].astype(jnp.int32),
        "blk_first": blk_first,
        "seg_e": jnp.concatenate([seg_e, seg_e[-1:]]).astype(jnp.int32),
        "cnt": jnp.stack([n_used, n_seg]).astype(jnp.int32),
    }


def kernel(x, c, ctx, c_ctx, w_ada, b_ada, norm1_g, norm2_g, w_in, pool_w, pool_scale, q_norm_g, k_norm_g,
           sink, w_branch, w_out, router_w, router_b, w_gate, w_up, w_down, final_g):
    assert x.shape[0] == 1, "single-sequence kernel"
    n_lat, d = x.shape[1], x.shape[2]
    n_ctx = ctx.shape[1]
    t = n_ctx + n_lat
    mix = d // 2
    depth = w_in.shape[0]
    n_exp = router_w.shape[1]
    tok = _pick(n_ctx, (256, 128))
    assert (2 * t) % MOE_BLOCK == 0
    n_blk = 2 * t // MOE_BLOCK + n_exp
    cos, sin_signed = _rope_tables(n_lat, n_ctx)
    mod = _ada(c, c_ctx, w_ada, b_ada)
    xs = jnp.concatenate([ctx[0], x[0]], axis=0)
    for l in range(depth):
        last = l == depth - 1
        p = _inproj(xs, mod[l], norm1_g[l][None], w_in, l, cos, sin_signed,
                    q_norm_g[l][None], k_norm_g[l][None], n_ctx)
        og = _gattn(p, q_norm_g[l], n_ctx, mix)
        ow = _wattn(p, sink[l], n_ctx, mix)
        z = _merge(p, og, ow, pool_w, pool_scale[l][None], w_branch, l, n_ctx, d)
        xs, h2, ints, ws, cnt = _outproj_router(z, w_out, l, xs, mod[l], norm2_g[l][None], router_w, router_b, n_ctx)
        plan = _moe_plan(cnt[:, 0].astype(jnp.int32), n_blk)
        dest_blocks = _dest_blocks(ints, plan["pad_start"], tok)
        xs_sorted = _dispatch(h2, dest_blocks, n_blk * MOE_BLOCK, tok)
        y_sorted = _ffn(xs_sorted, plan, w_gate, w_up, w_down, l)
        xs = _combine(y_sorted, dest_blocks, xs, ws, mod[l], final_g[None], n_ctx, tok, last)
    return xs[None]
```

```python
import functools

import jax
import jax.numpy as jnp
import numpy as np
from jax import lax
from jax.experimental import pallas as pl
from jax.experimental.pallas import tpu as pltpu

HEAD_DIM = 128
LANES = 128
BF16_ROWS = 16
KV_HEADS = 2
GRID_W = 64
WINDOW = 128
POOL_WINDOWS = (2, 4, 8, 16)
POOL_HALO = 16
N_GROUPS = 4
ROPE_THETA = 10000.0
RMS_EPS = 1e-6
NEG_INF = -1e30
ATTN_SCALE = HEAD_DIM ** -0.5
LOG2E = 1.4426950408889634
SOFTMAX_SAFE_LOG2 = 50.0
MOE_BLOCK = 256
DMA_UNROLL = 8
VMEM_CAP = 60 * 1024 * 1024

BF16 = jnp.bfloat16
F32 = jnp.float32


def _pick(n, prefs):
    for p in prefs:
        if n % p == 0:
            return p
    raise ValueError(f"no tile in {prefs} divides {n}")


def _largest_tile(n, cap, mult):
    return max(k for k in range(mult, min(n, cap) + 1, mult) if n % k == 0)


def _params(n_axes, vmem_bytes):
    limit = int(min(max(vmem_bytes * 5 // 4 + (4 << 20), 16 << 20), VMEM_CAP))
    return pltpu.CompilerParams(dimension_semantics=("arbitrary",) * n_axes, vmem_limit_bytes=limit)


def _sigmoid(x):
    return 1.0 / (1.0 + jnp.exp(-x))


def _sigmoid_tanh(x):
    return 0.5 * jnp.tanh(0.5 * x) + 0.5


def _dot(a, b):
    return jnp.dot(a, b, preferred_element_type=F32)


def _dot_nt(a, b):
    return lax.dot_general(a, b, (((1,), (1,)), ((), ())), preferred_element_type=F32)


def _row_select(row0, n_rows, n_ctx, mod_ref):
    rows = row0 + lax.broadcasted_iota(jnp.int32, (n_rows, 1), 0)
    return jnp.where(rows < n_ctx, mod_ref[1:2, :], mod_ref[0:1, :])


def _norm_modulate(x, g_ref, shift, scale):
    ms = jnp.mean(x * x, axis=-1, keepdims=True)
    y = x * lax.rsqrt(ms + RMS_EPS) * g_ref[...]
    return y * (1.0 + scale) + shift


def _ada_kernel(cb_ref, w_ref, b_ref, o_ref, *, tn):
    s = cb_ref[...]
    s = s * _sigmoid(s)
    outs = []
    for r in range(2):
        cols = [jnp.sum(w_ref[:, c * LANES:(c + 1) * LANES] * s[r], axis=0, keepdims=True)
                for c in range(tn // LANES)]
        outs.append(jnp.concatenate(cols, axis=1) + b_ref[...])
    o_ref[...] = jnp.concatenate(outs + [jnp.zeros((6, tn), F32)], axis=0)


def _ada(c, c_ctx, w_ada, b_ada):
    depth, d, w6 = w_ada.shape
    tn = _pick(w6, (1024, 512, 256, 128))
    cb = jnp.broadcast_to(jnp.stack([c[0], c_ctx])[:, :, None], (2, d, LANES))
    vmem = 2 * d * tn * 4 + 2 * d * LANES * 4 * 2 + d * LANES * 4 * 4
    return pl.pallas_call(
        functools.partial(_ada_kernel, tn=tn),
        grid=(depth, w6 // tn),
        in_specs=[pl.BlockSpec((2, d, LANES), lambda l, j: (0, 0, 0)),
                  pl.BlockSpec((None, d, tn), lambda l, j: (l, 0, j)),
                  pl.BlockSpec((None, 1, tn), lambda l, j: (l, 0, j))],
        out_specs=pl.BlockSpec((None, 8, tn), lambda l, j: (l, 0, j)),
        out_shape=jax.ShapeDtypeStruct((depth, 8, w6), F32),
        compiler_params=_params(2, vmem),
        name="ada",
    )(cb, w_ada, b_ada.reshape(depth, 1, w6))


def _rope(x, cos, sin_signed, first_half):
    rot = jnp.where(first_half, pltpu.roll(x, 3 * HEAD_DIM // 4, axis=1), pltpu.roll(x, HEAD_DIM // 4, axis=1))
    return x * cos + rot * sin_signed


def _head_rms(x, g_ref):
    ms = jnp.mean(x * x, axis=-1, keepdims=True)
    return x * lax.rsqrt(ms + RMS_EPS) * g_ref[...]


def _inproj_kernel(x_ref, shift_ref, scale_ref, g_ref, w_ref, cos_ref, sin_ref, qg_ref, kg_ref,
                   o_ref, h_sc, *, tm, tn, n_ctx, signatures):
    i = pl.program_id(0)
    j = pl.program_id(1)

    def normed():
        x = x_ref[...]
        return x * lax.rsqrt(jnp.mean(x * x, axis=-1, keepdims=True) + RMS_EPS)

    has_ctx_rows = i * tm < n_ctx

    @pl.when((j == 0) & has_ctx_rows)
    def _():
        gain = g_ref[...] * (1.0 + scale_ref[0:2, :])
        rows = i * tm + lax.broadcasted_iota(jnp.int32, (tm, 1), 0)
        is_ctx = rows < n_ctx
        h = normed() * jnp.where(is_ctx, gain[1:2], gain[0:1]) + jnp.where(is_ctx, shift_ref[1:2, :], shift_ref[0:1, :])
        h_sc[...] = h.astype(BF16)

    @pl.when((j == 0) & jnp.logical_not(has_ctx_rows))
    def _():
        h = normed() * (g_ref[...] * (1.0 + scale_ref[0:1, :])) + shift_ref[0:1, :]
        h_sc[...] = h.astype(BF16)

    acc = _dot(h_sc[...], w_ref[...].astype(BF16))

    lane = lax.broadcasted_iota(jnp.int32, (1, HEAD_DIM), 1)
    first_half = (lane % (HEAD_DIM // 2)) < (HEAD_DIM // 4)

    for sig, js in signatures:
        cond = functools.reduce(jnp.logical_or, [j == jj for jj in js])

        @pl.when(cond)
        def _(sig=sig):
            if all(t == "plain" for t in sig):
                o_ref[...] = acc.astype(BF16)
                return
            for bi, typ in enumerate(sig):
                a = acc[:, bi * LANES:(bi + 1) * LANES]
                if typ != "plain":
                    if typ in ("qnr", "knr"):
                        a = _head_rms(a, qg_ref if typ == "qnr" else kg_ref)
                    a = _rope(a, cos_ref[...], sin_ref[...], first_half)
                    if typ in ("qnr", "qr"):
                        a = a * (ATTN_SCALE * LOG2E)
                o_ref[:, bi * LANES:(bi + 1) * LANES] = a.astype(BF16)


def _col_types(mix, d):
    hq = mix // HEAD_DIM
    types = (["plain"] * hq + ["qnr"] * hq + ["knr"] * KV_HEADS + ["plain"] * KV_HEADS
             + ["qr"] * hq + ["kr"] * KV_HEADS + ["plain"] * KV_HEADS + ["plain"] * (3 * d // LANES))
    return types


def _inproj(xs, mod_l, g1, w_in, layer, cos, sin_signed, qg, kg, n_ctx):
    t, d = xs.shape
    in_w = w_in.shape[-1]
    mix = d // 2
    tm = _pick(t, (768, 512, 256, 128))
    tn = _pick(in_w, (1280, 1024, 512, 256))
    types = _col_types(mix, d)
    assert len(types) * LANES == in_w
    per = tn // LANES
    sigs = {}
    for jj in range(in_w // tn):
        sigs.setdefault(tuple(types[jj * per:(jj + 1) * per]), []).append(jj)
    signatures = tuple((s, tuple(js)) for s, js in sigs.items())
    vmem = (2 * tm * d * 4 + 2 * d * tn * 4 + tm * d * 2 + 2 * tm * tn * 2 + tm * tn * 4 * 2 + d * tn * 2
            + 4 * tm * LANES * 4)
    return pl.pallas_call(
        functools.partial(_inproj_kernel, tm=tm, tn=tn, n_ctx=n_ctx, signatures=signatures),
        grid=(t // tm, in_w // tn),
        in_specs=[pl.BlockSpec((tm, d), lambda i, j: (i, 0)),
                  pl.BlockSpec((8, d), lambda i, j: (0, 0)),
                  pl.BlockSpec((8, d), lambda i, j: (0, 1)),
                  pl.BlockSpec((1, d), lambda i, j: (0, 0)),
                  pl.BlockSpec((None, d, tn), lambda i, j: (layer, 0, j)),
                  pl.BlockSpec((tm, HEAD_DIM), lambda i, j: (i, 0)),
                  pl.BlockSpec((tm, HEAD_DIM), lambda i, j: (i, 0)),
                  pl.BlockSpec((1, HEAD_DIM), lambda i, j: (0, 0)),
                  pl.BlockSpec((1, HEAD_DIM), lambda i, j: (0, 0))],
        out_specs=pl.BlockSpec((tm, tn), lambda i, j: (i, j)),
        out_shape=jax.ShapeDtypeStruct((t, in_w), BF16),
        scratch_shapes=[pltpu.VMEM((tm, d), BF16)],
        compiler_params=_params(2, vmem),
        name="inproj",
    )(xs, mod_l, mod_l, g1, w_in, cos, sin_signed, qg, kg)


def _lane_repeat(x, n):
    return jnp.concatenate([x] * n, axis=1)


def _stack_heads(q_ref, g):
    return jnp.concatenate([q_ref[:, h * HEAD_DIM:(h + 1) * HEAD_DIM] for h in range(g)], axis=0)


def _unstack_store(o_ref, out, g, tq):
    for h in range(g):
        o_ref[:, h * HEAD_DIM:(h + 1) * HEAD_DIM] = out[h * tq:(h + 1) * tq, :].astype(o_ref.dtype)


def _gattn_kernel(qmax_ref, q_ref, k_ref, v_ref, o_ref, vx_sc, kmax_sc, flag_sc, m_sc, acc_sc, *, tq, tk, g, n_ctx, n_lat):
    qi = pl.program_id(1)
    rows = g * tq

    @pl.when(qi == 0)
    def _():
        vx_sc[:, 0:HEAD_DIM] = v_ref[...]
        vx_sc[:, HEAD_DIM:] = jnp.ones((vx_sc.shape[0], HEAD_DIM), BF16)
        kf = k_ref[...].astype(F32)
        k2 = jnp.max(jnp.sum(kf * kf, axis=-1, keepdims=True), axis=0, keepdims=True)
        kmax = jnp.broadcast_to(jnp.sqrt(k2), kmax_sc.shape)
        kmax_sc[...] = kmax
        flag_sc[0] = (qmax_ref[0] * jnp.max(kmax) <= SOFTMAX_SAFE_LOG2).astype(jnp.int32)

    bounded = flag_sc[0] == 1
    ctx_only = qi * tq < n_ctx
    ctx_chunk = [(0, n_ctx)]
    all_chunks = ctx_chunk + [(n_ctx + c * tk, tk) for c in range(n_lat // tk)]

    def finish():
        acc = acc_sc[...]
        _unstack_store(o_ref, acc[:, :HEAD_DIM] / acc[:, HEAD_DIM:], g, tq)

    def attend_bounded(chunks):
        q = _stack_heads(q_ref, g)
        qf = q.astype(F32)
        bound = jnp.sqrt(jnp.sum(qf * qf, axis=-1, keepdims=True)) * kmax_sc[0:1, :]
        for n, (lo, size) in enumerate(chunks):
            s = _dot_nt(q, k_ref[lo:lo + size, :])
            p = jnp.exp2(s - _lane_repeat(bound, size // LANES))
            pv = _dot(p.astype(BF16), vx_sc[lo:lo + size, :])
            acc_sc[...] = pv if n == 0 else acc_sc[...] + pv
        finish()

    def attend_online(chunks):
        q = _stack_heads(q_ref, g)
        for n, (lo, size) in enumerate(chunks):
            s = _dot_nt(q, k_ref[lo:lo + size, :])
            mx = jnp.max(s, axis=-1, keepdims=True)
            if n == 0:
                m_new = jnp.broadcast_to(mx, (rows, LANES))
            else:
                m_prev = m_sc[...]
                m_new = jnp.maximum(m_prev, mx)
            p = jnp.exp2(s - _lane_repeat(m_new, size // LANES))
            pv = _dot(p.astype(BF16), vx_sc[lo:lo + size, :])
            if n == 0:
                acc_sc[...] = pv
            else:
                acc_sc[...] = _lane_repeat(jnp.exp2(m_prev - m_new), 2) * acc_sc[...] + pv
            m_sc[...] = m_new
        finish()

    for use_bound, attend in ((True, attend_bounded), (False, attend_online)):
        path = bounded if use_bound else jnp.logical_not(bounded)
        pl.when(path & ctx_only)(functools.partial(attend, ctx_chunk))
        pl.when(path & jnp.logical_not(ctx_only))(functools.partial(attend, all_chunks))


def _gattn(p, q_gain, n_ctx, mix):
    t = p.shape[0]
    qmax = (1.02 * HEAD_DIM ** 0.5 * ATTN_SCALE * LOG2E) * jnp.max(jnp.abs(q_gain)).reshape(1)
    hq = mix // HEAD_DIM
    g = hq // KV_HEADS
    n_lat = t - n_ctx
    tq = _pick(n_ctx, (256, 128))
    assert t % tq == 0
    tk = _pick(n_lat, (2048, 1024, 512, 256, 128))
    gw = g * HEAD_DIM
    off_q = mix // gw
    off_k = (mix + hq * HEAD_DIM) // HEAD_DIM
    off_v = off_k + KV_HEADS
    rows = g * tq
    vmem = (2 * tq * gw * 2 * 2 + 2 * 2 * t * HEAD_DIM * 2 + t * 2 * HEAD_DIM * 2 + rows * LANES * 4 * 4
            + rows * max(tk, n_ctx) * 4 * 3)
    return pl.pallas_call(
        functools.partial(_gattn_kernel, tq=tq, tk=tk, g=g, n_ctx=n_ctx, n_lat=n_lat),
        grid=(KV_HEADS, t // tq),
        in_specs=[pl.BlockSpec(memory_space=pltpu.SMEM),
                  pl.BlockSpec((tq, gw), lambda h, i: (i, off_q + h)),
                  pl.BlockSpec((t, HEAD_DIM), lambda h, i: (0, off_k + h)),
                  pl.BlockSpec((t, HEAD_DIM), lambda h, i: (0, off_v + h))],
        out_specs=pl.BlockSpec((tq, gw), lambda h, i: (i, h)),
        out_shape=jax.ShapeDtypeStruct((t, mix), BF16),
        scratch_shapes=[pltpu.VMEM((t, 2 * HEAD_DIM), BF16), pltpu.VMEM((8, LANES), F32), pltpu.SMEM((1,), jnp.int32),
                        pltpu.VMEM((rows, LANES), F32), pltpu.VMEM((rows, 2 * HEAD_DIM), F32)],
        compiler_params=_params(2, vmem),
        name="gattn",
    )(qmax, p, p, p)


def _wattn_kernel(*refs, tq, n_sub, g, n_ctx, span, t):
    q_refs, k_refs, v_refs = refs[0:KV_HEADS], refs[KV_HEADS:2 * KV_HEADS], refs[2 * KV_HEADS:3 * KV_HEADS]
    sink_ref, o_ref, vx_sc = refs[3 * KV_HEADS:]
    qi = pl.program_id(0)

    @pl.when(qi == 0)
    def _():
        for h in range(KV_HEADS):
            vx_sc[h, :, 0:HEAD_DIM] = v_refs[h][...]
            vx_sc[h, :, HEAD_DIM:] = jnp.ones((t, HEAD_DIM), BF16)

    for s in range(n_sub):
        for h in range(KV_HEADS):
            _wattn_head(qi * n_sub + s, q_refs[h].at[s * tq:(s + 1) * tq], k_refs[h], vx_sc.at[h], sink_ref.at[h],
                        o_ref.at[s * tq:(s + 1) * tq, h * g * HEAD_DIM:(h + 1) * g * HEAD_DIM],
                        tq=tq, g=g, n_ctx=n_ctx, span=span, t=t)


def _wattn_head(qi, q_ref, k_ref, vx_sc, sink_ref, o_ref, *, tq, g, n_ctx, span, t):
    q = _stack_heads(q_ref, g)
    rows = q.shape[0]
    start = pl.multiple_of(jnp.clip(qi * tq - WINDOW, 0, t - span), LANES)

    far = t + 4 * WINDOW
    qrow = qi * tq + lax.broadcasted_iota(jnp.int32, (tq, 1), 0)
    qrow = jnp.concatenate([jnp.where(qrow >= n_ctx, qrow, -far)] * g, axis=0)
    krow = start + lax.broadcasted_iota(jnp.int32, (1, span), 1)
    krow = jnp.where(krow >= n_ctx, krow, far)
    keep = jnp.abs(krow - qrow) <= WINDOW

    s_c = _dot_nt(q, k_ref[0:n_ctx, :])
    s_w = jnp.where(keep, _dot_nt(q, k_ref[pl.ds(start, span), :]), NEG_INF)
    sink = sink_ref[...] * LOG2E
    mx = jnp.maximum(jnp.maximum(jnp.max(s_c, axis=-1, keepdims=True), jnp.max(s_w, axis=-1, keepdims=True)), sink)
    m = jnp.broadcast_to(mx, (rows, LANES))
    p_c = jnp.exp2(s_c - _lane_repeat(m, n_ctx // LANES))
    p_w = jnp.exp2(s_w - _lane_repeat(m, span // LANES))
    acc = _dot(p_c.astype(BF16), vx_sc[0:n_ctx, :]) + _dot(p_w.astype(BF16), vx_sc[pl.ds(start, span), :])
    denom = acc[:, HEAD_DIM:] + jnp.exp2(sink - m)
    _unstack_store(o_ref, acc[:, :HEAD_DIM] / denom, g, tq)


def _wattn(p, sink_l, n_ctx, mix):
    t = p.shape[0]
    hq = mix // HEAD_DIM
    g = hq // KV_HEADS
    tq = _pick(n_ctx, (256, 128))
    span = tq + 2 * WINDOW
    assert t % tq == 0 and t >= span
    gw = g * HEAD_DIM
    base = mix + hq * HEAD_DIM + 2 * KV_HEADS * HEAD_DIM
    off_q = base // gw
    off_k = (base + hq * HEAD_DIM) // HEAD_DIM
    off_v = off_k + KV_HEADS
    rows = g * tq
    sink_rows = jnp.repeat(sink_l.reshape(KV_HEADS, g), tq, axis=1).reshape(KV_HEADS, rows, 1)
    n_sub = 3 if (t // tq) % 3 == 0 else 1
    vmem = KV_HEADS * (2 * n_sub * tq * gw * 2 * 2 + 2 * 2 * t * HEAD_DIM * 2 + 2 * rows * LANES * 4
                       + n_sub * rows * (span + n_ctx) * 4 * 3 + t * 2 * HEAD_DIM * 2)
    heads = range(KV_HEADS)
    return pl.pallas_call(
        functools.partial(_wattn_kernel, tq=tq, n_sub=n_sub, g=g, n_ctx=n_ctx, span=span, t=t),
        grid=(t // (tq * n_sub),),
        in_specs=([pl.BlockSpec((tq * n_sub, gw), lambda i, h=h: (i, off_q + h)) for h in heads]
                  + [pl.BlockSpec((t, HEAD_DIM), lambda i, h=h: (0, off_k + h)) for h in heads]
                  + [pl.BlockSpec((t, HEAD_DIM), lambda i, h=h: (0, off_v + h)) for h in heads]
                  + [pl.BlockSpec((KV_HEADS, rows, 1), lambda i: (0, 0, 0))]),
        out_specs=pl.BlockSpec((tq * n_sub, mix), lambda i: (i, 0)),
        out_shape=jax.ShapeDtypeStruct((t, mix), BF16),
        scratch_shapes=[pltpu.VMEM((KV_HEADS, t, 2 * HEAD_DIM), BF16)],
        compiler_params=_params(1, vmem),
        name="wattn",
    )(*([p] * (3 * KV_HEADS)), sink_rows)


def _merge_kernel(u_ref, up_ref, un_ref, og_ref, ow_ref, g0_ref, g1_ref, g2_ref, pw_ref, ps_ref, wb_ref,
                  z_ref, ext_sc, pool_sc, *, tm, n_ctx, t, mix):
    i = pl.program_id(0)
    j = pl.program_id(1)
    gw = mix // len(POOL_WINDOWS)

    def pool(clipped):
        ext_sc[0:POOL_HALO, :] = up_ref[...].astype(F32)
        ext_sc[POOL_HALO:POOL_HALO + tm, :] = u_ref[...].astype(F32)
        ext_sc[POOL_HALO + tm:, :] = un_ref[...].astype(F32)
        r = i * tm + lax.broadcasted_iota(jnp.int32, (tm, 1), 0)
        r_ctx = r < n_ctx
        for gi, w in enumerate(POOL_WINDOWS):
            c0, c1 = gi * gw, (gi + 1) * gw
            tot = jnp.zeros((tm, gw), F32)
            cnt = jnp.zeros((tm, 1), F32)
            for off in range(-((w - 1) // 2), w // 2 + 1):
                part = ext_sc[POOL_HALO + off:POOL_HALO + off + tm, c0:c1]
                if clipped:
                    rr = r + off
                    ok = (rr >= 0) & (rr < t) & ((rr < n_ctx) == r_ctx)
                    part = jnp.where(ok, part, 0.0)
                    cnt = cnt + ok.astype(F32)
                tot = tot + part
            mean = tot / cnt if clipped else tot * (1.0 / w)
            dlt = mean - ext_sc[POOL_HALO:POOL_HALO + tm, c0:c1]
            y = _dot(dlt.astype(BF16), pw_ref[gi].astype(BF16)) * ps_ref[:, c0:c1]
            pool_sc[:, c0:c1] = y.astype(BF16)

    reach = max(POOL_WINDOWS) // 2
    interior = (i * tm - reach >= n_ctx) & ((i + 1) * tm + reach <= t)
    pl.when((j == 0) & interior)(functools.partial(pool, False))
    pl.when((j == 0) & jnp.logical_not(interior))(functools.partial(pool, True))

    def gate(g_ref):
        return _sigmoid_tanh(g_ref[...].astype(F32))

    z = gate(g0_ref) * _dot(pool_sc[...], wb_ref[0].astype(BF16))
    z = z + gate(g1_ref) * _dot(og_ref[...], wb_ref[1].astype(BF16))
    z = z + gate(g2_ref) * _dot(ow_ref[...], wb_ref[2].astype(BF16))
    z_ref[...] = z.astype(BF16)


def _merge(p, og, ow, pool_w, pool_scale, w_branch, layer, n_ctx, d):
    t = p.shape[0]
    mix = d // 2
    tm = _largest_tile(t, 1100, POOL_HALO)
    tn = _pick(d, (512, 256, 128))
    hb = tm // POOL_HALO
    n_hb = t // POOL_HALO
    gate0 = (p.shape[1] - 3 * d) // tn
    gs = pool_w.shape[-1]
    vmem = (2 * 3 * tm * mix * 2 + 2 * 3 * tm * tn * 2 + 2 * 3 * mix * tn * 4 + 3 * mix * tn * 2
            + (tm + 2 * POOL_HALO) * mix * 4 + tm * mix * 2 + 2 * tm * tn * 2 + 4 * tm * tn * 4
            + 2 * len(POOL_WINDOWS) * gs * gs * 4 + 6 * tm * gs * 4)
    return pl.pallas_call(
        functools.partial(_merge_kernel, tm=tm, n_ctx=n_ctx, t=t, mix=mix),
        grid=(t // tm, d // tn),
        in_specs=[pl.BlockSpec((tm, mix), lambda i, j: (i, 0)),
                  pl.BlockSpec((POOL_HALO, mix), lambda i, j: (jnp.maximum(i * hb - 1, 0), 0)),
                  pl.BlockSpec((POOL_HALO, mix), lambda i, j: (jnp.minimum((i + 1) * hb, n_hb - 1), 0)),
                  pl.BlockSpec((tm, mix), lambda i, j: (i, 0)),
                  pl.BlockSpec((tm, mix), lambda i, j: (i, 0)),
                  pl.BlockSpec((tm, tn), lambda i, j: (i, gate0 + j)),
                  pl.BlockSpec((tm, tn), lambda i, j: (i, gate0 + d // tn + j)),
                  pl.BlockSpec((tm, tn), lambda i, j: (i, gate0 + 2 * (d // tn) + j)),
                  pl.BlockSpec((None, len(POOL_WINDOWS), gs, gs), lambda i, j: (layer, 0, 0, 0)),
                  pl.BlockSpec((1, mix), lambda i, j: (0, 0)),
                  pl.BlockSpec((None, 3, mix, tn), lambda i, j: (layer, 0, 0, j))],
        out_specs=pl.BlockSpec((tm, tn), lambda i, j: (i, j)),
        out_shape=jax.ShapeDtypeStruct((t, d), BF16),
        scratch_shapes=[pltpu.VMEM((tm + 2 * POOL_HALO, mix), F32), pltpu.VMEM((tm, mix), BF16)],
        compiler_params=_params(2, vmem),
        name="merge",
    )(p, p, p, og, ow, p, p, p, pool_w, pool_scale, w_branch)


W_CHUNK = 256


def _route(h, rwt_ref, rb_ref, ints_ref, ws_ref, cnt_ref, carry_sc, *, tm, n_exp):
    scores = _sigmoid(_dot_nt(rwt_ref[...].astype(BF16), h.astype(BF16)))
    sel = scores + rb_ref[...]
    per = n_exp // N_GROUPS
    sub = lax.broadcasted_iota(jnp.int32, (per, tm), 0)

    def top2(v):
        m1 = jnp.max(v, axis=0, keepdims=True)
        i1 = jnp.min(jnp.where(v == m1, sub, per), axis=0, keepdims=True)
        rest = jnp.where(sub == i1, -jnp.inf, v)
        m2 = jnp.max(rest, axis=0, keepdims=True)
        i2 = jnp.min(jnp.where(rest == m2, sub, per), axis=0, keepdims=True)
        return m1 + m2, i1, i2

    tops = [top2(sel[gi * per:(gi + 1) * per, :]) for gi in range(N_GROUPS)]
    best, l1, l2 = tops[0]
    grp = jnp.zeros((1, tm), jnp.int32)
    for gi in range(1, N_GROUPS):
        gs, a1, a2 = tops[gi]
        better = gs > best
        best = jnp.where(better, gs, best)
        grp = jnp.where(better, gi, grp)
        l1 = jnp.where(better, a1, l1)
        l2 = jnp.where(better, a2, l2)

    hot1 = [(grp == gi) & (sub == l1) for gi in range(N_GROUPS)]
    hot2 = [(grp == gi) & (sub == l2) for gi in range(N_GROUPS)]
    assign = jnp.concatenate([(a | b).astype(F32) for a, b in zip(hot1, hot2)], axis=0)

    before = (lax.broadcasted_iota(jnp.int32, (tm, tm), 0) < lax.broadcasted_iota(jnp.int32, (tm, tm), 1))
    pos = _dot(assign.astype(BF16), before.astype(F32).astype(BF16)) + carry_sc[...]

    def pick(hots, val):
        return sum(jnp.sum(jnp.where(hots[gi], val[gi * per:(gi + 1) * per, :], 0.0), axis=0, keepdims=True)
                   for gi in range(N_GROUPS))

    s1 = pick(hot1, scores)
    s2 = pick(hot2, scores)
    r1 = pick(hot1, pos)
    r2 = pick(hot2, pos)
    tot = s1 + s2
    ints_ref[0:1, :] = grp * per + l1
    ints_ref[1:2, :] = grp * per + l2
    ints_ref[2:3, :] = r1.astype(jnp.int32)
    ints_ref[3:4, :] = r2.astype(jnp.int32)
    ints_ref[4:8, :] = jnp.zeros((4, tm), jnp.int32)
    ws_ref[0:1, :] = s1 / tot
    ws_ref[1:2, :] = s2 / tot
    ws_ref[2:8, :] = jnp.zeros((6, tm), F32)
    carry_sc[...] = carry_sc[...] + jnp.sum(assign, axis=1, keepdims=True)
    cnt_ref[...] = jnp.broadcast_to(carry_sc[...], cnt_ref.shape)


def _outproj_router_kernel(z_ref, w_hbm, xs_ref, gate_ref, shift_ref, scale_ref, g_ref, rwt_ref, rb_ref,
                           o_ref, h_ref, ints_ref, ws_ref, cnt_ref, w_sc, stage, sem, carry_sc,
                           *, tm, n_ctx, n_exp, layer):
    i = pl.program_id(0)
    d = w_sc.shape[0]

    @pl.when(i == 0)
    def _():
        carry_sc[...] = jnp.zeros(carry_sc.shape, F32)
        n_chunks = d // W_CHUNK

        def chunk(c):
            return pltpu.make_async_copy(w_hbm.at[layer, :, c * W_CHUNK:(c + 1) * W_CHUNK], stage.at[c % 2], sem.at[c % 2])

        chunk(0).start()
        for c in range(n_chunks):
            if c + 1 < n_chunks:
                chunk(c + 1).start()
            chunk(c).wait()
            w_sc[:, c * W_CHUNK:(c + 1) * W_CHUNK] = stage[c % 2].astype(BF16)

    gate = _row_select(i * tm, tm, n_ctx, gate_ref)
    x = xs_ref[...] + gate * _dot(z_ref[...], w_sc[...])
    o_ref[...] = x
    shift = _row_select(i * tm, tm, n_ctx, shift_ref)
    scale = _row_select(i * tm, tm, n_ctx, scale_ref)
    h = _norm_modulate(x, g_ref, shift, scale)
    h_ref[...] = h
    _route(h, rwt_ref, rb_ref, ints_ref, ws_ref, cnt_ref, carry_sc, tm=tm, n_exp=n_exp)


def _outproj_router(z, w_out, layer, xs, mod_l, g2, router_w, router_b, n_ctx):
    t, d = xs.shape
    n_exp = router_w.shape[1]
    tm = _largest_tile(t, 400, LANES)
    vmem = (d * d * 2 + 2 * d * W_CHUNK * 4 + 2 * tm * d * 2 + 6 * tm * d * 4 + 6 * tm * d * 4 + 3 * tm * tm * 4
            + 2 * n_exp * d * 4)
    row = lambda i: (i, 0)
    return pl.pallas_call(
        functools.partial(_outproj_router_kernel, tm=tm, n_ctx=n_ctx, n_exp=n_exp, layer=layer),
        grid=(t // tm,),
        in_specs=[pl.BlockSpec((tm, d), row),
                  pl.BlockSpec(memory_space=pl.ANY),
                  pl.BlockSpec((tm, d), row),
                  pl.BlockSpec((8, d), lambda i: (0, 2)),
                  pl.BlockSpec((8, d), lambda i: (0, 3)),
                  pl.BlockSpec((8, d), lambda i: (0, 4)),
                  pl.BlockSpec((1, d), lambda i: (0, 0)),
                  pl.BlockSpec((n_exp, d), lambda i: (0, 0)),
                  pl.BlockSpec((n_exp, 1), lambda i: (0, 0))],
        out_specs=[pl.BlockSpec((tm, d), row),
                   pl.BlockSpec((tm, d), row),
                   pl.BlockSpec((8, tm), lambda i: (0, i)),
                   pl.BlockSpec((8, tm), lambda i: (0, i)),
                   pl.BlockSpec((n_exp, LANES), lambda i: (0, 0))],
        out_shape=[jax.ShapeDtypeStruct((t, d), F32),
                   jax.ShapeDtypeStruct((t, d), F32),
                   jax.ShapeDtypeStruct((8, t), jnp.int32),
                   jax.ShapeDtypeStruct((8, t), F32),
                   jax.ShapeDtypeStruct((n_exp, LANES), F32)],
        scratch_shapes=[pltpu.VMEM((d, d), BF16), pltpu.VMEM((2, d, W_CHUNK), F32), pltpu.SemaphoreType.DMA((2,)),
                        pltpu.VMEM((n_exp, 1), F32)],
        compiler_params=_params(1, vmem),
        name="outproj_router",
    )(z, w_out, xs, mod_l, mod_l, mod_l, g2, router_w.T, router_b.reshape(n_exp, 1))


def _row_copy(src_ref, src_row, dst_ref, dst_row, sem):
    return pltpu.make_async_copy(src_ref.at[pl.ds(src_row, 1)], dst_ref.at[pl.ds(dst_row, 1)], sem)


def _dispatch_kernel(dst_ref, h_ref, xs_hbm, sem, *, tm):
    def issue(r8, carry):
        base = pl.multiple_of(r8 * DMA_UNROLL, DMA_UNROLL)
        for j in range(DMA_UNROLL):
            for k in range(2):
                _row_copy(h_ref, base + j, xs_hbm, dst_ref[0, 0, k * tm + base + j], sem).start(priority=k)
        return carry

    lax.fori_loop(0, tm // DMA_UNROLL, issue, 0)
    for k in range(2):
        pltpu.make_async_copy(h_ref, xs_hbm.at[pl.ds(0, tm)], sem).wait()


def _dest_blocks(ints, pad_start, tm):
    t = ints.shape[1]
    experts = jnp.arange(pad_start.shape[0], dtype=jnp.int32)
    start = jnp.sum(jnp.where(ints[0:2, :, None] == experts, pad_start, 0), axis=-1)
    dest = start + ints[2:4]
    return dest.reshape(2, t // tm, tm).transpose(1, 0, 2).reshape(t // tm, 1, 2 * tm)


def _dispatch(h2, dest_blocks, n_rows, tm):
    t, d = h2.shape
    return pl.pallas_call(
        functools.partial(_dispatch_kernel, tm=tm),
        grid=(t // tm,),
        in_specs=[pl.BlockSpec((1, 1, 2 * tm), lambda i: (i, 0, 0), memory_space=pltpu.SMEM),
                  pl.BlockSpec((tm, d), lambda i: (i, 0))],
        out_specs=pl.BlockSpec(memory_space=pl.ANY),
        out_shape=jax.ShapeDtypeStruct((n_rows, d), F32),
        scratch_shapes=[pltpu.SemaphoreType.DMA(())],
        compiler_params=_params(1, 2 * tm * d * 4),
        name="dispatch",
    )(dest_blocks, h2)


def _ffn_kernel(blk_valid_ref, blk_seg_ref, blk_first_ref, seg_e_ref, cnt_ref, x_ref, wg_hbm, wu_hbm, wd_hbm, y_ref,
                wg_st, wu_st, wd_st, wg_sc, wu_sc, wd_sc, sem_g, sem_u, sem_d, *, layer):
    b = pl.program_id(0)
    n_seg = cnt_ref[1]

    def copies(k):
        e = seg_e_ref[k]
        return [pltpu.make_async_copy(w.at[layer, e], st, sm)
                for w, st, sm in ((wg_hbm, wg_st, sem_g), (wu_hbm, wu_st, sem_u), (wd_hbm, wd_st, sem_d))]

    def x_block():
        rows = lax.broadcasted_iota(jnp.int32, (x_ref.shape[0], 1), 0)
        return jnp.where(rows < blk_valid_ref[b], x_ref[...], 0.0).astype(BF16)

    def finish(gte, up):
        hidden = (gte * _sigmoid(gte) * up).astype(BF16)
        y_ref[...] = _dot(hidden, wd_sc[...])

    @pl.when(b < cnt_ref[0])
    def _():
        @pl.when(b == 0)
        def _():
            for cp in copies(0):
                cp.start()

        @pl.when(blk_first_ref[b] == 1)
        def _():
            k = blk_seg_ref[b]
            for cp, nxt, st, dst in zip(copies(k), copies(k + 1), (wg_st, wu_st, wd_st), (wg_sc, wu_sc, wd_sc)):
                cp.wait()
                dst[...] = st[...].astype(BF16)
                pl.when(k + 1 < n_seg)(nxt.start)

        x = x_block()
        finish(_dot(x, wg_sc[...]), _dot(x, wu_sc[...]))


def _ffn(xs_sorted, plan, w_gate, w_up, w_down, layer):
    n_rows, d = xs_sorted.shape
    ff = w_gate.shape[-1]
    blk = MOE_BLOCK
    n_blk = n_rows // blk
    prefetch = (plan["blk_valid"], plan["blk_seg"], plan["blk_first"], plan["seg_e"], plan["cnt"])
    row_map = lambda b, v, s, f, e, c: (jnp.minimum(b, c[0] - 1), 0)
    hbm = pl.BlockSpec(memory_space=pl.ANY)
    dma = pltpu.SemaphoreType.DMA(())
    grid_spec = pltpu.PrefetchScalarGridSpec(
        num_scalar_prefetch=len(prefetch),
        grid=(n_blk,),
        in_specs=[pl.BlockSpec((blk, d), row_map), hbm, hbm, hbm],
        out_specs=pl.BlockSpec((blk, d), row_map),
        scratch_shapes=[pltpu.VMEM((d, ff), F32), pltpu.VMEM((d, ff), F32), pltpu.VMEM((ff, d), F32),
                        pltpu.VMEM((d, ff), BF16), pltpu.VMEM((d, ff), BF16), pltpu.VMEM((ff, d), BF16),
                        dma, dma, dma],
    )
    vmem = 3 * d * ff * (4 + 2) + 4 * blk * d * 4 + blk * d * 2 + 3 * blk * ff * 4 + blk * d * 4
    return pl.pallas_call(
        functools.partial(_ffn_kernel, layer=layer),
        grid_spec=grid_spec,
        out_shape=jax.ShapeDtypeStruct((n_rows, d), F32),
        compiler_params=pltpu.CompilerParams(dimension_semantics=("arbitrary",), vmem_limit_bytes=min(vmem + (6 << 20), VMEM_CAP)),
        name="ffn",
    )(*prefetch, xs_sorted, w_gate, w_up, w_down)


def _combine_kernel(src_ref, nxt_ref, y_hbm, xs_ref, ws_ref, gate_ref, fg_ref, o_ref, ybuf, sem, *, tm, n_ctx, final):
    i = pl.program_id(0)
    slot = i % 2

    def gather(idx_ref, into):
        def issue(r8, carry):
            base = pl.multiple_of(r8 * DMA_UNROLL, DMA_UNROLL)
            for j in range(DMA_UNROLL):
                for k in range(2):
                    _row_copy(y_hbm, idx_ref[0, 0, k * tm + base + j], ybuf.at[into, k], base + j, sem.at[into]).start(priority=k)
            return carry

        lax.fori_loop(0, tm // DMA_UNROLL, issue, 0)

    @pl.when(i == 0)
    def _():
        gather(src_ref, 0)

    @pl.when(i + 1 < pl.num_programs(0))
    def _():
        gather(nxt_ref, 1 - slot)

    for k in range(2):
        pltpu.make_async_copy(y_hbm.at[pl.ds(0, tm)], ybuf.at[slot, k], sem.at[slot]).wait()

    wcol = jnp.transpose(ws_ref[...])
    y = wcol[:, 0:1] * ybuf[slot, 0] + wcol[:, 1:2] * ybuf[slot, 1]
    x = xs_ref[...] + _row_select(i * tm, tm, n_ctx, gate_ref) * y
    if final:
        ms = jnp.mean(x * x, axis=-1, keepdims=True)
        x = x * lax.rsqrt(ms + RMS_EPS) * fg_ref[...]
    o_ref[...] = x


def _combine(y_sorted, dest_blocks, xs, ws, mod_l, final_g, n_ctx, tm, final):
    t, d = xs.shape
    if final:
        skip = n_ctx // tm
        out_rows = t - n_ctx
        out_map = lambda i: (jnp.maximum(i - skip, 0), 0)
    else:
        out_rows = t
        out_map = lambda i: (i, 0)
    return pl.pallas_call(
        functools.partial(_combine_kernel, tm=tm, n_ctx=n_ctx, final=final),
        grid=(t // tm,),
        in_specs=[pl.BlockSpec((1, 1, 2 * tm), lambda i: (i, 0, 0), memory_space=pltpu.SMEM),
                  pl.BlockSpec((1, 1, 2 * tm), lambda i: (jnp.minimum(i + 1, t // tm - 1), 0, 0), memory_space=pltpu.SMEM),
                  pl.BlockSpec(memory_space=pl.ANY),
                  pl.BlockSpec((tm, d), lambda i: (i, 0)),
                  pl.BlockSpec((8, tm), lambda i: (0, i)),
                  pl.BlockSpec((8, d), lambda i: (0, 5)),
                  pl.BlockSpec((1, d), lambda i: (0, 0))],
        out_specs=pl.BlockSpec((tm, d), out_map),
        out_shape=jax.ShapeDtypeStruct((out_rows, d), F32),
        scratch_shapes=[pltpu.VMEM((2, 2, tm, d), F32), pltpu.SemaphoreType.DMA((2,))],
        compiler_params=_params(1, 4 * tm * d * 4 + 4 * tm * d * 4 + 4 * tm * d * 4),
        name="combine",
    )(dest_blocks, dest_blocks, y_sorted, xs, ws, mod_l, final_g)


def _rope_tables(n_lat, n_ctx):
    rows = n_lat // GRID_W
    row_ids = np.repeat(np.arange(rows, dtype=np.float32), GRID_W)
    col_ids = np.tile(np.arange(GRID_W, dtype=np.float32), rows)
    n_freq = HEAD_DIM // 4
    inv = (np.float32(ROPE_THETA) ** (-np.arange(n_freq, dtype=np.float32) / np.float32(n_freq))).astype(np.float32)
    ar = row_ids[:, None] * inv
    ac = col_ids[:, None] * inv
    ang = np.concatenate([ar, ar, ac, ac], axis=-1).astype(np.float32)
    sign = np.where((np.arange(HEAD_DIM) % (HEAD_DIM // 2)) < HEAD_DIM // 4, -1.0, 1.0).astype(np.float32)
    cos = np.concatenate([np.ones((n_ctx, HEAD_DIM), np.float32), np.cos(ang)], axis=0)
    sin = np.concatenate([np.zeros((n_ctx, HEAD_DIM), np.float32), np.sin(ang) * sign], axis=0)
    return jnp.asarray(cos, F32), jnp.asarray(sin, F32)


def _moe_plan(counts, n_blk):
    blk = MOE_BLOCK
    n_exp = counts.shape[0]
    padded = (counts + blk - 1) // blk * blk
    pad_end = jnp.cumsum(padded)
    pad_start = pad_end - padded
    n_used = jnp.maximum(pad_end[-1] // blk, 1)
    b = jnp.minimum(jnp.arange(n_blk, dtype=jnp.int32), n_used - 1)
    blk_e = jnp.sum((pad_end[None, :] <= (b * blk)[:, None]).astype(jnp.int32), axis=1)
    blk_e = jnp.minimum(blk_e, n_exp - 1)
    blk_valid = jnp.clip(counts[blk_e] - (b * blk - pad_start[blk_e]), 0, blk).astype(jnp.int32)
    live = counts > 0
    seg_of_e = jnp.cumsum(live.astype(jnp.int32)) - 1
    n_seg = jnp.maximum(seg_of_e[-1] + 1, 1)
    ks = jnp.arange(n_exp, dtype=jnp.int32)
    seg_e = jnp.sum(jnp.where(live[None, :] & (seg_of_e[None, :] == ks[:, None]), ks[None, :], 0), axis=1)
    blk_first = jnp.concatenate([jnp.ones((1,), jnp.int32), (blk_e[1:] != blk_e[:-1]).astype(jnp.int32)])
    return {
        "pad_start": pad_start.astype(jnp.int32),
        "blk_valid": blk_valid,
        "blk_seg": seg_of_e[blk_e].astype(jnp.int32),
        "blk_first": blk_first,
        "seg_e": jnp.concatenate([seg_e, seg_e[-1:]]).astype(jnp.int32),
        "cnt": jnp.stack([n_used, n_seg]).astype(jnp.int32),
    }


def kernel(x, c, ctx, c_ctx, w_ada, b_ada, norm1_g, norm2_g, w_in, pool_w, pool_scale, q_norm_g, k_norm_g,
           sink, w_branch, w_out, router_w, router_b, w_gate, w_up, w_down, final_g):
    assert x.shape[0] == 1, "single-sequence kernel"
    n_lat, d = x.shape[1], x.shape[2]
    n_ctx = ctx.shape[1]
    t = n_ctx + n_lat
    mix = d // 2
    depth = w_in.shape[0]
    n_exp = router_w.shape[1]
    tok = _pick(n_ctx, (256, 128))
    assert (2 * t) % MOE_BLOCK == 0
    n_blk = 2 * t // MOE_BLOCK + n_exp
    cos, sin_signed = _rope_tables(n_lat, n_ctx)
    mod = _ada(c, c_ctx, w_ada, b_ada)
    xs = jnp.concatenate([ctx[0], x[0]], axis=0)
    for l in range(depth):
        last = l == depth - 1
        p = _inproj(xs, mod[l], norm1_g[l][None], w_in, l, cos, sin_signed,
                    q_norm_g[l][None], k_norm_g[l][None], n_ctx)
        og = _gattn(p, q_norm_g[l], n_ctx, mix)
        ow = _wattn(p, sink[l], n_ctx, mix)
        z = _merge(p, og, ow, pool_w, pool_scale[l][None], w_branch, l, n_ctx, d)
        xs, h2, ints, ws, cnt = _outproj_router(z, w_out, l, xs, mod[l], norm2_g[l][None], router_w, router_b, n_ctx)
        plan = _moe_plan(cnt[:, 0].astype(jnp.int32), n_blk)
        dest_blocks = _dest_blocks(ints, plan["pad_start"], tok)
        xs_sorted = _dispatch(h2, dest_blocks, n_blk * MOE_BLOCK, tok)
        y_sorted = _ffn(xs_sorted, plan, w_gate, w_up, w_down, l)
        xs = _combine(y_sorted, dest_blocks, xs, ws, mod[l], final_g[None], n_ctx, tok, last)
    return xs[None]
```

```python
import functools

import jax
import jax.numpy as jnp
import numpy as np
from jax import lax
from jax.experimental import pallas as pl
from jax.experimental.pallas import tpu as pltpu

HEAD_DIM = 128
LANES = 128
BF16_ROWS = 16
KV_HEADS = 2
GRID_W = 64
WINDOW = 128
POOL_WINDOWS = (2, 4, 8, 16)
POOL_HALO = 16
N_GROUPS = 4
ROPE_THETA = 10000.0
RMS_EPS = 1e-6
NEG_INF = -1e30
ATTN_SCALE = HEAD_DIM ** -0.5
LOG2E = 1.4426950408889634
SOFTMAX_SAFE_LOG2 = 50.0
MOE_BLOCK = 256
DMA_UNROLL = 8
VMEM_CAP = 60 * 1024 * 1024

BF16 = jnp.bfloat16
F32 = jnp.float32


def _pick(n, prefs):
    for p in prefs:
        if n % p == 0:
            return p
    raise ValueError(f"no tile in {prefs} divides {n}")


def _largest_tile(n, cap, mult):
    return max(k for k in range(mult, min(n, cap) + 1, mult) if n % k == 0)


def _params(n_axes, vmem_bytes):
    limit = int(min(max(vmem_bytes * 5 // 4 + (4 << 20), 16 << 20), VMEM_CAP))
    return pltpu.CompilerParams(dimension_semantics=("arbitrary",) * n_axes, vmem_limit_bytes=limit)


def _sigmoid(x):
    return 1.0 / (1.0 + jnp.exp(-x))


def _sigmoid_tanh(x):
    return 0.5 * jnp.tanh(0.5 * x) + 0.5


def _dot(a, b):
    return jnp.dot(a, b, preferred_element_type=F32)


def _dot_nt(a, b):
    return lax.dot_general(a, b, (((1,), (1,)), ((), ())), preferred_element_type=F32)


def _row_select(row0, n_rows, n_ctx, mod_ref):
    rows = row0 + lax.broadcasted_iota(jnp.int32, (n_rows, 1), 0)
    return jnp.where(rows < n_ctx, mod_ref[1:2, :], mod_ref[0:1, :])


def _norm_modulate(x, g_ref, shift, scale):
    ms = jnp.mean(x * x, axis=-1, keepdims=True)
    y = x * lax.rsqrt(ms + RMS_EPS) * g_ref[...]
    return y * (1.0 + scale) + shift


def _ada_kernel(cb_ref, w_ref, b_ref, o_ref, *, tn):
    s = cb_ref[...]
    s = s * _sigmoid(s)
    outs = []
    for r in range(2):
        cols = [jnp.sum(w_ref[:, c * LANES:(c + 1) * LANES] * s[r], axis=0, keepdims=True)
                for c in range(tn // LANES)]
        outs.append(jnp.concatenate(cols, axis=1) + b_ref[...])
    o_ref[...] = jnp.concatenate(outs + [jnp.zeros((6, tn), F32)], axis=0)


def _ada(c, c_ctx, w_ada, b_ada):
    depth, d, w6 = w_ada.shape
    tn = _pick(w6, (1024, 512, 256, 128))
    cb = jnp.broadcast_to(jnp.stack([c[0], c_ctx])[:, :, None], (2, d, LANES))
    vmem = 2 * d * tn * 4 + 2 * d * LANES * 4 * 2 + d * LANES * 4 * 4
    return pl.pallas_call(
        functools.partial(_ada_kernel, tn=tn),
        grid=(depth, w6 // tn),
        in_specs=[pl.BlockSpec((2, d, LANES), lambda l, j: (0, 0, 0)),
                  pl.BlockSpec((None, d, tn), lambda l, j: (l, 0, j)),
                  pl.BlockSpec((None, 1, tn), lambda l, j: (l, 0, j))],
        out_specs=pl.BlockSpec((None, 8, tn), lambda l, j: (l, 0, j)),
        out_shape=jax.ShapeDtypeStruct((depth, 8, w6), F32),
        compiler_params=_params(2, vmem),
        name="ada",
    )(cb, w_ada, b_ada.reshape(depth, 1, w6))


def _rope(x, cos, sin_signed, first_half):
    rot = jnp.where(first_half, pltpu.roll(x, 3 * HEAD_DIM // 4, axis=1), pltpu.roll(x, HEAD_DIM // 4, axis=1))
    return x * cos + rot * sin_signed


def _head_rms(x, g_ref):
    ms = jnp.mean(x * x, axis=-1, keepdims=True)
    return x * lax.rsqrt(ms + RMS_EPS) * g_ref[...]


def _norm1_kernel(x_ref, shift_ref, scale_ref, g_ref, h_ref, *, tm, n_ctx):
    i = pl.program_id(0)

    def normed():
        x = x_ref[...]
        return x * lax.rsqrt(jnp.mean(x * x, axis=-1, keepdims=True) + RMS_EPS)

    has_ctx_rows = i * tm < n_ctx

    @pl.when(has_ctx_rows)
    def _():
        gain = g_ref[...] * (1.0 + scale_ref[0:2, :])
        rows = i * tm + lax.broadcasted_iota(jnp.int32, (tm, 1), 0)
        is_ctx = rows < n_ctx
        h = normed() * jnp.where(is_ctx, gain[1:2], gain[0:1]) + jnp.where(is_ctx, shift_ref[1:2, :], shift_ref[0:1, :])
        h_ref[...] = h.astype(BF16)

    @pl.when(jnp.logical_not(has_ctx_rows))
    def _():
        h = normed() * (g_ref[...] * (1.0 + scale_ref[0:1, :])) + shift_ref[0:1, :]
        h_ref[...] = h.astype(BF16)


def _norm1(xs, mod_l, g1, n_ctx):
    t, d = xs.shape
    tm = _pick(t, (768, 512, 256, 128))
    return pl.pallas_call(
        functools.partial(_norm1_kernel, tm=tm, n_ctx=n_ctx),
        grid=(t // tm,),
        in_specs=[pl.BlockSpec((tm, d), lambda i: (i, 0)),
                  pl.BlockSpec((8, d), lambda i: (0, 0)),
                  pl.BlockSpec((8, d), lambda i: (0, 1)),
                  pl.BlockSpec((1, d), lambda i: (0, 0))],
        out_specs=pl.BlockSpec((tm, d), lambda i: (i, 0)),
        out_shape=jax.ShapeDtypeStruct((t, d), BF16),
        compiler_params=_params(1, 2 * tm * d * 4 + 2 * tm * d * 2 + 4 * tm * d * 4),
        name="norm1",
    )(xs, mod_l, mod_l, g1)


def _inproj_kernel(h_ref, w_ref, cos_ref, sin_ref, qg_ref, kg_ref, o_ref, w_sc, *, signatures):
    j = pl.program_id(0)
    i = pl.program_id(1)

    @pl.when(i == 0)
    def _():
        w_sc[...] = w_ref[...].astype(BF16)

    acc = _dot(h_ref[...], w_sc[...])

    lane = lax.broadcasted_iota(jnp.int32, (1, HEAD_DIM), 1)
    first_half = (lane % (HEAD_DIM // 2)) < (HEAD_DIM // 4)

    for sig, js in signatures:
        cond = functools.reduce(jnp.logical_or, [j == jj for jj in js])

        @pl.when(cond)
        def _(sig=sig):
            if all(t == "plain" for t in sig):
                o_ref[...] = acc.astype(BF16)
                return
            for bi, typ in enumerate(sig):
                a = acc[:, bi * LANES:(bi + 1) * LANES]
                if typ != "plain":
                    if typ in ("qnr", "knr"):
                        a = _head_rms(a, qg_ref if typ == "qnr" else kg_ref)
                    a = _rope(a, cos_ref[...], sin_ref[...], first_half)
                    if typ in ("qnr", "qr"):
                        a = a * (ATTN_SCALE * LOG2E)
                o_ref[:, bi * LANES:(bi + 1) * LANES] = a.astype(BF16)


def _col_types(mix, d):
    hq = mix // HEAD_DIM
    types = (["plain"] * hq + ["qnr"] * hq + ["knr"] * KV_HEADS + ["plain"] * KV_HEADS
             + ["qr"] * hq + ["kr"] * KV_HEADS + ["plain"] * KV_HEADS + ["plain"] * (3 * d // LANES))
    return types


def _inproj(h, w_in, layer, cos, sin_signed, qg, kg):
    t, d = h.shape
    in_w = w_in.shape[-1]
    mix = d // 2
    tm = _pick(t, (768, 512, 256, 128))
    tn = _pick(in_w, (1280, 1024, 512, 256))
    types = _col_types(mix, d)
    assert len(types) * LANES == in_w
    per = tn // LANES
    sigs = {}
    for jj in range(in_w // tn):
        sigs.setdefault(tuple(types[jj * per:(jj + 1) * per]), []).append(jj)
    signatures = tuple((s, tuple(js)) for s, js in sigs.items())
    vmem = (2 * tm * d * 2 + 2 * d * tn * 4 + d * tn * 2 + 2 * tm * tn * 2 + tm * tn * 4 * 2 + 4 * tm * LANES * 4)
    return pl.pallas_call(
        functools.partial(_inproj_kernel, signatures=signatures),
        grid=(in_w // tn, t // tm),
        in_specs=[pl.BlockSpec((tm, d), lambda j, i: (i, 0)),
                  pl.BlockSpec((None, d, tn), lambda j, i: (layer, 0, j)),
                  pl.BlockSpec((tm, HEAD_DIM), lambda j, i: (i, 0)),
                  pl.BlockSpec((tm, HEAD_DIM), lambda j, i: (i, 0)),
                  pl.BlockSpec((1, HEAD_DIM), lambda j, i: (0, 0)),
                  pl.BlockSpec((1, HEAD_DIM), lambda j, i: (0, 0))],
        out_specs=pl.BlockSpec((tm, tn), lambda j, i: (i, j)),
        out_shape=jax.ShapeDtypeStruct((t, in_w), BF16),
        scratch_shapes=[pltpu.VMEM((d, tn), BF16)],
        compiler_params=_params(2, vmem),
        name="inproj",
    )(h, w_in, cos, sin_signed, qg, kg)


def _lane_repeat(x, n):
    return jnp.concatenate([x] * n, axis=1)


def _stack_heads(q_ref, g):
    return jnp.concatenate([q_ref[:, h * HEAD_DIM:(h + 1) * HEAD_DIM] for h in range(g)], axis=0)


def _unstack_store(o_ref, out, g, tq):
    for h in range(g):
        o_ref[:, h * HEAD_DIM:(h + 1) * HEAD_DIM] = out[h * tq:(h + 1) * tq, :].astype(o_ref.dtype)


def _gattn_kernel(qmax_ref, q_ref, k_ref, v_ref, o_ref, vx_sc, kmax_sc, flag_sc, m_sc, acc_sc, *, tq, tk, g, n_ctx, n_lat):
    qi = pl.program_id(1)
    rows = g * tq

    @pl.when(qi == 0)
    def _():
        vx_sc[:, 0:HEAD_DIM] = v_ref[...]
        vx_sc[:, HEAD_DIM:] = jnp.ones((vx_sc.shape[0], HEAD_DIM), BF16)
        kf = k_ref[...].astype(F32)
        k2 = jnp.max(jnp.sum(kf * kf, axis=-1, keepdims=True), axis=0, keepdims=True)
        kmax = jnp.broadcast_to(jnp.sqrt(k2), kmax_sc.shape)
        kmax_sc[...] = kmax
        flag_sc[0] = (qmax_ref[0] * jnp.max(kmax) <= SOFTMAX_SAFE_LOG2).astype(jnp.int32)

    bounded = flag_sc[0] == 1
    ctx_only = qi * tq < n_ctx
    ctx_chunk = [(0, n_ctx)]
    all_chunks = ctx_chunk + [(n_ctx + c * tk, tk) for c in range(n_lat // tk)]

    def finish():
        acc = acc_sc[...]
        _unstack_store(o_ref, acc[:, :HEAD_DIM] / acc[:, HEAD_DIM:], g, tq)

    def attend_bounded(chunks):
        q = _stack_heads(q_ref, g)
        qf = q.astype(F32)
        bound = jnp.sqrt(jnp.sum(qf * qf, axis=-1, keepdims=True)) * kmax_sc[0:1, :]
        for n, (lo, size) in enumerate(chunks):
            s = _dot_nt(q, k_ref[lo:lo + size, :])
            p = jnp.exp2(s - _lane_repeat(bound, size // LANES))
            pv = _dot(p.astype(BF16), vx_sc[lo:lo + size, :])
            acc_sc[...] = pv if n == 0 else acc_sc[...] + pv
        finish()

    def attend_online(chunks):
        q = _stack_heads(q_ref, g)
        for n, (lo, size) in enumerate(chunks):
            s = _dot_nt(q, k_ref[lo:lo + size, :])
            mx = jnp.max(s, axis=-1, keepdims=True)
            if n == 0:
                m_new = jnp.broadcast_to(mx, (rows, LANES))
            else:
                m_prev = m_sc[...]
                m_new = jnp.maximum(m_prev, mx)
            p = jnp.exp2(s - _lane_repeat(m_new, size // LANES))
            pv = _dot(p.astype(BF16), vx_sc[lo:lo + size, :])
            if n == 0:
                acc_sc[...] = pv
            else:
                acc_sc[...] = _lane_repeat(jnp.exp2(m_prev - m_new), 2) * acc_sc[...] + pv
            m_sc[...] = m_new
        finish()

    for use_bound, attend in ((True, attend_bounded), (False, attend_online)):
        path = bounded if use_bound else jnp.logical_not(bounded)
        pl.when(path & ctx_only)(functools.partial(attend, ctx_chunk))
        pl.when(path & jnp.logical_not(ctx_only))(functools.partial(attend, all_chunks))


def _gattn(p, q_gain, n_ctx, mix):
    t = p.shape[0]
    qmax = (1.02 * HEAD_DIM ** 0.5 * ATTN_SCALE * LOG2E) * jnp.max(jnp.abs(q_gain)).reshape(1)
    hq = mix // HEAD_DIM
    g = hq // KV_HEADS
    n_lat = t - n_ctx
    tq = _pick(n_ctx, (256, 128))
    assert t % tq == 0
    tk = _pick(n_lat, (2048, 1024, 512, 256, 128))
    gw = g * HEAD_DIM
    off_q = mix // gw
    off_k = (mix + hq * HEAD_DIM) // HEAD_DIM
    off_v = off_k + KV_HEADS
    rows = g * tq
    vmem = (2 * tq * gw * 2 * 2 + 2 * 2 * t * HEAD_DIM * 2 + t * 2 * HEAD_DIM * 2 + rows * LANES * 4 * 4
            + rows * max(tk, n_ctx) * 4 * 3)
    return pl.pallas_call(
        functools.partial(_gattn_kernel, tq=tq, tk=tk, g=g, n_ctx=n_ctx, n_lat=n_lat),
        grid=(KV_HEADS, t // tq),
        in_specs=[pl.BlockSpec(memory_space=pltpu.SMEM),
                  pl.BlockSpec((tq, gw), lambda h, i: (i, off_q + h)),
                  pl.BlockSpec((t, HEAD_DIM), lambda h, i: (0, off_k + h)),
                  pl.BlockSpec((t, HEAD_DIM), lambda h, i: (0, off_v + h))],
        out_specs=pl.BlockSpec((tq, gw), lambda h, i: (i, h)),
        out_shape=jax.ShapeDtypeStruct((t, mix), BF16),
        scratch_shapes=[pltpu.VMEM((t, 2 * HEAD_DIM), BF16), pltpu.VMEM((8, LANES), F32), pltpu.SMEM((1,), jnp.int32),
                        pltpu.VMEM((rows, LANES), F32), pltpu.VMEM((rows, 2 * HEAD_DIM), F32)],
        compiler_params=_params(2, vmem),
        name="gattn",
    )(qmax, p, p, p)


def _wattn_kernel(*refs, tq, n_sub, g, n_ctx, span, t):
    q_refs, k_refs, v_refs = refs[0:KV_HEADS], refs[KV_HEADS:2 * KV_HEADS], refs[2 * KV_HEADS:3 * KV_HEADS]
    sink_ref, o_ref, vx_sc = refs[3 * KV_HEADS:]
    qi = pl.program_id(0)

    @pl.when(qi == 0)
    def _():
        for h in range(KV_HEADS):
            vx_sc[h, :, 0:HEAD_DIM] = v_refs[h][...]
            vx_sc[h, :, HEAD_DIM:] = jnp.ones((t, HEAD_DIM), BF16)

    for s in range(n_sub):
        for h in range(KV_HEADS):
            _wattn_head(qi * n_sub + s, q_refs[h].at[s * tq:(s + 1) * tq], k_refs[h], vx_sc.at[h], sink_ref.at[h],
                        o_ref.at[s * tq:(s + 1) * tq, h * g * HEAD_DIM:(h + 1) * g * HEAD_DIM],
                        tq=tq, g=g, n_ctx=n_ctx, span=span, t=t)


def _wattn_head(qi, q_ref, k_ref, vx_sc, sink_ref, o_ref, *, tq, g, n_ctx, span, t):
    q = _stack_heads(q_ref, g)
    rows = q.shape[0]
    start = pl.multiple_of(jnp.clip(qi * tq - WINDOW, 0, t - span), LANES)

    far = t + 4 * WINDOW
    qrow = qi * tq + lax.broadcasted_iota(jnp.int32, (tq, 1), 0)
    qrow = jnp.concatenate([jnp.where(qrow >= n_ctx, qrow, -far)] * g, axis=0)
    krow = start + lax.broadcasted_iota(jnp.int32, (1, span), 1)
    krow = jnp.where(krow >= n_ctx, krow, far)
    keep = jnp.abs(krow - qrow) <= WINDOW

    s_c = _dot_nt(q, k_ref[0:n_ctx, :])
    s_w = jnp.where(keep, _dot_nt(q, k_ref[pl.ds(start, span), :]), NEG_INF)
    sink = sink_ref[...] * LOG2E
    mx = jnp.maximum(jnp.maximum(jnp.max(s_c, axis=-1, keepdims=True), jnp.max(s_w, axis=-1, keepdims=True)), sink)
    m = jnp.broadcast_to(mx, (rows, LANES))
    p_c = jnp.exp2(s_c - _lane_repeat(m, n_ctx // LANES))
    p_w = jnp.exp2(s_w - _lane_repeat(m, span // LANES))
    acc = _dot(p_c.astype(BF16), vx_sc[0:n_ctx, :]) + _dot(p_w.astype(BF16), vx_sc[pl.ds(start, span), :])
    denom = acc[:, HEAD_DIM:] + jnp.exp2(sink - m)
    _unstack_store(o_ref, acc[:, :HEAD_DIM] / denom, g, tq)


def _wattn(p, sink_l, n_ctx, mix):
    t = p.shape[0]
    hq = mix // HEAD_DIM
    g = hq // KV_HEADS
    tq = _pick(n_ctx, (256, 128))
    span = tq + 2 * WINDOW
    assert t % tq == 0 and t >= span
    gw = g * HEAD_DIM
    base = mix + hq * HEAD_DIM + 2 * KV_HEADS * HEAD_DIM
    off_q = base // gw
    off_k = (base + hq * HEAD_DIM) // HEAD_DIM
    off_v = off_k + KV_HEADS
    rows = g * tq
    sink_rows = jnp.repeat(sink_l.reshape(KV_HEADS, g), tq, axis=1).reshape(KV_HEADS, rows, 1)
    n_sub = 3 if (t // tq) % 3 == 0 else 1
    vmem = KV_HEADS * (2 * n_sub * tq * gw * 2 * 2 + 2 * 2 * t * HEAD_DIM * 2 + 2 * rows * LANES * 4
                       + n_sub * rows * (span + n_ctx) * 4 * 3 + t * 2 * HEAD_DIM * 2)
    heads = range(KV_HEADS)
    return pl.pallas_call(
        functools.partial(_wattn_kernel, tq=tq, n_sub=n_sub, g=g, n_ctx=n_ctx, span=span, t=t),
        grid=(t // (tq * n_sub),),
        in_specs=([pl.BlockSpec((tq * n_sub, gw), lambda i, h=h: (i, off_q + h)) for h in heads]
                  + [pl.BlockSpec((t, HEAD_DIM), lambda i, h=h: (0, off_k + h)) for h in heads]
                  + [pl.BlockSpec((t, HEAD_DIM), lambda i, h=h: (0, off_v + h)) for h in heads]
                  + [pl.BlockSpec((KV_HEADS, rows, 1), lambda i: (0, 0, 0))]),
        out_specs=pl.BlockSpec((tq * n_sub, mix), lambda i: (i, 0)),
        out_shape=jax.ShapeDtypeStruct((t, mix), BF16),
        scratch_shapes=[pltpu.VMEM((KV_HEADS, t, 2 * HEAD_DIM), BF16)],
        compiler_params=_params(1, vmem),
        name="wattn",
    )(*([p] * (3 * KV_HEADS)), sink_rows)


def _merge_kernel(u_ref, up_ref, un_ref, og_ref, ow_ref, g0_ref, g1_ref, g2_ref, pw_ref, ps_ref, wb_ref,
                  z_ref, ext_sc, pool_sc, *, tm, n_ctx, t, mix):
    i = pl.program_id(0)
    j = pl.program_id(1)
    gw = mix // len(POOL_WINDOWS)

    def pool(clipped):
        ext_sc[0:POOL_HALO, :] = up_ref[...].astype(F32)
        ext_sc[POOL_HALO:POOL_HALO + tm, :] = u_ref[...].astype(F32)
        ext_sc[POOL_HALO + tm:, :] = un_ref[...].astype(F32)
        r = i * tm + lax.broadcasted_iota(jnp.int32, (tm, 1), 0)
        r_ctx = r < n_ctx
        for gi, w in enumerate(POOL_WINDOWS):
            c0, c1 = gi * gw, (gi + 1) * gw
            tot = jnp.zeros((tm, gw), F32)
            cnt = jnp.zeros((tm, 1), F32)
            for off in range(-((w - 1) // 2), w // 2 + 1):
                part = ext_sc[POOL_HALO + off:POOL_HALO + off + tm, c0:c1]
                if clipped:
                    rr = r + off
                    ok = (rr >= 0) & (rr < t) & ((rr < n_ctx) == r_ctx)
                    part = jnp.where(ok, part, 0.0)
                    cnt = cnt + ok.astype(F32)
                tot = tot + part
            mean = tot / cnt if clipped else tot * (1.0 / w)
            dlt = mean - ext_sc[POOL_HALO:POOL_HALO + tm, c0:c1]
            y = _dot(dlt.astype(BF16), pw_ref[gi].astype(BF16)) * ps_ref[:, c0:c1]
            pool_sc[:, c0:c1] = y.astype(BF16)

    reach = max(POOL_WINDOWS) // 2
    interior = (i * tm - reach >= n_ctx) & ((i + 1) * tm + reach <= t)
    pl.when((j == 0) & interior)(functools.partial(pool, False))
    pl.when((j == 0) & jnp.logical_not(interior))(functools.partial(pool, True))

    def gate(g_ref):
        return _sigmoid_tanh(g_ref[...].astype(F32))

    z = gate(g0_ref) * _dot(pool_sc[...], wb_ref[0].astype(BF16))
    z = z + gate(g1_ref) * _dot(og_ref[...], wb_ref[1].astype(BF16))
    z = z + gate(g2_ref) * _dot(ow_ref[...], wb_ref[2].astype(BF16))
    z_ref[...] = z.astype(BF16)


def _merge(p, og, ow, pool_w, pool_scale, w_branch, layer, n_ctx, d):
    t = p.shape[0]
    mix = d // 2
    tm = _largest_tile(t, 1100, POOL_HALO)
    tn = _pick(d, (512, 256, 128))
    hb = tm // POOL_HALO
    n_hb = t // POOL_HALO
    gate0 = (p.shape[1] - 3 * d) // tn
    gs = pool_w.shape[-1]
    vmem = (2 * 3 * tm * mix * 2 + 2 * 3 * tm * tn * 2 + 2 * 3 * mix * tn * 4 + 3 * mix * tn * 2
            + (tm + 2 * POOL_HALO) * mix * 4 + tm * mix * 2 + 2 * tm * tn * 2 + 4 * tm * tn * 4
            + 2 * len(POOL_WINDOWS) * gs * gs * 4 + 6 * tm * gs * 4)
    return pl.pallas_call(
        functools.partial(_merge_kernel, tm=tm, n_ctx=n_ctx, t=t, mix=mix),
        grid=(t // tm, d // tn),
        in_specs=[pl.BlockSpec((tm, mix), lambda i, j: (i, 0)),
                  pl.BlockSpec((POOL_HALO, mix), lambda i, j: (jnp.maximum(i * hb - 1, 0), 0)),
                  pl.BlockSpec((POOL_HALO, mix), lambda i, j: (jnp.minimum((i + 1) * hb, n_hb - 1), 0)),
                  pl.BlockSpec((tm, mix), lambda i, j: (i, 0)),
                  pl.BlockSpec((tm, mix), lambda i, j: (i, 0)),
                  pl.BlockSpec((tm, tn), lambda i, j: (i, gate0 + j)),
                  pl.BlockSpec((tm, tn), lambda i, j: (i, gate0 + d // tn + j)),
                  pl.BlockSpec((tm, tn), lambda i, j: (i, gate0 + 2 * (d // tn) + j)),
                  pl.BlockSpec((None, len(POOL_WINDOWS), gs, gs), lambda i, j: (layer, 0, 0, 0)),
                  pl.BlockSpec((1, mix), lambda i, j: (0, 0)),
                  pl.BlockSpec((None, 3, mix, tn), lambda i, j: (layer, 0, 0, j))],
        out_specs=pl.BlockSpec((tm, tn), lambda i, j: (i, j)),
        out_shape=jax.ShapeDtypeStruct((t, d), BF16),
        scratch_shapes=[pltpu.VMEM((tm + 2 * POOL_HALO, mix), F32), pltpu.VMEM((tm, mix), BF16)],
        compiler_params=_params(2, vmem),
        name="merge",
    )(p, p, p, og, ow, p, p, p, pool_w, pool_scale, w_branch)


W_CHUNK = 256


def _route(h, rwt_ref, rb_ref, ints_ref, ws_ref, cnt_ref, carry_sc, *, tm, n_exp):
    scores = _sigmoid(_dot_nt(rwt_ref[...].astype(BF16), h.astype(BF16)))
    sel = scores + rb_ref[...]
    per = n_exp // N_GROUPS
    sub = lax.broadcasted_iota(jnp.int32, (per, tm), 0)

    def top2(v):
        m1 = jnp.max(v, axis=0, keepdims=True)
        i1 = jnp.min(jnp.where(v == m1, sub, per), axis=0, keepdims=True)
        rest = jnp.where(sub == i1, -jnp.inf, v)
        m2 = jnp.max(rest, axis=0, keepdims=True)
        i2 = jnp.min(jnp.where(rest == m2, sub, per), axis=0, keepdims=True)
        return m1 + m2, i1, i2

    tops = [top2(sel[gi * per:(gi + 1) * per, :]) for gi in range(N_GROUPS)]
    best, l1, l2 = tops[0]
    grp = jnp.zeros((1, tm), jnp.int32)
    for gi in range(1, N_GROUPS):
        gs, a1, a2 = tops[gi]
        better = gs > best
        best = jnp.where(better, gs, best)
        grp = jnp.where(better, gi, grp)
        l1 = jnp.where(better, a1, l1)
        l2 = jnp.where(better, a2, l2)

    hot1 = [(grp == gi) & (sub == l1) for gi in range(N_GROUPS)]
    hot2 = [(grp == gi) & (sub == l2) for gi in range(N_GROUPS)]
    assign = jnp.concatenate([(a | b).astype(F32) for a, b in zip(hot1, hot2)], axis=0)

    before = (lax.broadcasted_iota(jnp.int32, (tm, tm), 0) < lax.broadcasted_iota(jnp.int32, (tm, tm), 1))
    pos = _dot(assign.astype(BF16), before.astype(F32).astype(BF16)) + carry_sc[...]

    def pick(hots, val):
        return sum(jnp.sum(jnp.where(hots[gi], val[gi * per:(gi + 1) * per, :], 0.0), axis=0, keepdims=True)
                   for gi in range(N_GROUPS))

    s1 = pick(hot1, scores)
    s2 = pick(hot2, scores)
    r1 = pick(hot1, pos)
    r2 = pick(hot2, pos)
    tot = s1 + s2
    ints_ref[0:1, :] = grp * per + l1
    ints_ref[1:2, :] = grp * per + l2
    ints_ref[2:3, :] = r1.astype(jnp.int32)
    ints_ref[3:4, :] = r2.astype(jnp.int32)
    ints_ref[4:8, :] = jnp.zeros((4, tm), jnp.int32)
    ws_ref[0:1, :] = s1 / tot
    ws_ref[1:2, :] = s2 / tot
    ws_ref[2:8, :] = jnp.zeros((6, tm), F32)
    carry_sc[...] = carry_sc[...] + jnp.sum(assign, axis=1, keepdims=True)
    cnt_ref[...] = jnp.broadcast_to(carry_sc[...], cnt_ref.shape)


def _outproj_router_kernel(z_ref, w_hbm, xs_ref, gate_ref, shift_ref, scale_ref, g_ref, rwt_ref, rb_ref,
                           o_ref, h_ref, ints_ref, ws_ref, cnt_ref, w_sc, stage, sem, carry_sc,
                           *, tm, n_ctx, n_exp, layer):
    i = pl.program_id(0)
    d = w_sc.shape[0]

    @pl.when(i == 0)
    def _():
        carry_sc[...] = jnp.zeros(carry_sc.shape, F32)
        n_chunks = d // W_CHUNK

        def chunk(c):
            return pltpu.make_async_copy(w_hbm.at[layer, :, c * W_CHUNK:(c + 1) * W_CHUNK], stage.at[c % 2], sem.at[c % 2])

        chunk(0).start()
        for c in range(n_chunks):
            if c + 1 < n_chunks:
                chunk(c + 1).start()
            chunk(c).wait()
            w_sc[:, c * W_CHUNK:(c + 1) * W_CHUNK] = stage[c % 2].astype(BF16)

    gate = _row_select(i * tm, tm, n_ctx, gate_ref)
    x = xs_ref[...] + gate * _dot(z_ref[...], w_sc[...])
    o_ref[...] = x
    shift = _row_select(i * tm, tm, n_ctx, shift_ref)
    scale = _row_select(i * tm, tm, n_ctx, scale_ref)
    h = _norm_modulate(x, g_ref, shift, scale)
    h_ref[...] = h
    _route(h, rwt_ref, rb_ref, ints_ref, ws_ref, cnt_ref, carry_sc, tm=tm, n_exp=n_exp)


def _outproj_router(z, w_out, layer, xs, mod_l, g2, router_w, router_b, n_ctx):
    t, d = xs.shape
    n_exp = router_w.shape[1]
    tm = _largest_tile(t, 400, LANES)
    vmem = (d * d * 2 + 2 * d * W_CHUNK * 4 + 2 * tm * d * 2 + 6 * tm * d * 4 + 6 * tm * d * 4 + 3 * tm * tm * 4
            + 2 * n_exp * d * 4)
    row = lambda i: (i, 0)
    return pl.pallas_call(
        functools.partial(_outproj_router_kernel, tm=tm, n_ctx=n_ctx, n_exp=n_exp, layer=layer),
        grid=(t // tm,),
        in_specs=[pl.BlockSpec((tm, d), row),
                  pl.BlockSpec(memory_space=pl.ANY),
                  pl.BlockSpec((tm, d), row),
                  pl.BlockSpec((8, d), lambda i: (0, 2)),
                  pl.BlockSpec((8, d), lambda i: (0, 3)),
                  pl.BlockSpec((8, d), lambda i: (0, 4)),
                  pl.BlockSpec((1, d), lambda i: (0, 0)),
                  pl.BlockSpec((n_exp, d), lambda i: (0, 0)),
                  pl.BlockSpec((n_exp, 1), lambda i: (0, 0))],
        out_specs=[pl.BlockSpec((tm, d), row),
                   pl.BlockSpec((tm, d), row),
                   pl.BlockSpec((8, tm), lambda i: (0, i)),
                   pl.BlockSpec((8, tm), lambda i: (0, i)),
                   pl.BlockSpec((n_exp, LANES), lambda i: (0, 0))],
        out_shape=[jax.ShapeDtypeStruct((t, d), F32),
                   jax.ShapeDtypeStruct((t, d), F32),
                   jax.ShapeDtypeStruct((8, t), jnp.int32),
                   jax.ShapeDtypeStruct((8, t), F32),
                   jax.ShapeDtypeStruct((n_exp, LANES), F32)],
        scratch_shapes=[pltpu.VMEM((d, d), BF16), pltpu.VMEM((2, d, W_CHUNK), F32), pltpu.SemaphoreType.DMA((2,)),
                        pltpu.VMEM((n_exp, 1), F32)],
        compiler_params=_params(1, vmem),
        name="outproj_router",
    )(z, w_out, xs, mod_l, mod_l, mod_l, g2, router_w.T, router_b.reshape(n_exp, 1))


def _row_copy(src_ref, src_row, dst_ref, dst_row, sem):
    return pltpu.make_async_copy(src_ref.at[pl.ds(src_row, 1)], dst_ref.at[pl.ds(dst_row, 1)], sem)


def _dispatch_kernel(dst_ref, h_ref, xs_hbm, sem, *, tm):
    def issue(r8, carry):
        base = pl.multiple_of(r8 * DMA_UNROLL, DMA_UNROLL)
        for j in range(DMA_UNROLL):
            for k in range(2):
                _row_copy(h_ref, base + j, xs_hbm, dst_ref[0, 0, k * tm + base + j], sem).start(priority=k)
        return carry

    lax.fori_loop(0, tm // DMA_UNROLL, issue, 0)
    for k in range(2):
        pltpu.make_async_copy(h_ref, xs_hbm.at[pl.ds(0, tm)], sem).wait()


def _dest_blocks(ints, pad_start, tm):
    t = ints.shape[1]
    experts = jnp.arange(pad_start.shape[0], dtype=jnp.int32)
    start = jnp.sum(jnp.where(ints[0:2, :, None] == experts, pad_start, 0), axis=-1)
    dest = start + ints[2:4]
    return dest.reshape(2, t // tm, tm).transpose(1, 0, 2).reshape(t // tm, 1, 2 * tm)


def _dispatch(h2, dest_blocks, n_rows, tm):
    t, d = h2.shape
    return pl.pallas_call(
        functools.partial(_dispatch_kernel, tm=tm),
        grid=(t // tm,),
        in_specs=[pl.BlockSpec((1, 1, 2 * tm), lambda i: (i, 0, 0), memory_space=pltpu.SMEM),
                  pl.BlockSpec((tm, d), lambda i: (i, 0))],
        out_specs=pl.BlockSpec(memory_space=pl.ANY),
        out_shape=jax.ShapeDtypeStruct((n_rows, d), F32),
        scratch_shapes=[pltpu.SemaphoreType.DMA(())],
        compiler_params=_params(1, 2 * tm * d * 4),
        name="dispatch",
    )(dest_blocks, h2)


def _ffn_kernel(blk_valid_ref, blk_seg_ref, blk_first_ref, seg_e_ref, cnt_ref, x_ref, wg_hbm, wu_hbm, wd_hbm, y_ref,
                wg_st, wu_st, wd_st, wg_sc, wu_sc, wd_sc, sem_g, sem_u, sem_d, *, layer):
    b = pl.program_id(0)
    n_seg = cnt_ref[1]

    def copies(k):
        e = seg_e_ref[k]
        return [pltpu.make_async_copy(w.at[layer, e], st, sm)
                for w, st, sm in ((wg_hbm, wg_st, sem_g), (wu_hbm, wu_st, sem_u), (wd_hbm, wd_st, sem_d))]

    def x_block():
        rows = lax.broadcasted_iota(jnp.int32, (x_ref.shape[0], 1), 0)
        return jnp.where(rows < blk_valid_ref[b], x_ref[...], 0.0).astype(BF16)

    def finish(gte, up):
        hidden = (gte * _sigmoid(gte) * up).astype(BF16)
        y_ref[...] = _dot(hidden, wd_sc[...])

    @pl.when(b < cnt_ref[0])
    def _():
        @pl.when(b == 0)
        def _():
            for cp in copies(0):
                cp.start()

        @pl.when(blk_first_ref[b] == 1)
        def _():
            k = blk_seg_ref[b]
            for cp, nxt, st, dst in zip(copies(k), copies(k + 1), (wg_st, wu_st, wd_st), (wg_sc, wu_sc, wd_sc)):
                cp.wait()
                dst[...] = st[...].astype(BF16)
                pl.when(k + 1 < n_seg)(nxt.start)

        x = x_block()
        finish(_dot(x, wg_sc[...]), _dot(x, wu_sc[...]))


def _ffn(xs_sorted, plan, w_gate, w_up, w_down, layer):
    n_rows, d = xs_sorted.shape
    ff = w_gate.shape[-1]
    blk = MOE_BLOCK
    n_blk = n_rows // blk
    prefetch = (plan["blk_valid"], plan["blk_seg"], plan["blk_first"], plan["seg_e"], plan["cnt"])
    row_map = lambda b, v, s, f, e, c: (jnp.minimum(b, c[0] - 1), 0)
    hbm = pl.BlockSpec(memory_space=pl.ANY)
    dma = pltpu.SemaphoreType.DMA(())
    grid_spec = pltpu.PrefetchScalarGridSpec(
        num_scalar_prefetch=len(prefetch),
        grid=(n_blk,),
        in_specs=[pl.BlockSpec((blk, d), row_map), hbm, hbm, hbm],
        out_specs=pl.BlockSpec((blk, d), row_map),
        scratch_shapes=[pltpu.VMEM((d, ff), F32), pltpu.VMEM((d, ff), F32), pltpu.VMEM((ff, d), F32),
                        pltpu.VMEM((d, ff), BF16), pltpu.VMEM((d, ff), BF16), pltpu.VMEM((ff, d), BF16),
                        dma, dma, dma],
    )
    vmem = 3 * d * ff * (4 + 2) + 4 * blk * d * 4 + blk * d * 2 + 3 * blk * ff * 4 + blk * d * 4
    return pl.pallas_call(
        functools.partial(_ffn_kernel, layer=layer),
        grid_spec=grid_spec,
        out_shape=jax.ShapeDtypeStruct((n_rows, d), F32),
        compiler_params=pltpu.CompilerParams(dimension_semantics=("arbitrary",), vmem_limit_bytes=min(vmem + (6 << 20), VMEM_CAP)),
        name="ffn",
    )(*prefetch, xs_sorted, w_gate, w_up, w_down)


def _combine_kernel(src_ref, nxt_ref, y_hbm, xs_ref, ws_ref, gate_ref, fg_ref, o_ref, ybuf, sem, *, tm, n_ctx, final):
    i = pl.program_id(0)
    slot = i % 2

    def gather(idx_ref, into):
        def issue(r8, carry):
            base = pl.multiple_of(r8 * DMA_UNROLL, DMA_UNROLL)
            for j in range(DMA_UNROLL):
                for k in range(2):
                    _row_copy(y_hbm, idx_ref[0, 0, k * tm + base + j], ybuf.at[into, k], base + j, sem.at[into]).start(priority=k)
            return carry

        lax.fori_loop(0, tm // DMA_UNROLL, issue, 0)

    @pl.when(i == 0)
    def _():
        gather(src_ref, 0)

    @pl.when(i + 1 < pl.num_programs(0))
    def _():
        gather(nxt_ref, 1 - slot)

    for k in range(2):
        pltpu.make_async_copy(y_hbm.at[pl.ds(0, tm)], ybuf.at[slot, k], sem.at[slot]).wait()

    wcol = jnp.transpose(ws_ref[...])
    y = wcol[:, 0:1] * ybuf[slot, 0] + wcol[:, 1:2] * ybuf[slot, 1]
    x = xs_ref[...] + _row_select(i * tm, tm, n_ctx, gate_ref) * y
    if final:
        ms = jnp.mean(x * x, axis=-1, keepdims=True)
        x = x * lax.rsqrt(ms + RMS_EPS) * fg_ref[...]
    o_ref[...] = x


def _combine(y_sorted, dest_blocks, xs, ws, mod_l, final_g, n_ctx, tm, final):
    t, d = xs.shape
    if final:
        skip = n_ctx // tm
        out_rows = t - n_ctx
        out_map = lambda i: (jnp.maximum(i - skip, 0), 0)
    else:
        out_rows = t
        out_map = lambda i: (i, 0)
    return pl.pallas_call(
        functools.partial(_combine_kernel, tm=tm, n_ctx=n_ctx, final=final),
        grid=(t // tm,),
        in_specs=[pl.BlockSpec((1, 1, 2 * tm), lambda i: (i, 0, 0), memory_space=pltpu.SMEM),
                  pl.BlockSpec((1, 1, 2 * tm), lambda i: (jnp.minimum(i + 1, t // tm - 1), 0, 0), memory_space=pltpu.SMEM),
                  pl.BlockSpec(memory_space=pl.ANY),
                  pl.BlockSpec((tm, d), lambda i: (i, 0)),
                  pl.BlockSpec((8, tm), lambda i: (0, i)),
                  pl.BlockSpec((8, d), lambda i: (0, 5)),
                  pl.BlockSpec((1, d), lambda i: (0, 0))],
        out_specs=pl.BlockSpec((tm, d), out_map),
        out_shape=jax.ShapeDtypeStruct((out_rows, d), F32),
        scratch_shapes=[pltpu.VMEM((2, 2, tm, d), F32), pltpu.SemaphoreType.DMA((2,))],
        compiler_params=_params(1, 4 * tm * d * 4 + 4 * tm * d * 4 + 4 * tm * d * 4),
        name="combine",
    )(dest_blocks, dest_blocks, y_sorted, xs, ws, mod_l, final_g)


def _rope_tables(n_lat, n_ctx):
    rows = n_lat // GRID_W
    row_ids = np.repeat(np.arange(rows, dtype=np.float32), GRID_W)
    col_ids = np.tile(np.arange(GRID_W, dtype=np.float32), rows)
    n_freq = HEAD_DIM // 4
    inv = (np.float32(ROPE_THETA) ** (-np.arange(n_freq, dtype=np.float32) / np.float32(n_freq))).astype(np.float32)
    ar = row_ids[:, None] * inv
    ac = col_ids[:, None] * inv
    ang = np.concatenate([ar, ar, ac, ac], axis=-1).astype(np.float32)
    sign = np.where((np.arange(HEAD_DIM) % (HEAD_DIM // 2)) < HEAD_DIM // 4, -1.0, 1.0).astype(np.float32)
    cos = np.concatenate([np.ones((n_ctx, HEAD_DIM), np.float32), np.cos(ang)], axis=0)
    sin = np.concatenate([np.zeros((n_ctx, HEAD_DIM), np.float32), np.sin(ang) * sign], axis=0)
    return jnp.asarray(cos, F32), jnp.asarray(sin, F32)


def _moe_plan(counts, n_blk):
    blk = MOE_BLOCK
    n_exp = counts.shape[0]
    padded = (counts + blk - 1) // blk * blk
    pad_end = jnp.cumsum(padded)
    pad_start = pad_end - padded
    n_used = jnp.maximum(pad_end[-1] // blk, 1)
    b = jnp.minimum(jnp.arange(n_blk, dtype=jnp.int32), n_used - 1)
    blk_e = jnp.sum((pad_end[None, :] <= (b * blk)[:, None]).astype(jnp.int32), axis=1)
    blk_e = jnp.minimum(blk_e, n_exp - 1)
    blk_valid = jnp.clip(counts[blk_e] - (b * blk - pad_start[blk_e]), 0, blk).astype(jnp.int32)
    live = counts > 0
    seg_of_e = jnp.cumsum(live.astype(jnp.int32)) - 1
    n_seg = jnp.maximum(seg_of_e[-1] + 1, 1)
    ks = jnp.arange(n_exp, dtype=jnp.int32)
    seg_e = jnp.sum(jnp.where(live[None, :] & (seg_of_e[None, :] == ks[:, None]), ks[None, :], 0), axis=1)
    blk_first = jnp.concatenate([jnp.ones((1,), jnp.int32), (blk_e[1:] != blk_e[:-1]).astype(jnp.int32)])
    return {
        "pad_start": pad_start.astype(jnp.int32),
        "blk_valid": blk_valid,
        "blk_seg": seg_of_e[blk_e].astype(jnp.int32),
        "blk_first": blk_first,
        "seg_e": jnp.concatenate([seg_e, seg_e[-1:]]).astype(jnp.int32),
        "cnt": jnp.stack([n_used, n_seg]).astype(jnp.int32),
    }


def kernel(x, c, ctx, c_ctx, w_ada, b_ada, norm1_g, norm2_g, w_in, pool_w, pool_scale, q_norm_g, k_norm_g,
           sink, w_branch, w_out, router_w, router_b, w_gate, w_up, w_down, final_g):
    assert x.shape[0] == 1, "single-sequence kernel"
    n_lat, d = x.shape[1], x.shape[2]
    n_ctx = ctx.shape[1]
    t = n_ctx + n_lat
    mix = d // 2
    depth = w_in.shape[0]
    n_exp = router_w.shape[1]
    tok = _pick(n_ctx, (256, 128))
    assert (2 * t) % MOE_BLOCK == 0
    n_blk = 2 * t // MOE_BLOCK + n_exp
    cos, sin_signed = _rope_tables(n_lat, n_ctx)
    mod = _ada(c, c_ctx, w_ada, b_ada)
    xs = jnp.concatenate([ctx[0], x[0]], axis=0)
    for l in range(depth):
        last = l == depth - 1
        h1 = _norm1(xs, mod[l], norm1_g[l][None], n_ctx)
        p = _inproj(h1, w_in, l, cos, sin_signed, q_norm_g[l][None], k_norm_g[l][None])
        og = _gattn(p, q_norm_g[l], n_ctx, mix)
        ow = _wattn(p, sink[l], n_ctx, mix)
        z = _merge(p, og, ow, pool_w, pool_scale[l][None], w_branch, l, n_ctx, d)
        xs, h2, ints, ws, cnt = _outproj_router(z, w_out, l, xs, mod[l], norm2_g[l][None], router_w, router_b, n_ctx)
        plan = _moe_plan(cnt[:, 0].astype(jnp.int32), n_blk)
        dest_blocks = _dest_blocks(ints, plan["pad_start"], tok)
        xs_sorted = _dispatch(h2, dest_blocks, n_blk * MOE_BLOCK, tok)
        y_sorted = _ffn(xs_sorted, plan, w_gate, w_up, w_down, l)
        xs = _combine(y_sorted, dest_blocks, xs, ws, mod[l], final_g[None], n_ctx, tok, last)
    return xs[None]
```

```python
import functools

import jax
import jax.numpy as jnp
import numpy as np
from jax import lax
from jax.experimental import pallas as pl
from jax.experimental.pallas import tpu as pltpu

HEAD_DIM = 128
LANES = 128
BF16_ROWS = 16
KV_HEADS = 2
GRID_W = 64
WINDOW = 128
POOL_WINDOWS = (2, 4, 8, 16)
POOL_HALO = 16
N_GROUPS = 4
ROPE_THETA = 10000.0
RMS_EPS = 1e-6
NEG_INF = -1e30
ATTN_SCALE = HEAD_DIM ** -0.5
LOG2E = 1.4426950408889634
SOFTMAX_SAFE_LOG2 = 50.0
MOE_BLOCK = 256
DMA_UNROLL = 8
VMEM_CAP = 60 * 1024 * 1024

BF16 = jnp.bfloat16
F32 = jnp.float32


def _pick(n, prefs):
    for p in prefs:
        if n % p == 0:
            return p
    raise ValueError(f"no tile in {prefs} divides {n}")


def _largest_tile(n, cap, mult):
    return max(k for k in range(mult, min(n, cap) + 1, mult) if n % k == 0)


def _params(n_axes, vmem_bytes):
    limit = int(min(max(vmem_bytes * 5 // 4 + (4 << 20), 16 << 20), VMEM_CAP))
    return pltpu.CompilerParams(dimension_semantics=("arbitrary",) * n_axes, vmem_limit_bytes=limit)


def _sigmoid(x):
    return 1.0 / (1.0 + jnp.exp(-x))


def _sigmoid_tanh(x):
    return 0.5 * jnp.tanh(0.5 * x) + 0.5


def _dot(a, b):
    return jnp.dot(a, b, preferred_element_type=F32)


def _dot_nt(a, b):
    return lax.dot_general(a, b, (((1,), (1,)), ((), ())), preferred_element_type=F32)


def _row_select(row0, n_rows, n_ctx, mod_ref):
    rows = row0 + lax.broadcasted_iota(jnp.int32, (n_rows, 1), 0)
    return jnp.where(rows < n_ctx, mod_ref[1:2, :], mod_ref[0:1, :])


def _norm_modulate(x, g_ref, shift, scale):
    ms = jnp.mean(x * x, axis=-1, keepdims=True)
    y = x * lax.rsqrt(ms + RMS_EPS) * g_ref[...]
    return y * (1.0 + scale) + shift


def _ada_kernel(cb_ref, w_ref, b_ref, o_ref, *, tn):
    s = cb_ref[...]
    s = s * _sigmoid(s)
    outs = []
    for r in range(2):
        cols = [jnp.sum(w_ref[:, c * LANES:(c + 1) * LANES] * s[r], axis=0, keepdims=True)
                for c in range(tn // LANES)]
        outs.append(jnp.concatenate(cols, axis=1) + b_ref[...])
    o_ref[...] = jnp.concatenate(outs + [jnp.zeros((6, tn), F32)], axis=0)


def _ada(c, c_ctx, w_ada, b_ada):
    depth, d, w6 = w_ada.shape
    tn = _pick(w6, (1024, 512, 256, 128))
    cb = jnp.broadcast_to(jnp.stack([c[0], c_ctx])[:, :, None], (2, d, LANES))
    vmem = 2 * d * tn * 4 + 2 * d * LANES * 4 * 2 + d * LANES * 4 * 4
    return pl.pallas_call(
        functools.partial(_ada_kernel, tn=tn),
        grid=(depth, w6 // tn),
        in_specs=[pl.BlockSpec((2, d, LANES), lambda l, j: (0, 0, 0)),
                  pl.BlockSpec((None, d, tn), lambda l, j: (l, 0, j)),
                  pl.BlockSpec((None, 1, tn), lambda l, j: (l, 0, j))],
        out_specs=pl.BlockSpec((None, 8, tn), lambda l, j: (l, 0, j)),
        out_shape=jax.ShapeDtypeStruct((depth, 8, w6), F32),
        compiler_params=_params(2, vmem),
        name="ada",
    )(cb, w_ada, b_ada.reshape(depth, 1, w6))


def _rope(x, cos, sin_signed, first_half):
    rot = jnp.where(first_half, pltpu.roll(x, 3 * HEAD_DIM // 4, axis=1), pltpu.roll(x, HEAD_DIM // 4, axis=1))
    return x * cos + rot * sin_signed


def _head_rms(x, g_ref):
    ms = jnp.mean(x * x, axis=-1, keepdims=True)
    return x * lax.rsqrt(ms + RMS_EPS) * g_ref[...]


def _norm1_kernel(x_ref, shift_ref, scale_ref, g_ref, h_ref, *, tm, n_ctx):
    i = pl.program_id(0)

    def normed():
        x = x_ref[...]
        return x * lax.rsqrt(jnp.mean(x * x, axis=-1, keepdims=True) + RMS_EPS)

    has_ctx_rows = i * tm < n_ctx

    @pl.when(has_ctx_rows)
    def _():
        gain = g_ref[...] * (1.0 + scale_ref[0:2, :])
        rows = i * tm + lax.broadcasted_iota(jnp.int32, (tm, 1), 0)
        is_ctx = rows < n_ctx
        h = normed() * jnp.where(is_ctx, gain[1:2], gain[0:1]) + jnp.where(is_ctx, shift_ref[1:2, :], shift_ref[0:1, :])
        h_ref[...] = h.astype(BF16)

    @pl.when(jnp.logical_not(has_ctx_rows))
    def _():
        h = normed() * (g_ref[...] * (1.0 + scale_ref[0:1, :])) + shift_ref[0:1, :]
        h_ref[...] = h.astype(BF16)


def _norm1(xs, mod_l, g1, n_ctx):
    t, d = xs.shape
    tm = _pick(t, (768, 512, 256, 128))
    return pl.pallas_call(
        functools.partial(_norm1_kernel, tm=tm, n_ctx=n_ctx),
        grid=(t // tm,),
        in_specs=[pl.BlockSpec((tm, d), lambda i: (i, 0)),
                  pl.BlockSpec((8, d), lambda i: (0, 0)),
                  pl.BlockSpec((8, d), lambda i: (0, 1)),
                  pl.BlockSpec((1, d), lambda i: (0, 0))],
        out_specs=pl.BlockSpec((tm, d), lambda i: (i, 0)),
        out_shape=jax.ShapeDtypeStruct((t, d), BF16),
        compiler_params=_params(1, 2 * tm * d * 4 + 2 * tm * d * 2 + 4 * tm * d * 4),
        name="norm1",
    )(xs, mod_l, mod_l, g1)


def _stack_norm1_kernel(ctx_ref, x_ref, shift_ref, scale_ref, g_ref, xs_ref, h_ref):
    i = pl.program_id(0)

    def emit(x, row):
        xs_ref[...] = x
        n = x * lax.rsqrt(jnp.mean(x * x, axis=-1, keepdims=True) + RMS_EPS)
        h = n * (g_ref[...] * (1.0 + scale_ref[row:row + 1, :])) + shift_ref[row:row + 1, :]
        h_ref[...] = h.astype(BF16)

    @pl.when(i == 0)
    def _():
        emit(ctx_ref[...], 1)

    @pl.when(i > 0)
    def _():
        emit(x_ref[...], 0)


def _stack_norm1(ctx2d, x2d, mod_l, g1):
    n_ctx, d = ctx2d.shape
    n_lat = x2d.shape[0]
    assert n_lat % n_ctx == 0
    t = n_ctx + n_lat
    blk = pl.BlockSpec((n_ctx, d), lambda i: (i, 0))
    return pl.pallas_call(
        _stack_norm1_kernel,
        grid=(t // n_ctx,),
        in_specs=[pl.BlockSpec((n_ctx, d), lambda i: (0, 0)),
                  pl.BlockSpec((n_ctx, d), lambda i: (jnp.maximum(i - 1, 0), 0)),
                  pl.BlockSpec((8, d), lambda i: (0, 0)),
                  pl.BlockSpec((8, d), lambda i: (0, 1)),
                  pl.BlockSpec((1, d), lambda i: (0, 0))],
        out_specs=[blk, blk],
        out_shape=[jax.ShapeDtypeStruct((t, d), F32), jax.ShapeDtypeStruct((t, d), BF16)],
        compiler_params=_params(1, 6 * n_ctx * d * 4 + 2 * n_ctx * d * 2 + 4 * n_ctx * d * 4),
        name="stack_norm1",
    )(ctx2d, x2d, mod_l, mod_l, g1)


def _inproj_kernel(h_ref, w_ref, cos_ref, sin_ref, qg_ref, kg_ref, o_ref, w_sc, *, signatures):
    j = pl.program_id(0)
    i = pl.program_id(1)

    @pl.when(i == 0)
    def _():
        w_sc[...] = w_ref[...].astype(BF16)

    acc = _dot(h_ref[...], w_sc[...])

    lane = lax.broadcasted_iota(jnp.int32, (1, HEAD_DIM), 1)
    first_half = (lane % (HEAD_DIM // 2)) < (HEAD_DIM // 4)

    for sig, js in signatures:
        cond = functools.reduce(jnp.logical_or, [j == jj for jj in js])

        @pl.when(cond)
        def _(sig=sig):
            if all(t == "plain" for t in sig):
                o_ref[...] = acc.astype(BF16)
                return
            for bi, typ in enumerate(sig):
                a = acc[:, bi * LANES:(bi + 1) * LANES]
                if typ != "plain":
                    if typ in ("qnr", "knr"):
                        a = _head_rms(a, qg_ref if typ == "qnr" else kg_ref)
                    a = _rope(a, cos_ref[...], sin_ref[...], first_half)
                    if typ in ("qnr", "qr"):
                        a = a * (ATTN_SCALE * LOG2E)
                o_ref[:, bi * LANES:(bi + 1) * LANES] = a.astype(BF16)


def _col_types(mix, d):
    hq = mix // HEAD_DIM
    types = (["plain"] * hq + ["qnr"] * hq + ["knr"] * KV_HEADS + ["plain"] * KV_HEADS
             + ["qr"] * hq + ["kr"] * KV_HEADS + ["plain"] * KV_HEADS + ["plain"] * (3 * d // LANES))
    return types


def _inproj(h, w_in, layer, cos, sin_signed, qg, kg):
    t, d = h.shape
    in_w = w_in.shape[-1]
    mix = d // 2
    tm = _pick(t, (768, 512, 256, 128))
    tn = _pick(in_w, (1280, 1024, 512, 256))
    types = _col_types(mix, d)
    assert len(types) * LANES == in_w
    per = tn // LANES
    sigs = {}
    for jj in range(in_w // tn):
        sigs.setdefault(tuple(types[jj * per:(jj + 1) * per]), []).append(jj)
    signatures = tuple((s, tuple(js)) for s, js in sigs.items())
    vmem = (2 * tm * d * 2 + 2 * d * tn * 4 + d * tn * 2 + 2 * tm * tn * 2 + tm * tn * 4 * 2 + 4 * tm * LANES * 4)
    return pl.pallas_call(
        functools.partial(_inproj_kernel, signatures=signatures),
        grid=(in_w // tn, t // tm),
        in_specs=[pl.BlockSpec((tm, d), lambda j, i: (i, 0)),
                  pl.BlockSpec((None, d, tn), lambda j, i: (layer, 0, j)),
                  pl.BlockSpec((tm, HEAD_DIM), lambda j, i: (i, 0)),
                  pl.BlockSpec((tm, HEAD_DIM), lambda j, i: (i, 0)),
                  pl.BlockSpec((1, HEAD_DIM), lambda j, i: (0, 0)),
                  pl.BlockSpec((1, HEAD_DIM), lambda j, i: (0, 0))],
        out_specs=pl.BlockSpec((tm, tn), lambda j, i: (i, j)),
        out_shape=jax.ShapeDtypeStruct((t, in_w), BF16),
        scratch_shapes=[pltpu.VMEM((d, tn), BF16)],
        compiler_params=_params(2, vmem),
        name="inproj",
    )(h, w_in, cos, sin_signed, qg, kg)


def _lane_repeat(x, n):
    return jnp.concatenate([x] * n, axis=1)


def _stack_heads(q_ref, g):
    return jnp.concatenate([q_ref[:, h * HEAD_DIM:(h + 1) * HEAD_DIM] for h in range(g)], axis=0)


def _unstack_store(o_ref, out, g, tq):
    for h in range(g):
        o_ref[:, h * HEAD_DIM:(h + 1) * HEAD_DIM] = out[h * tq:(h + 1) * tq, :].astype(o_ref.dtype)


def _gattn_kernel(qmax_ref, q_ref, k_ref, v_ref, o_ref, vx_sc, kmax_sc, flag_sc, m_sc, acc_sc, *, tq, tk, g, n_ctx, n_lat):
    qi = pl.program_id(1)
    rows = g * tq

    @pl.when(qi == 0)
    def _():
        vx_sc[:, 0:HEAD_DIM] = v_ref[...]
        vx_sc[:, HEAD_DIM:] = jnp.ones((vx_sc.shape[0], HEAD_DIM), BF16)
        kf = k_ref[...].astype(F32)
        k2 = jnp.max(jnp.sum(kf * kf, axis=-1, keepdims=True), axis=0, keepdims=True)
        kmax = jnp.broadcast_to(jnp.sqrt(k2), kmax_sc.shape)
        kmax_sc[...] = kmax
        flag_sc[0] = (qmax_ref[0] * jnp.max(kmax) <= SOFTMAX_SAFE_LOG2).astype(jnp.int32)

    bounded = flag_sc[0] == 1
    ctx_only = qi * tq < n_ctx
    ctx_chunk = [(0, n_ctx)]
    all_chunks = ctx_chunk + [(n_ctx + c * tk, tk) for c in range(n_lat // tk)]

    def finish():
        acc = acc_sc[...]
        _unstack_store(o_ref, acc[:, :HEAD_DIM] / acc[:, HEAD_DIM:], g, tq)

    def attend_bounded(chunks):
        q = _stack_heads(q_ref, g)
        qf = q.astype(F32)
        bound = jnp.sqrt(jnp.sum(qf * qf, axis=-1, keepdims=True)) * kmax_sc[0:1, :]
        for n, (lo, size) in enumerate(chunks):
            s = _dot_nt(q, k_ref[lo:lo + size, :])
            p = jnp.exp2(s - _lane_repeat(bound, size // LANES))
            pv = _dot(p.astype(BF16), vx_sc[lo:lo + size, :])
            acc_sc[...] = pv if n == 0 else acc_sc[...] + pv
        finish()

    def attend_online(chunks):
        q = _stack_heads(q_ref, g)
        for n, (lo, size) in enumerate(chunks):
            s = _dot_nt(q, k_ref[lo:lo + size, :])
            mx = jnp.max(s, axis=-1, keepdims=True)
            if n == 0:
                m_new = jnp.broadcast_to(mx, (rows, LANES))
            else:
                m_prev = m_sc[...]
                m_new = jnp.maximum(m_prev, mx)
            p = jnp.exp2(s - _lane_repeat(m_new, size // LANES))
            pv = _dot(p.astype(BF16), vx_sc[lo:lo + size, :])
            if n == 0:
                acc_sc[...] = pv
            else:
                acc_sc[...] = _lane_repeat(jnp.exp2(m_prev - m_new), 2) * acc_sc[...] + pv
            m_sc[...] = m_new
        finish()

    for use_bound, attend in ((True, attend_bounded), (False, attend_online)):
        path = bounded if use_bound else jnp.logical_not(bounded)
        pl.when(path & ctx_only)(functools.partial(attend, ctx_chunk))
        pl.when(path & jnp.logical_not(ctx_only))(functools.partial(attend, all_chunks))


def _gattn(p, q_gain, n_ctx, mix):
    t = p.shape[0]
    qmax = (1.02 * HEAD_DIM ** 0.5 * ATTN_SCALE * LOG2E) * jnp.max(jnp.abs(q_gain)).reshape(1)
    hq = mix // HEAD_DIM
    g = hq // KV_HEADS
    n_lat = t - n_ctx
    tq = _pick(n_ctx, (256, 128))
    assert t % tq == 0
    tk = _pick(n_lat, (2048, 1024, 512, 256, 128))
    gw = g * HEAD_DIM
    off_q = mix // gw
    off_k = (mix + hq * HEAD_DIM) // HEAD_DIM
    off_v = off_k + KV_HEADS
    rows = g * tq
    vmem = (2 * tq * gw * 2 * 2 + 2 * 2 * t * HEAD_DIM * 2 + t * 2 * HEAD_DIM * 2 + rows * LANES * 4 * 4
            + rows * max(tk, n_ctx) * 4 * 3)
    return pl.pallas_call(
        functools.partial(_gattn_kernel, tq=tq, tk=tk, g=g, n_ctx=n_ctx, n_lat=n_lat),
        grid=(KV_HEADS, t // tq),
        in_specs=[pl.BlockSpec(memory_space=pltpu.SMEM),
                  pl.BlockSpec((tq, gw), lambda h, i: (i, off_q + h)),
                  pl.BlockSpec((t, HEAD_DIM), lambda h, i: (0, off_k + h)),
                  pl.BlockSpec((t, HEAD_DIM), lambda h, i: (0, off_v + h))],
        out_specs=pl.BlockSpec((tq, gw), lambda h, i: (i, h)),
        out_shape=jax.ShapeDtypeStruct((t, mix), BF16),
        scratch_shapes=[pltpu.VMEM((t, 2 * HEAD_DIM), BF16), pltpu.VMEM((8, LANES), F32), pltpu.SMEM((1,), jnp.int32),
                        pltpu.VMEM((rows, LANES), F32), pltpu.VMEM((rows, 2 * HEAD_DIM), F32)],
        compiler_params=_params(2, vmem),
        name="gattn",
    )(qmax, p, p, p)


def _wattn_kernel(*refs, tq, n_sub, g, n_ctx, span, t):
    q_refs, k_refs, v_refs = refs[0:KV_HEADS], refs[KV_HEADS:2 * KV_HEADS], refs[2 * KV_HEADS:3 * KV_HEADS]
    sink_ref, o_ref, vx_sc = refs[3 * KV_HEADS:]
    qi = pl.program_id(0)

    @pl.when(qi == 0)
    def _():
        for h in range(KV_HEADS):
            vx_sc[h, :, 0:HEAD_DIM] = v_refs[h][...]
            vx_sc[h, :, HEAD_DIM:] = jnp.ones((t, HEAD_DIM), BF16)

    for s in range(n_sub):
        for h in range(KV_HEADS):
            _wattn_head(qi * n_sub + s, q_refs[h].at[s * tq:(s + 1) * tq], k_refs[h], vx_sc.at[h], sink_ref.at[h],
                        o_ref.at[s * tq:(s + 1) * tq, h * g * HEAD_DIM:(h + 1) * g * HEAD_DIM],
                        tq=tq, g=g, n_ctx=n_ctx, span=span, t=t)


def _wattn_head(qi, q_ref, k_ref, vx_sc, sink_ref, o_ref, *, tq, g, n_ctx, span, t):
    q = _stack_heads(q_ref, g)
    rows = q.shape[0]
    start = pl.multiple_of(jnp.clip(qi * tq - WINDOW, 0, t - span), LANES)

    far = t + 4 * WINDOW
    qrow = qi * tq + lax.broadcasted_iota(jnp.int32, (tq, 1), 0)
    qrow = jnp.concatenate([jnp.where(qrow >= n_ctx, qrow, -far)] * g, axis=0)
    krow = start + lax.broadcasted_iota(jnp.int32, (1, span), 1)
    krow = jnp.where(krow >= n_ctx, krow, far)
    keep = jnp.abs(krow - qrow) <= WINDOW

    s_c = _dot_nt(q, k_ref[0:n_ctx, :])
    s_w = jnp.where(keep, _dot_nt(q, k_ref[pl.ds(start, span), :]), NEG_INF)
    sink = sink_ref[...] * LOG2E
    mx = jnp.maximum(jnp.maximum(jnp.max(s_c, axis=-1, keepdims=True), jnp.max(s_w, axis=-1, keepdims=True)), sink)
    m = jnp.broadcast_to(mx, (rows, LANES))
    p_c = jnp.exp2(s_c - _lane_repeat(m, n_ctx // LANES))
    p_w = jnp.exp2(s_w - _lane_repeat(m, span // LANES))
    acc = _dot(p_c.astype(BF16), vx_sc[0:n_ctx, :]) + _dot(p_w.astype(BF16), vx_sc[pl.ds(start, span), :])
    denom = acc[:, HEAD_DIM:] + jnp.exp2(sink - m)
    _unstack_store(o_ref, acc[:, :HEAD_DIM] / denom, g, tq)


def _wattn(p, sink_l, n_ctx, mix):
    t = p.shape[0]
    hq = mix // HEAD_DIM
    g = hq // KV_HEADS
    tq = _pick(n_ctx, (256, 128))
    span = tq + 2 * WINDOW
    assert t % tq == 0 and t >= span
    gw = g * HEAD_DIM
    base = mix + hq * HEAD_DIM + 2 * KV_HEADS * HEAD_DIM
    off_q = base // gw
    off_k = (base + hq * HEAD_DIM) // HEAD_DIM
    off_v = off_k + KV_HEADS
    rows = g * tq
    sink_rows = jnp.repeat(sink_l.reshape(KV_HEADS, g), tq, axis=1).reshape(KV_HEADS, rows, 1)
    n_sub = 3 if (t // tq) % 3 == 0 else 1
    vmem = KV_HEADS * (2 * n_sub * tq * gw * 2 * 2 + 2 * 2 * t * HEAD_DIM * 2 + 2 * rows * LANES * 4
                       + n_sub * rows * (span + n_ctx) * 4 * 3 + t * 2 * HEAD_DIM * 2)
    heads = range(KV_HEADS)
    return pl.pallas_call(
        functools.partial(_wattn_kernel, tq=tq, n_sub=n_sub, g=g, n_ctx=n_ctx, span=span, t=t),
        grid=(t // (tq * n_sub),),
        in_specs=([pl.BlockSpec((tq * n_sub, gw), lambda i, h=h: (i, off_q + h)) for h in heads]
                  + [pl.BlockSpec((t, HEAD_DIM), lambda i, h=h: (0, off_k + h)) for h in heads]
                  + [pl.BlockSpec((t, HEAD_DIM), lambda i, h=h: (0, off_v + h)) for h in heads]
                  + [pl.BlockSpec((KV_HEADS, rows, 1), lambda i: (0, 0, 0))]),
        out_specs=pl.BlockSpec((tq * n_sub, mix), lambda i: (i, 0)),
        out_shape=jax.ShapeDtypeStruct((t, mix), BF16),
        scratch_shapes=[pltpu.VMEM((KV_HEADS, t, 2 * HEAD_DIM), BF16)],
        compiler_params=_params(1, vmem),
        name="wattn",
    )(*([p] * (3 * KV_HEADS)), sink_rows)


def _merge_kernel(u_ref, up_ref, un_ref, og_ref, ow_ref, g0_ref, g1_ref, g2_ref, pw_ref, ps_ref, wb_ref,
                  z_ref, ext_sc, pool_sc, *, tm, n_ctx, t, mix):
    i = pl.program_id(0)
    j = pl.program_id(1)
    gw = mix // len(POOL_WINDOWS)

    def pool(clipped):
        ext_sc[0:POOL_HALO, :] = up_ref[...].astype(F32)
        ext_sc[POOL_HALO:POOL_HALO + tm, :] = u_ref[...].astype(F32)
        ext_sc[POOL_HALO + tm:, :] = un_ref[...].astype(F32)
        r = i * tm + lax.broadcasted_iota(jnp.int32, (tm, 1), 0)
        r_ctx = r < n_ctx
        for gi, w in enumerate(POOL_WINDOWS):
            c0, c1 = gi * gw, (gi + 1) * gw
            tot = jnp.zeros((tm, gw), F32)
            cnt = jnp.zeros((tm, 1), F32)
            for off in range(-((w - 1) // 2), w // 2 + 1):
                part = ext_sc[POOL_HALO + off:POOL_HALO + off + tm, c0:c1]
                if clipped:
                    rr = r + off
                    ok = (rr >= 0) & (rr < t) & ((rr < n_ctx) == r_ctx)
                    part = jnp.where(ok, part, 0.0)
                    cnt = cnt + ok.astype(F32)
                tot = tot + part
            mean = tot / cnt if clipped else tot * (1.0 / w)
            dlt = mean - ext_sc[POOL_HALO:POOL_HALO + tm, c0:c1]
            y = _dot(dlt.astype(BF16), pw_ref[gi].astype(BF16)) * ps_ref[:, c0:c1]
            pool_sc[:, c0:c1] = y.astype(BF16)

    reach = max(POOL_WINDOWS) // 2
    interior = (i * tm - reach >= n_ctx) & ((i + 1) * tm + reach <= t)
    pl.when((j == 0) & interior)(functools.partial(pool, False))
    pl.when((j == 0) & jnp.logical_not(interior))(functools.partial(pool, True))

    def gate(g_ref):
        return _sigmoid_tanh(g_ref[...].astype(F32))

    z = gate(g0_ref) * _dot(pool_sc[...], wb_ref[0].astype(BF16))
    z = z + gate(g1_ref) * _dot(og_ref[...], wb_ref[1].astype(BF16))
    z = z + gate(g2_ref) * _dot(ow_ref[...], wb_ref[2].astype(BF16))
    z_ref[...] = z.astype(BF16)


def _merge(p, og, ow, pool_w, pool_scale, w_branch, layer, n_ctx, d):
    t = p.shape[0]
    mix = d // 2
    tm = _largest_tile(t, 1100, POOL_HALO)
    tn = _pick(d, (512, 256, 128))
    hb = tm // POOL_HALO
    n_hb = t // POOL_HALO
    gate0 = (p.shape[1] - 3 * d) // tn
    gs = pool_w.shape[-1]
    vmem = (2 * 3 * tm * mix * 2 + 2 * 3 * tm * tn * 2 + 2 * 3 * mix * tn * 4 + 3 * mix * tn * 2
            + (tm + 2 * POOL_HALO) * mix * 4 + tm * mix * 2 + 2 * tm * tn * 2 + 4 * tm * tn * 4
            + 2 * len(POOL_WINDOWS) * gs * gs * 4 + 6 * tm * gs * 4)
    return pl.pallas_call(
        functools.partial(_merge_kernel, tm=tm, n_ctx=n_ctx, t=t, mix=mix),
        grid=(t // tm, d // tn),
        in_specs=[pl.BlockSpec((tm, mix), lambda i, j: (i, 0)),
                  pl.BlockSpec((POOL_HALO, mix), lambda i, j: (jnp.maximum(i * hb - 1, 0), 0)),
                  pl.BlockSpec((POOL_HALO, mix), lambda i, j: (jnp.minimum((i + 1) * hb, n_hb - 1), 0)),
                  pl.BlockSpec((tm, mix), lambda i, j: (i, 0)),
                  pl.BlockSpec((tm, mix), lambda i, j: (i, 0)),
                  pl.BlockSpec((tm, tn), lambda i, j: (i, gate0 + j)),
                  pl.BlockSpec((tm, tn), lambda i, j: (i, gate0 + d // tn + j)),
                  pl.BlockSpec((tm, tn), lambda i, j: (i, gate0 + 2 * (d // tn) + j)),
                  pl.BlockSpec((None, len(POOL_WINDOWS), gs, gs), lambda i, j: (layer, 0, 0, 0)),
                  pl.BlockSpec((1, mix), lambda i, j: (0, 0)),
                  pl.BlockSpec((None, 3, mix, tn), lambda i, j: (layer, 0, 0, j))],
        out_specs=pl.BlockSpec((tm, tn), lambda i, j: (i, j)),
        out_shape=jax.ShapeDtypeStruct((t, d), BF16),
        scratch_shapes=[pltpu.VMEM((tm + 2 * POOL_HALO, mix), F32), pltpu.VMEM((tm, mix), BF16)],
        compiler_params=_params(2, vmem),
        name="merge",
    )(p, p, p, og, ow, p, p, p, pool_w, pool_scale, w_branch)


W_CHUNK = 256


def _route(h, rwt_ref, rb_ref, ints_ref, ws_ref, cnt_ref, carry_sc, *, tm, n_exp):
    scores = _sigmoid(_dot_nt(rwt_ref[...].astype(BF16), h.astype(BF16)))
    sel = scores + rb_ref[...]
    per = n_exp // N_GROUPS
    sub = lax.broadcasted_iota(jnp.int32, (per, tm), 0)

    def top2(v):
        m1 = jnp.max(v, axis=0, keepdims=True)
        i1 = jnp.min(jnp.where(v == m1, sub, per), axis=0, keepdims=True)
        rest = jnp.where(sub == i1, -jnp.inf, v)
        m2 = jnp.max(rest, axis=0, keepdims=True)
        i2 = jnp.min(jnp.where(rest == m2, sub, per), axis=0, keepdims=True)
        return m1 + m2, i1, i2

    tops = [top2(sel[gi * per:(gi + 1) * per, :]) for gi in range(N_GROUPS)]
    best, l1, l2 = tops[0]
    grp = jnp.zeros((1, tm), jnp.int32)
    for gi in range(1, N_GROUPS):
        gs, a1, a2 = tops[gi]
        better = gs > best
        best = jnp.where(better, gs, best)
        grp = jnp.where(better, gi, grp)
        l1 = jnp.where(better, a1, l1)
        l2 = jnp.where(better, a2, l2)

    hot1 = [(grp == gi) & (sub == l1) for gi in range(N_GROUPS)]
    hot2 = [(grp == gi) & (sub == l2) for gi in range(N_GROUPS)]
    assign = jnp.concatenate([(a | b).astype(F32) for a, b in zip(hot1, hot2)], axis=0)

    before = (lax.broadcasted_iota(jnp.int32, (tm, tm), 0) < lax.broadcasted_iota(jnp.int32, (tm, tm), 1))
    pos = _dot(assign.astype(BF16), before.astype(F32).astype(BF16)) + carry_sc[...]

    def pick(hots, val):
        return sum(jnp.sum(jnp.where(hots[gi], val[gi * per:(gi + 1) * per, :], 0.0), axis=0, keepdims=True)
                   for gi in range(N_GROUPS))

    s1 = pick(hot1, scores)
    s2 = pick(hot2, scores)
    r1 = pick(hot1, pos)
    r2 = pick(hot2, pos)
    tot = s1 + s2
    ints_ref[0:1, :] = grp * per + l1
    ints_ref[1:2, :] = grp * per + l2
    ints_ref[2:3, :] = r1.astype(jnp.int32)
    ints_ref[3:4, :] = r2.astype(jnp.int32)
    ints_ref[4:8, :] = jnp.zeros((4, tm), jnp.int32)
    ws_ref[0:1, :] = s1 / tot
    ws_ref[1:2, :] = s2 / tot
    ws_ref[2:8, :] = jnp.zeros((6, tm), F32)
    carry_sc[...] = carry_sc[...] + jnp.sum(assign, axis=1, keepdims=True)
    cnt_ref[...] = jnp.broadcast_to(carry_sc[...], cnt_ref.shape)


def _outproj_router_kernel(z_ref, w_hbm, xs_ref, gate_ref, shift_ref, scale_ref, g_ref, rwt_ref, rb_ref,
                           o_ref, h_ref, ints_ref, ws_ref, cnt_ref, w_sc, stage, sem, carry_sc,
                           *, tm, n_ctx, n_exp, layer):
    i = pl.program_id(0)
    d = w_sc.shape[0]

    @pl.when(i == 0)
    def _():
        carry_sc[...] = jnp.zeros(carry_sc.shape, F32)
        n_chunks = d // W_CHUNK

        def chunk(c):
            return pltpu.make_async_copy(w_hbm.at[layer, :, c * W_CHUNK:(c + 1) * W_CHUNK], stage.at[c % 2], sem.at[c % 2])

        chunk(0).start()
        for c in range(n_chunks):
            if c + 1 < n_chunks:
                chunk(c + 1).start()
            chunk(c).wait()
            w_sc[:, c * W_CHUNK:(c + 1) * W_CHUNK] = stage[c % 2].astype(BF16)

    gate = _row_select(i * tm, tm, n_ctx, gate_ref)
    x = xs_ref[...] + gate * _dot(z_ref[...], w_sc[...])
    o_ref[...] = x
    shift = _row_select(i * tm, tm, n_ctx, shift_ref)
    scale = _row_select(i * tm, tm, n_ctx, scale_ref)
    h = _norm_modulate(x, g_ref, shift, scale)
    h_ref[...] = h
    _route(h, rwt_ref, rb_ref, ints_ref, ws_ref, cnt_ref, carry_sc, tm=tm, n_exp=n_exp)


def _outproj_router(z, w_out, layer, xs, mod_l, g2, router_w, router_b, n_ctx):
    t, d = xs.shape
    n_exp = router_w.shape[1]
    tm = _largest_tile(t, 400, LANES)
    vmem = (d * d * 2 + 2 * d * W_CHUNK * 4 + 2 * tm * d * 2 + 6 * tm * d * 4 + 6 * tm * d * 4 + 3 * tm * tm * 4
            + 2 * n_exp * d * 4)
    row = lambda i: (i, 0)
    return pl.pallas_call(
        functools.partial(_outproj_router_kernel, tm=tm, n_ctx=n_ctx, n_exp=n_exp, layer=layer),
        grid=(t // tm,),
        in_specs=[pl.BlockSpec((tm, d), row),
                  pl.BlockSpec(memory_space=pl.ANY),
                  pl.BlockSpec((tm, d), row),
                  pl.BlockSpec((8, d), lambda i: (0, 2)),
                  pl.BlockSpec((8, d), lambda i: (0, 3)),
                  pl.BlockSpec((8, d), lambda i: (0, 4)),
                  pl.BlockSpec((1, d), lambda i: (0, 0)),
                  pl.BlockSpec((n_exp, d), lambda i: (0, 0)),
                  pl.BlockSpec((n_exp, 1), lambda i: (0, 0))],
        out_specs=[pl.BlockSpec((tm, d), row),
                   pl.BlockSpec((tm, d), row),
                   pl.BlockSpec((8, tm), lambda i: (0, i)),
                   pl.BlockSpec((8, tm), lambda i: (0, i)),
                   pl.BlockSpec((n_exp, LANES), lambda i: (0, 0))],
        out_shape=[jax.ShapeDtypeStruct((t, d), F32),
                   jax.ShapeDtypeStruct((t, d), F32),
                   jax.ShapeDtypeStruct((8, t), jnp.int32),
                   jax.ShapeDtypeStruct((8, t), F32),
                   jax.ShapeDtypeStruct((n_exp, LANES), F32)],
        scratch_shapes=[pltpu.VMEM((d, d), BF16), pltpu.VMEM((2, d, W_CHUNK), F32), pltpu.SemaphoreType.DMA((2,)),
                        pltpu.VMEM((n_exp, 1), F32)],
        compiler_params=_params(1, vmem),
        name="outproj_router",
    )(z, w_out, xs, mod_l, mod_l, mod_l, g2, router_w.T, router_b.reshape(n_exp, 1))


def _row_copy(src_ref, src_row, dst_ref, dst_row, sem):
    return pltpu.make_async_copy(src_ref.at[pl.ds(src_row, 1)], dst_ref.at[pl.ds(dst_row, 1)], sem)


def _dispatch_kernel(dst_ref, h_ref, xs_hbm, sem, *, tm):
    def issue(r8, carry):
        base = pl.multiple_of(r8 * DMA_UNROLL, DMA_UNROLL)
        for j in range(DMA_UNROLL):
            for k in range(2):
                _row_copy(h_ref, base + j, xs_hbm, dst_ref[0, 0, k * tm + base + j], sem).start(priority=k)
        return carry

    lax.fori_loop(0, tm // DMA_UNROLL, issue, 0)
    for k in range(2):
        pltpu.make_async_copy(h_ref, xs_hbm.at[pl.ds(0, tm)], sem).wait()


def _dest_blocks(ints, pad_start, tm):
    t = ints.shape[1]
    experts = jnp.arange(pad_start.shape[0], dtype=jnp.int32)
    start = jnp.sum(jnp.where(ints[0:2, :, None] == experts, pad_start, 0), axis=-1)
    dest = start + ints[2:4]
    return dest.reshape(2, t // tm, tm).transpose(1, 0, 2).reshape(t // tm, 1, 2 * tm)


def _dispatch(h2, dest_blocks, n_rows, tm):
    t, d = h2.shape
    return pl.pallas_call(
        functools.partial(_dispatch_kernel, tm=tm),
        grid=(t // tm,),
        in_specs=[pl.BlockSpec((1, 1, 2 * tm), lambda i: (i, 0, 0), memory_space=pltpu.SMEM),
                  pl.BlockSpec((tm, d), lambda i: (i, 0))],
        out_specs=pl.BlockSpec(memory_space=pl.ANY),
        out_shape=jax.ShapeDtypeStruct((n_rows, d), F32),
        scratch_shapes=[pltpu.SemaphoreType.DMA(())],
        compiler_params=_params(1, 2 * tm * d * 4),
        name="dispatch",
    )(dest_blocks, h2)


def _ffn_kernel(blk_valid_ref, blk_seg_ref, blk_first_ref, seg_e_ref, cnt_ref, x_ref, wg_hbm, wu_hbm, wd_hbm, y_ref,
                wg_st, wu_st, wd_st, wg_sc, wu_sc, wd_sc, sem_g, sem_u, sem_d, *, layer):
    b = pl.program_id(0)
    n_seg = cnt_ref[1]

    def copies(k):
        e = seg_e_ref[k]
        return [pltpu.make_async_copy(w.at[layer, e], st, sm)
                for w, st, sm in ((wg_hbm, wg_st, sem_g), (wu_hbm, wu_st, sem_u), (wd_hbm, wd_st, sem_d))]

    def x_block():
        rows = lax.broadcasted_iota(jnp.int32, (x_ref.shape[0], 1), 0)
        return jnp.where(rows < blk_valid_ref[b], x_ref[...], 0.0).astype(BF16)

    def finish(gte, up):
        hidden = (gte * _sigmoid(gte) * up).astype(BF16)
        y_ref[...] = _dot(hidden, wd_sc[...])

    @pl.when(b < cnt_ref[0])
    def _():
        @pl.when(b == 0)
        def _():
            for cp in copies(0):
                cp.start()

        @pl.when(blk_first_ref[b] == 1)
        def _():
            k = blk_seg_ref[b]
            for cp, nxt, st, dst in zip(copies(k), copies(k + 1), (wg_st, wu_st, wd_st), (wg_sc, wu_sc, wd_sc)):
                cp.wait()
                dst[...] = st[...].astype(BF16)
                pl.when(k + 1 < n_seg)(nxt.start)

        x = x_block()
        finish(_dot(x, wg_sc[...]), _dot(x, wu_sc[...]))


def _ffn(xs_sorted, plan, w_gate, w_up, w_down, layer):
    n_rows, d = xs_sorted.shape
    ff = w_gate.shape[-1]
    blk = MOE_BLOCK
    n_blk = n_rows // blk
    prefetch = (plan["blk_valid"], plan["blk_seg"], plan["blk_first"], plan["seg_e"], plan["cnt"])
    row_map = lambda b, v, s, f, e, c: (jnp.minimum(b, c[0] - 1), 0)
    hbm = pl.BlockSpec(memory_space=pl.ANY)
    dma = pltpu.SemaphoreType.DMA(())
    grid_spec = pltpu.PrefetchScalarGridSpec(
        num_scalar_prefetch=len(prefetch),
        grid=(n_blk,),
        in_specs=[pl.BlockSpec((blk, d), row_map), hbm, hbm, hbm],
        out_specs=pl.BlockSpec((blk, d), row_map),
        scratch_shapes=[pltpu.VMEM((d, ff), F32), pltpu.VMEM((d, ff), F32), pltpu.VMEM((ff, d), F32),
                        pltpu.VMEM((d, ff), BF16), pltpu.VMEM((d, ff), BF16), pltpu.VMEM((ff, d), BF16),
                        dma, dma, dma],
    )
    vmem = 3 * d * ff * (4 + 2) + 4 * blk * d * 4 + blk * d * 2 + 3 * blk * ff * 4 + blk * d * 4
    return pl.pallas_call(
        functools.partial(_ffn_kernel, layer=layer),
        grid_spec=grid_spec,
        out_shape=jax.ShapeDtypeStruct((n_rows, d), F32),
        compiler_params=pltpu.CompilerParams(dimension_semantics=("arbitrary",), vmem_limit_bytes=min(vmem + (6 << 20), VMEM_CAP)),
        name="ffn",
    )(*prefetch, xs_sorted, w_gate, w_up, w_down)


def _combine_kernel(src_ref, nxt_ref, y_hbm, xs_ref, ws_ref, gate_ref, fg_ref, o_ref, ybuf, sem, *, tm, n_ctx, final):
    i = pl.program_id(0)
    slot = i % 2

    def gather(idx_ref, into):
        def issue(r8, carry):
            base = pl.multiple_of(r8 * DMA_UNROLL, DMA_UNROLL)
            for j in range(DMA_UNROLL):
                for k in range(2):
                    _row_copy(y_hbm, idx_ref[0, 0, k * tm + base + j], ybuf.at[into, k], base + j, sem.at[into]).start(priority=k)
            return carry

        lax.fori_loop(0, tm // DMA_UNROLL, issue, 0)

    @pl.when(i == 0)
    def _():
        gather(src_ref, 0)

    @pl.when(i + 1 < pl.num_programs(0))
    def _():
        gather(nxt_ref, 1 - slot)

    for k in range(2):
        pltpu.make_async_copy(y_hbm.at[pl.ds(0, tm)], ybuf.at[slot, k], sem.at[slot]).wait()

    wcol = jnp.transpose(ws_ref[...])
    y = wcol[:, 0:1] * ybuf[slot, 0] + wcol[:, 1:2] * ybuf[slot, 1]
    x = xs_ref[...] + _row_select(i * tm, tm, n_ctx, gate_ref) * y
    if final:
        ms = jnp.mean(x * x, axis=-1, keepdims=True)
        x = x * lax.rsqrt(ms + RMS_EPS) * fg_ref[...]
    o_ref[...] = x


def _combine(y_sorted, dest_blocks, xs, ws, mod_l, final_g, n_ctx, tm, final):
    t, d = xs.shape
    if final:
        skip = n_ctx // tm
        out_rows = t - n_ctx
        out_map = lambda i: (jnp.maximum(i - skip, 0), 0)
    else:
        out_rows = t
        out_map = lambda i: (i, 0)
    return pl.pallas_call(
        functools.partial(_combine_kernel, tm=tm, n_ctx=n_ctx, final=final),
        grid=(t // tm,),
        in_specs=[pl.BlockSpec((1, 1, 2 * tm), lambda i: (i, 0, 0), memory_space=pltpu.SMEM),
                  pl.BlockSpec((1, 1, 2 * tm), lambda i: (jnp.minimum(i + 1, t // tm - 1), 0, 0), memory_space=pltpu.SMEM),
                  pl.BlockSpec(memory_space=pl.ANY),
                  pl.BlockSpec((tm, d), lambda i: (i, 0)),
                  pl.BlockSpec((8, tm), lambda i: (0, i)),
                  pl.BlockSpec((8, d), lambda i: (0, 5)),
                  pl.BlockSpec((1, d), lambda i: (0, 0))],
        out_specs=pl.BlockSpec((tm, d), out_map),
        out_shape=jax.ShapeDtypeStruct((out_rows, d), F32),
        scratch_shapes=[pltpu.VMEM((2, 2, tm, d), F32), pltpu.SemaphoreType.DMA((2,))],
        compiler_params=_params(1, 4 * tm * d * 4 + 4 * tm * d * 4 + 4 * tm * d * 4),
        name="combine",
    )(dest_blocks, dest_blocks, y_sorted, xs, ws, mod_l, final_g)


def _rope_tables(n_lat, n_ctx):
    rows = n_lat // GRID_W
    row_ids = np.repeat(np.arange(rows, dtype=np.float32), GRID_W)
    col_ids = np.tile(np.arange(GRID_W, dtype=np.float32), rows)
    n_freq = HEAD_DIM // 4
    inv = (np.float32(ROPE_THETA) ** (-np.arange(n_freq, dtype=np.float32) / np.float32(n_freq))).astype(np.float32)
    ar = row_ids[:, None] * inv
    ac = col_ids[:, None] * inv
    ang = np.concatenate([ar, ar, ac, ac], axis=-1).astype(np.float32)
    sign = np.where((np.arange(HEAD_DIM) % (HEAD_DIM // 2)) < HEAD_DIM // 4, -1.0, 1.0).astype(np.float32)
    cos = np.concatenate([np.ones((n_ctx, HEAD_DIM), np.float32), np.cos(ang)], axis=0)
    sin = np.concatenate([np.zeros((n_ctx, HEAD_DIM), np.float32), np.sin(ang) * sign], axis=0)
    return jnp.asarray(cos, F32), jnp.asarray(sin, F32)


def _moe_plan(counts, n_blk):
    blk = MOE_BLOCK
    n_exp = counts.shape[0]
    padded = (counts + blk - 1) // blk * blk
    pad_end = jnp.cumsum(padded)
    pad_start = pad_end - padded
    n_used = jnp.maximum(pad_end[-1] // blk, 1)
    b = jnp.minimum(jnp.arange(n_blk, dtype=jnp.int32), n_used - 1)
    blk_e = jnp.sum((pad_end[None, :] <= (b * blk)[:, None]).astype(jnp.int32), axis=1)
    blk_e = jnp.minimum(blk_e, n_exp - 1)
    blk_valid = jnp.clip(counts[blk_e] - (b * blk - pad_start[blk_e]), 0, blk).astype(jnp.int32)
    live = counts > 0
    seg_of_e = jnp.cumsum(live.astype(jnp.int32)) - 1
    n_seg = jnp.maximum(seg_of_e[-1] + 1, 1)
    ks = jnp.arange(n_exp, dtype=jnp.int32)
    seg_e = jnp.sum(jnp.where(live[None, :] & (seg_of_e[None, :] == ks[:, None]), ks[None, :], 0), axis=1)
    blk_first = jnp.concatenate([jnp.ones((1,), jnp.int32), (blk_e[1:] != blk_e[:-1]).astype(jnp.int32)])
    return {
        "pad_start": pad_start.astype(jnp.int32),
        "blk_valid": blk_valid,
        "blk_seg": seg_of_e[blk_e].astype(jnp.int32),
        "blk_first": blk_first,
        "seg_e": jnp.concatenate([seg_e, seg_e[-1:]]).astype(jnp.int32),
        "cnt": jnp.stack([n_used, n_seg]).astype(jnp.int32),
    }


def kernel(x, c, ctx, c_ctx, w_ada, b_ada, norm1_g, norm2_g, w_in, pool_w, pool_scale, q_norm_g, k_norm_g,
           sink, w_branch, w_out, router_w, router_b, w_gate, w_up, w_down, final_g):
    assert x.shape[0] == 1, "single-sequence kernel"
    n_lat, d = x.shape[1], x.shape[2]
    n_ctx = ctx.shape[1]
    t = n_ctx + n_lat
    mix = d // 2
    depth = w_in.shape[0]
    n_exp = router_w.shape[1]
    tok = _pick(n_ctx, (256, 128))
    assert (2 * t) % MOE_BLOCK == 0
    n_blk = 2 * t // MOE_BLOCK + n_exp
    cos, sin_signed = _rope_tables(n_lat, n_ctx)
    mod = _ada(c, c_ctx, w_ada, b_ada)
    for l in range(depth):
        last = l == depth - 1
        if l == 0:
            xs, h1 = _stack_norm1(ctx[0], x[0], mod[0], norm1_g[0][None])
        else:
            h1 = _norm1(xs, mod[l], norm1_g[l][None], n_ctx)
        p = _inproj(h1, w_in, l, cos, sin_signed, q_norm_g[l][None], k_norm_g[l][None])
        og = _gattn(p, q_norm_g[l], n_ctx, mix)
        ow = _wattn(p, sink[l], n_ctx, mix)
        z = _merge(p, og, ow, pool_w, pool_scale[l][None], w_branch, l, n_ctx, d)
        xs, h2, ints, ws, cnt = _outproj_router(z, w_out, l, xs, mod[l], norm2_g[l][None], router_w, router_b, n_ctx)
        plan = _moe_plan(cnt[:, 0].astype(jnp.int32), n_blk)
        dest_blocks = _dest_blocks(ints, plan["pad_start"], tok)
        xs_sorted = _dispatch(h2, dest_blocks, n_blk * MOE_BLOCK, tok)
        y_sorted = _ffn(xs_sorted, plan, w_gate, w_up, w_down, l)
        xs = _combine(y_sorted, dest_blocks, xs, ws, mod[l], final_g[None], n_ctx, tok, last)
    return xs[None]
```
